```python
import math
import jax, jax.numpy as jnp
from jax import lax
import numpy as np

D_MODEL = 2048
BATCH = 16
SEQ = 256
DEPTH = 4
DEC_BATCH = 2
DEC_SEQ = 2048
PAST_LEN = 256

GRID_W = 64
N_MIXERS = 3
N_LAYERS_A = (DEPTH + 2) // 3
N_LAYERS_B = (DEPTH + 1) // 3
N_LAYERS_C = DEPTH // 3
N_DENSE = (DEPTH + 1) // 2
N_MOE = DEPTH // 2

HG_HEADS = 16
HG_DK = D_MODEL // HG_HEADS
HG_DV = D_MODEL // HG_HEADS
HG_CHUNK = 16
SSD_INNER = 2 * D_MODEL
SSD_HEADDIM = 64
SSD_HEADS = SSD_INNER // SSD_HEADDIM
SSD_GROUPS = 8
SSD_STATE = 128
SSD_CONV = 5
SSD_CHUNK = 128
AT_HEADS = 16
AT_KV_HEADS = 4
AT_HEAD_DIM = D_MODEL // AT_HEADS
Q_BLOCK = 128
ROPE_THETA = 10000.0
FF_DIM = 5632
N_EXPERTS = 8
TOP_K = 2
EXPERT_DIM = 1408
NORM_EPS = 1e-6
GATE_FLOOR = 1e-30

kernel_name = 'hybrid_diffusion_hgrn2_ssd_gqa_step'

F32 = jnp.float32


def rmsnorm(x, w):
    xf = x.astype(F32)
    y = xf * lax.rsqrt(jnp.mean(xf * xf, axis=-1, keepdims=True) + NORM_EPS)
    return y.astype(x.dtype) * w


def rope_1d(x, pos):
    half = x.shape[-1] // 2
    inv = ROPE_THETA ** (-jnp.arange(half, dtype=F32) / half)
    ang = pos.astype(F32)[:, None] * inv[None, :]
    cos = jnp.cos(ang)[:, None, :]
    sin = jnp.sin(ang)[:, None, :]
    xf = x.astype(F32)
    x1, x2 = xf[..., :half], xf[..., half:]
    return jnp.concatenate([x1 * cos - x2 * sin, x2 * cos + x1 * sin], axis=-1).astype(x.dtype)


def rope_2d(x):
    n_tok = x.shape[1]
    rows = n_tok // GRID_W
    row = jnp.repeat(jnp.arange(rows), GRID_W)
    col = jnp.tile(jnp.arange(GRID_W), rows)
    half = x.shape[-1] // 2
    return jnp.concatenate([rope_1d(x[..., :half], row), rope_1d(x[..., half:], col)], axis=-1)


def rev(a):
    return jnp.flip(a, axis=1)


def masked_exp(mask, d):
    return jnp.where(mask, jnp.exp(jnp.where(mask, d, 0.0)), 0.0)


def gla_chunked(q, k, v, logf, s0):
    bsz, n_tok, n_h, _ = q.shape
    d_v = v.shape[-1]
    nc = n_tok // HG_CHUNK
    q, k, v, logf = [a.astype(F32).reshape(bsz, nc, HG_CHUNK, n_h, -1) for a in (q, k, v, logf)]
    b = jnp.cumsum(logf, axis=2)
    causal = jnp.tril(jnp.ones((HG_CHUNK, HG_CHUNK), bool))[None, None, :, :, None, None]
    diff = b[:, :, :, None] - b[:, :, None, :]
    decay = masked_exp(causal, diff)
    scores = jnp.einsum('bcthk,bcshk,bctshk->bchts', q, k, decay)
    o_intra = jnp.einsum('bchts,bcshv->bcthv', scores, v)
    b_last = b[:, :, -1]
    u = jnp.einsum('bcshk,bcshv->bchkv', k * jnp.exp(b_last[:, :, None] - b), v)

    def step(s, inp):
        g, uc = inp
        return g[..., None] * s + uc, s

    s_fin, s_prev = lax.scan(step, s0.astype(F32),
                             (jnp.moveaxis(jnp.exp(b_last), 1, 0), jnp.moveaxis(u, 1, 0)))
    o_inter = jnp.einsum('bcthk,cbhkv->bcthv', q * jnp.exp(b), s_prev)
    return (o_intra + o_inter).reshape(bsz, n_tok, n_h, d_v), s_fin


def hgrn_lower_bounds(logits):
    pr = jax.nn.softmax(logits.astype(F32), axis=0)
    return jnp.cumsum(pr, axis=0) - pr[0]


def hgrn2_mixer(h, w_in, lb, onorm, w_o, s0_f, s0_b):
    bsz, n_tok, _ = h.shape
    hk = HG_HEADS * HG_DK
    hv = HG_HEADS * HG_DV
    q, zf_f, zf_b, inp, g = jnp.split(jnp.matmul(h, w_in), [hk, 2 * hk, 3 * hk, 3 * hk + hv], axis=-1)
    q = q.reshape(bsz, n_tok, HG_HEADS, HG_DK)
    inp = inp.reshape(bsz, n_tok, HG_HEADS, HG_DV)
    lbh = lb.reshape(HG_HEADS, HG_DK)

    def gates(zf):
        zf = zf.reshape(bsz, n_tok, HG_HEADS, HG_DK).astype(F32)
        f = lbh + (1.0 - lbh) * jax.nn.sigmoid(zf)
        logf = jnp.log(jnp.maximum(f, GATE_FLOOR))
        key = (1.0 - lbh) * jax.nn.sigmoid(-zf)
        return logf, key

    logf_f, k_f = gates(zf_f)
    logf_b, k_b = gates(zf_b)
    o_f, s_f = gla_chunked(q, k_f, inp, logf_f, s0_f)
    o_b, s_b = gla_chunked(rev(q), rev(k_b), rev(inp), rev(logf_b), s0_b)
    o = (o_f + rev(o_b)).astype(h.dtype)
    o = rmsnorm(o, onorm) * jax.nn.silu(g.reshape(bsz, n_tok, HG_HEADS, HG_DV))
    return jnp.matmul(o.reshape(bsz, n_tok, hv), w_o), s_f.astype(h.dtype), s_b.astype(h.dtype)


def dwconv_centred(x, w, b):
    n_ch = x.shape[-1]
    y = lax.conv_general_dilated(x, w[:, None, :].astype(x.dtype), window_strides=(1,),
                                 padding=[(SSD_CONV // 2, SSD_CONV // 2)],
                                 dimension_numbers=('NWC', 'WIO', 'NWC'),
                                 feature_group_count=n_ch)
    return y + b


def ssd_chunked(x, dt, a_head, bm, cm, h0):
    bsz, n_tok, n_h, d_p = x.shape
    n_g, d_n = bm.shape[2], bm.shape[3]
    hpg = n_h // n_g
    nc = n_tok // SSD_CHUNK
    dt = dt.astype(F32)
    a = (dt * a_head.astype(F32)).reshape(bsz, nc, SSD_CHUNK, n_h)
    xdt = (x.astype(F32) * dt[..., None]).reshape(bsz, nc, SSD_CHUNK, n_g, hpg, d_p)
    bc = bm.astype(F32).reshape(bsz, nc, SSD_CHUNK, n_g, d_n)
    cc = cm.astype(F32).reshape(bsz, nc, SSD_CHUNK, n_g, d_n)
    acs = jnp.cumsum(a, axis=2)
    causal = jnp.tril(jnp.ones((SSD_CHUNK, SSD_CHUNK), bool))[None, None, :, :, None]
    seg = acs[:, :, :, None, :] - acs[:, :, None, :, :]
    lmat = masked_exp(causal, seg).reshape(bsz, nc, SSD_CHUNK, SSD_CHUNK, n_g, hpg)
    cb = jnp.einsum('bctgn,bcsgn->bctsg', cc, bc)
    y_intra = jnp.einsum('bctsg,bctsgh,bcsghp->bctghp', cb, lmat, xdt)
    decay_end = jnp.exp(acs[:, :, -1:, :] - acs).reshape(bsz, nc, SSD_CHUNK, n_g, hpg)
    states = jnp.einsum('bcsgn,bcsgh,bcsghp->bcghpn', bc, decay_end, xdt)
    chunk_decay = jnp.exp(acs[:, :, -1, :]).reshape(bsz, nc, n_g, hpg)

    def step(hs, inp):
        dec, st = inp
        return dec[..., None, None] * hs + st, hs

    h_fin, h_prev = lax.scan(step, h0.astype(F32).reshape(bsz, n_g, hpg, d_p, d_n),
                             (jnp.moveaxis(chunk_decay, 1, 0), jnp.moveaxis(states, 1, 0)))
    y_inter = jnp.einsum('bctgn,cbghpn->bctghp', cc, h_prev) * \
        jnp.exp(acs).reshape(bsz, nc, SSD_CHUNK, n_g, hpg)[..., None]
    y = (y_intra + y_inter).reshape(bsz, n_tok, n_h, d_p)
    return y, h_fin.reshape(bsz, n_h, d_p, d_n)


def ssd_mixer(h, w_in, conv_w, conv_b, a_log, dt_bias, d_skip, norm_w, w_o, h0_f, h0_b):
    bsz, n_tok, _ = h.shape
    gn = SSD_GROUPS * SSD_STATE
    z, xbc, dt = jnp.split(jnp.matmul(h, w_in), [SSD_INNER, 2 * SSD_INNER + 2 * gn], axis=-1)
    xbc = jax.nn.silu(dwconv_centred(xbc, conv_w, conv_b))
    xs, bm, cm = jnp.split(xbc, [SSD_INNER, SSD_INNER + gn], axis=-1)
    xs = xs.reshape(bsz, n_tok, SSD_HEADS, SSD_HEADDIM)
    bm = bm.reshape(bsz, n_tok, SSD_GROUPS, SSD_STATE)
    cm = cm.reshape(bsz, n_tok, SSD_GROUPS, SSD_STATE)
    dt = jax.nn.softplus(dt.astype(F32).reshape(bsz, n_tok, 2, SSD_HEADS) + dt_bias.astype(F32))
    a_head = -jnp.exp(a_log.astype(F32))
    y_f, h_f = ssd_chunked(xs, dt[:, :, 0], a_head[0], bm, cm, h0_f)
    y_b, h_b = ssd_chunked(rev(xs), rev(dt[:, :, 1]), a_head[1], rev(bm), rev(cm), h0_b)
    d_sum = (d_skip[0] + d_skip[1]).astype(F32)[:, None]
    y = y_f + rev(y_b) + d_sum * xs.astype(F32)
    y = y.astype(h.dtype).reshape(bsz, n_tok, SSD_INNER) * jax.nn.silu(z)
    y = rmsnorm(y.reshape(bsz, n_tok, SSD_GROUPS, -1), norm_w.reshape(SSD_GROUPS, -1))
    y = y.reshape(bsz, n_tok, SSD_INNER)
    return jnp.matmul(y, w_o), h_f.astype(h.dtype), h_b.astype(h.dtype)


def block_attention(q, k, v):
    bsz, n_q, n_h, d_h = q.shape
    n_kv = k.shape[2]
    grp = n_h // n_kv
    nb = n_q // Q_BLOCK
    qb = jnp.moveaxis(q.reshape(bsz, nb, Q_BLOCK, n_kv, grp, d_h), 1, 0)
    scale = d_h ** -0.5

    def one(qblk):
        s = jnp.einsum('bqkgd,bskd->bkgqs', qblk, k, preferred_element_type=F32) * scale
        pr = jax.nn.softmax(s, axis=-1).astype(v.dtype)
        return jnp.einsum('bkgqs,bskd->bqkgd', pr, v)

    o = lax.map(one, qb)
    return jnp.moveaxis(o, 0, 1).reshape(bsz, n_q, n_h, d_h)


def attn_project(h, w_qkv, qn, kn):
    bsz, n_tok, _ = h.shape
    qd = AT_HEADS * AT_HEAD_DIM
    kd = AT_KV_HEADS * AT_HEAD_DIM
    q, k, v = jnp.split(jnp.matmul(h, w_qkv), [qd, qd + kd], axis=-1)
    q = rmsnorm(q.reshape(bsz, n_tok, AT_HEADS, AT_HEAD_DIM), qn)
    k = rmsnorm(k.reshape(bsz, n_tok, AT_KV_HEADS, AT_HEAD_DIM), kn)
    v = v.reshape(bsz, n_tok, AT_KV_HEADS, AT_HEAD_DIM)
    return q, k, v


def attn_context(h, w_qkv, qn, kn, w_o):
    bsz, n_tok, _ = h.shape
    q, k, v = attn_project(h, w_qkv, qn, kn)
    o = block_attention(q, k, v)
    return jnp.matmul(o.reshape(bsz, n_tok, -1), w_o), k, v


def attn_latent(h, w_qkv, qn, kn, w_o, ctx_k, ctx_v):
    bsz, n_tok, _ = h.shape
    q, k, v = attn_project(h, w_qkv, qn, kn)
    q = rope_2d(q)
    k = rope_2d(k)
    keys = jnp.concatenate([ctx_k.astype(k.dtype), k], axis=1)
    vals = jnp.concatenate([ctx_v.astype(v.dtype), v], axis=1)
    o = block_attention(q, keys, vals)
    return jnp.matmul(o.reshape(bsz, n_tok, -1), w_o)


def swiglu(h, w13, w2):
    a, b = jnp.split(jnp.matmul(h, w13), 2, axis=-1)
    return jnp.matmul(jax.nn.silu(a) * b, w2)


def moe_swiglu(h, router, w13, w2):
    logits = jnp.einsum('btd,de->bte', h, router, preferred_element_type=F32)
    probs = jax.nn.softmax(logits, axis=-1)
    top_w, top_i = lax.top_k(probs, TOP_K)
    top_w = top_w / jnp.sum(top_w, axis=-1, keepdims=True)
    gates = jnp.sum(jax.nn.one_hot(top_i, N_EXPERTS, dtype=F32) * top_w[..., None], axis=-2).astype(h.dtype)
    hid = jnp.einsum('btd,edf->btef', h, w13)
    a, b = jnp.split(hid, 2, axis=-1)
    act = jax.nn.silu(a) * b * gates[..., None]
    return jnp.einsum('btef,efd->btd', act, w2)


def run_trunk(x, cond, p, cache):
    bsz = x.shape[0]
    lb_all = hgrn_lower_bounds(p['hg_lb_logits'])
    hg_states, ssd_states, k_list, v_list = [], [], [], []
    for layer in range(DEPTH):
        j = layer // N_MIXERS
        mod = jnp.matmul(jax.nn.silu(cond), p['ada_w'][layer]) + p['ada_b'][layer]
        sh1, sc1, g1, sh2, sc2, g2 = jnp.split(mod[:, None, :], 6, axis=-1)
        h = rmsnorm(x, p['norm_w'][layer, 0]) * (1 + sc1) + sh1
        kind = layer % N_MIXERS
        if kind == 0:
            if cache is None:
                s0_f = jnp.zeros((bsz, HG_HEADS, HG_DK, HG_DV), x.dtype)
                s0_b = s0_f
            else:
                s0_f, s0_b = cache[0][:, j, 0], cache[0][:, j, 1]
            m, s_f, s_b = hgrn2_mixer(h, p['hg_w_in'][j], lb_all[j], p['hg_onorm'][j],
                                      p['hg_w_o'][j], s0_f, s0_b)
            if cache is None:
                hg_states.append(jnp.stack([s_f, s_b], axis=1))
        elif kind == 1:
            if cache is None:
                h0_f = jnp.zeros((bsz, SSD_HEADS, SSD_HEADDIM, SSD_STATE), x.dtype)
                h0_b = h0_f
            else:
                h0_f, h0_b = cache[1][:, j, 0], cache[1][:, j, 1]
            m, st_f, st_b = ssd_mixer(h, p['ssd_w_in'][j], p['ssd_conv_w'][j], p['ssd_conv_b'][j],
                                      p['ssd_a_log'][j], p['ssd_dt_bias'][j], p['ssd_d'][j],
                                      p['ssd_norm'][j], p['ssd_w_o'][j], h0_f, h0_b)
            if cache is None:
                ssd_states.append(jnp.stack([st_f, st_b], axis=1))
        else:
            if cache is None:
                m, k_ctx, v_ctx = attn_context(h, p['at_w_qkv'][j], p['at_qn'][j], p['at_kn'][j], p['at_w_o'][j])
                k_list.append(k_ctx)
                v_list.append(v_ctx)
            else:
                m = attn_latent(h, p['at_w_qkv'][j], p['at_qn'][j], p['at_kn'][j], p['at_w_o'][j],
                                cache[2][:, j], cache[3][:, j])
        x = x + g1 * m
        h = rmsnorm(x, p['norm_w'][layer, 1]) * (1 + sc2) + sh2
        if layer % 2 == 0:
            f = swiglu(h, p['ff_w13'][layer // 2], p['ff_w2'][layer // 2])
        else:
            f = moe_swiglu(h, p['moe_router'][layer // 2], p['moe_w13'][layer // 2], p['moe_w2'][layer // 2])
        x = x + g2 * f
    if cache is not None:
        return x, None
    return x, (jnp.stack(hg_states, axis=1), jnp.stack(ssd_states, axis=1),
               jnp.stack(k_list, axis=1), jnp.stack(v_list, axis=1))


def setup_inputs(seed: int = 0) -> dict:
    key = jax.random.key(seed)
    keys = iter(jax.random.split(key, 48))

    def nrm(shape, scale):
        return jax.random.normal(next(keys), shape, F32) * scale

    gn = SSD_GROUPS * SSD_STATE
    hk = HG_HEADS * HG_DK
    hv = HG_HEADS * HG_DV
    x_prompt = nrm((BATCH, SEQ, D_MODEL), 1.0)
    x_sample = nrm((DEC_BATCH, DEC_SEQ, D_MODEL), 1.0)
    state_hgrn = nrm((DEC_BATCH, N_LAYERS_A, 2, HG_HEADS, HG_DK, HG_DV), 1.0)
    state_ssd = nrm((DEC_BATCH, N_LAYERS_B, 2, SSD_HEADS, SSD_HEADDIM, SSD_STATE), 1.0)
    cache_k = nrm((DEC_BATCH, N_LAYERS_C, PAST_LEN, AT_KV_HEADS, AT_HEAD_DIM), 1.0)
    cache_v = nrm((DEC_BATCH, N_LAYERS_C, PAST_LEN, AT_KV_HEADS, AT_HEAD_DIM), 1.0)
    c = nrm((DEC_BATCH, D_MODEL), 1.0)
    c_ctx = nrm((D_MODEL,), 1.0)
    ada_w = nrm((DEPTH, D_MODEL, 6 * D_MODEL), 0.5 * D_MODEL ** -0.5)
    ada_b = nrm((DEPTH, 6 * D_MODEL), 0.02)
    norm_w = 1.0 + nrm((DEPTH, 2, D_MODEL), 0.02)
    hg_w_in = nrm((N_LAYERS_A, D_MODEL, 3 * hk + 2 * hv), D_MODEL ** -0.5)
    hg_lb_logits = nrm((N_LAYERS_A, hk), 1.0)
    hg_onorm = 1.0 + nrm((N_LAYERS_A, HG_DV), 0.02)
    hg_w_o = nrm((N_LAYERS_A, hv, D_MODEL), hv ** -0.5)
    ssd_w_in = nrm((N_LAYERS_B, D_MODEL, 2 * SSD_INNER + 2 * gn + 2 * SSD_HEADS), D_MODEL ** -0.5)
    ssd_conv_w = nrm((N_LAYERS_B, SSD_CONV, SSD_INNER + 2 * gn), SSD_CONV ** -0.5)
    ssd_conv_b = nrm((N_LAYERS_B, SSD_INNER + 2 * gn), 0.02)
    ssd_a_log = jnp.log(jax.random.uniform(next(keys), (N_LAYERS_B, 2, SSD_HEADS), F32, 1.0, 16.0))
    dt0 = jnp.exp(jax.random.uniform(next(keys), (N_LAYERS_B, 2, SSD_HEADS), F32,
                                     math.log(1e-3), math.log(1e-1)))
    ssd_dt_bias = dt0 + jnp.log(-jnp.expm1(-dt0))
    ssd_d = 1.0 + nrm((N_LAYERS_B, 2, SSD_HEADS), 0.1)
    ssd_norm = 1.0 + nrm((N_LAYERS_B, SSD_INNER), 0.02)
    ssd_w_o = nrm((N_LAYERS_B, SSD_INNER, D_MODEL), SSD_INNER ** -0.5)
    at_w_qkv = nrm((N_LAYERS_C, D_MODEL, (AT_HEADS + 2 * AT_KV_HEADS) * AT_HEAD_DIM), D_MODEL ** -0.5)
    at_qn = 1.0 + nrm((N_LAYERS_C, AT_HEAD_DIM), 0.02)
    at_kn = 1.0 + nrm((N_LAYERS_C, AT_HEAD_DIM), 0.02)
    at_w_o = nrm((N_LAYERS_C, AT_HEADS * AT_HEAD_DIM, D_MODEL), (AT_HEADS * AT_HEAD_DIM) ** -0.5)
    ff_w13 = nrm((N_DENSE, D_MODEL, 2 * FF_DIM), D_MODEL ** -0.5)
    ff_w2 = nrm((N_DENSE, FF_DIM, D_MODEL), FF_DIM ** -0.5)
    moe_router = nrm((N_MOE, D_MODEL, N_EXPERTS), D_MODEL ** -0.5)
    moe_w13 = nrm((N_MOE, N_EXPERTS, D_MODEL, 2 * EXPERT_DIM), D_MODEL ** -0.5)
    moe_w2 = nrm((N_MOE, N_EXPERTS, EXPERT_DIM, D_MODEL), EXPERT_DIM ** -0.5)
    return {'x_prompt': x_prompt, 'x_sample': x_sample, 'state_hgrn': state_hgrn,
            'state_ssd': state_ssd, 'cache_k': cache_k, 'cache_v': cache_v, 'c': c, 'c_ctx': c_ctx,
            'ada_w': ada_w, 'ada_b': ada_b, 'norm_w': norm_w,
            'hg_w_in': hg_w_in, 'hg_lb_logits': hg_lb_logits, 'hg_onorm': hg_onorm, 'hg_w_o': hg_w_o,
            'ssd_w_in': ssd_w_in, 'ssd_conv_w': ssd_conv_w, 'ssd_conv_b': ssd_conv_b,
            'ssd_a_log': ssd_a_log, 'ssd_dt_bias': ssd_dt_bias, 'ssd_d': ssd_d,
            'ssd_norm': ssd_norm, 'ssd_w_o': ssd_w_o,
            'at_w_qkv': at_w_qkv, 'at_qn': at_qn, 'at_kn': at_kn, 'at_w_o': at_w_o,
            'ff_w13': ff_w13, 'ff_w2': ff_w2,
            'moe_router': moe_router, 'moe_w13': moe_w13, 'moe_w2': moe_w2}


def reference(x_prompt, x_sample, state_hgrn, state_ssd, cache_k, cache_v, c, c_ctx,
              ada_w, ada_b, norm_w, hg_w_in, hg_lb_logits, hg_onorm, hg_w_o,
              ssd_w_in, ssd_conv_w, ssd_conv_b, ssd_a_log, ssd_dt_bias, ssd_d, ssd_norm, ssd_w_o,
              at_w_qkv, at_qn, at_kn, at_w_o, ff_w13, ff_w2, moe_router, moe_w13, moe_w2):
    p = dict(ada_w=ada_w, ada_b=ada_b, norm_w=norm_w,
             hg_w_in=hg_w_in, hg_lb_logits=hg_lb_logits, hg_onorm=hg_onorm, hg_w_o=hg_w_o,
             ssd_w_in=ssd_w_in, ssd_conv_w=ssd_conv_w, ssd_conv_b=ssd_conv_b, ssd_a_log=ssd_a_log,
             ssd_dt_bias=ssd_dt_bias, ssd_d=ssd_d, ssd_norm=ssd_norm, ssd_w_o=ssd_w_o,
             at_w_qkv=at_w_qkv, at_qn=at_qn, at_kn=at_kn, at_w_o=at_w_o,
             ff_w13=ff_w13, ff_w2=ff_w2, moe_router=moe_router, moe_w13=moe_w13, moe_w2=moe_w2)
    y_prompt, ctx_tensors = run_trunk(x_prompt, c_ctx[None, :], p, None)
    new_hgrn, new_ssd, new_k, new_v = ctx_tensors
    y_sample, _ = run_trunk(x_sample, c, p, (state_hgrn, state_ssd, cache_k, cache_v))
    return (y_prompt, y_sample, new_hgrn, new_ssd, new_k, new_v)
```

```python
import functools
import math

import numpy as np
import jax
import jax.numpy as jnp
from jax import lax
from jax.experimental import pallas as pl
from jax.experimental.pallas import tpu as pltpu

F32 = jnp.float32
BF16 = jnp.bfloat16

NORM_EPS = 1e-6
GATE_FLOOR = 1e-30
ROPE_THETA = 10000.0
GRID_W = 64
LANES = 128
SUBLANES = 8
CHUNK = 128
BASE = 16
EXP_CLAMP = 80.0
NEG_BIG = -1e30
VMEM_LIMIT_BYTES = 52 * 1024 * 1024


def _cparams(*sem):
    return pltpu.CompilerParams(dimension_semantics=sem, vmem_limit_bytes=VMEM_LIMIT_BYTES)


def _sigmoid(x):
    return 1.0 / (1.0 + jnp.exp(-x))


def _silu(x):
    return x * _sigmoid(x)


def _pick_tile(n, cap, quantum=LANES):
    best = None
    t = quantum
    while t <= min(n, cap):
        if n % t == 0:
            best = t
        t += quantum
    assert best is not None, (n, cap)
    return best


def _dot(a, b):
    return jnp.dot(a, b, preferred_element_type=F32)


def _dot_nt(a, b):
    return lax.dot_general(a, b, (((1,), (1,)), ((), ())), preferred_element_type=F32)


def _split3(x):
    h1 = x.astype(BF16).astype(F32)
    r1 = x - h1
    h2 = r1.astype(BF16).astype(F32)
    h3 = (r1 - h2).astype(BF16).astype(F32)
    return h1, h2, h3


def _ada_kernel(c_ref, w_ref, b_ref, o_ref):
    s = _silu(c_ref[...]).astype(BF16)
    o_ref[...] = _dot(s, w_ref[...].astype(BF16)) + b_ref[...]


def _ada_mod(cond_rows, ada_w, ada_b):
    nl, d, n6 = ada_w.shape
    tn = _pick_tile(n6, 1024)
    rows = cond_rows.shape[0]
    return pl.pallas_call(
        _ada_kernel,
        out_shape=jax.ShapeDtypeStruct((nl, rows, n6), F32),
        grid=(nl, n6 // tn),
        in_specs=[pl.BlockSpec((rows, d), lambda l, j: (0, 0)),
                  pl.BlockSpec((None, d, tn), lambda l, j: (l, 0, j)),
                  pl.BlockSpec((None, 1, tn), lambda l, j: (l, 0, j))],
        out_specs=pl.BlockSpec((None, rows, tn), lambda l, j: (l, 0, j)),
        compiler_params=_cparams("parallel", "parallel"),
    )(cond_rows, ada_w, ada_b.reshape(nl, 1, n6))


def _modulated_norm(x, nw, mod, which):
    ms = jnp.mean(x * x, axis=-1, keepdims=True)
    y = x * lax.rsqrt(ms + NORM_EPS) * nw
    sh = mod[3 * which:3 * which + 1, :]
    sc = mod[3 * which + 1:3 * which + 2, :]
    return y * (1.0 + sc) + sh


def _norm_mod_kernel(x_ref, nw_ref, mod_ref, o_ref, *, which):
    o_ref[...] = _modulated_norm(x_ref[...], nw_ref[...], mod_ref[...], which).astype(o_ref.dtype)


class _Rows:
    def __init__(self, ctx_rows, dec_seq, n_dec):
        self.ctx_rows = ctx_rows
        self.dec_seq = dec_seq
        self.total = ctx_rows + dec_seq * n_dec
        self.tile = math.gcd(ctx_rows, dec_seq)

    def row_tile(self, cap):
        t = self.tile
        while t > cap and t % 2 == 0:
            t //= 2
        return t

    def cond_set(self, row0):
        return jnp.where(row0 < self.ctx_rows, 0, 1 + (row0 - self.ctx_rows) // self.dec_seq)


def _norm_mod(x, nw, mod_l, which, rows):
    nt, d = x.shape
    tm = rows.row_tile(256)
    return pl.pallas_call(
        functools.partial(_norm_mod_kernel, which=which),
        out_shape=jax.ShapeDtypeStruct((nt, d), BF16),
        grid=(nt // tm,),
        in_specs=[pl.BlockSpec((tm, d), lambda i: (i, 0)),
                  pl.BlockSpec((1, d), lambda i: (0, 0)),
                  pl.BlockSpec((None, SUBLANES, d), lambda i: (rows.cond_set(i * tm), 0, 0))],
        out_specs=pl.BlockSpec((tm, d), lambda i: (i, 0)),
        compiler_params=_cparams("parallel"),
    )(x, nw.reshape(1, d), mod_l)


def _cache_weight(w_ref, wbf_ref):
    @pl.when(pl.program_id(1) == 0)
    def _():
        wbf_ref[...] = w_ref[...].astype(BF16)


def _mm_plain_kernel(x_ref, w_ref, o_ref, wbf_ref):
    _cache_weight(w_ref, wbf_ref)
    o_ref[...] = _dot(x_ref[...], wbf_ref[...]).astype(o_ref.dtype)


def _matmul(x, w, out_dtype, tm_cap=512, tn_cap=1152):
    m, k = x.shape
    n = w.shape[1]
    tm = _pick_tile(m, tm_cap, SUBLANES)
    tn = _pick_tile(n, tn_cap)
    return pl.pallas_call(
        _mm_plain_kernel,
        out_shape=jax.ShapeDtypeStruct((m, n), out_dtype),
        grid=(n // tn, m // tm),
        in_specs=[pl.BlockSpec((tm, k), lambda j, i: (i, 0)),
                  pl.BlockSpec((k, tn), lambda j, i: (0, j))],
        out_specs=pl.BlockSpec((tm, tn), lambda j, i: (i, j)),
        scratch_shapes=[pltpu.VMEM((k, tn), BF16)],
        compiler_params=_cparams("parallel", "arbitrary"),
    )(x, w)


def _mm_res_kernel(x_ref, w_ref, r_ref, mod_ref, o_ref, wbf_ref, *, gate_row):
    _cache_weight(w_ref, wbf_ref)
    acc = _dot(x_ref[...], wbf_ref[...])
    o_ref[...] = r_ref[...] + mod_ref[gate_row:gate_row + 1, :] * acc


def _matmul_residual(x, w, res, mod_l, gate_row, rows):
    m, k = x.shape
    n = w.shape[1]
    tm = rows.row_tile(512 if k <= 2048 else 256)
    tn = _pick_tile(n, 1024 if k <= 2048 else 512)
    return pl.pallas_call(
        functools.partial(_mm_res_kernel, gate_row=gate_row),
        out_shape=jax.ShapeDtypeStruct((m, n), F32),
        grid=(n // tn, m // tm),
        in_specs=[pl.BlockSpec((tm, k), lambda j, i: (i, 0)),
                  pl.BlockSpec((k, tn), lambda j, i: (0, j)),
                  pl.BlockSpec((tm, tn), lambda j, i: (i, j)),
                  pl.BlockSpec((None, SUBLANES, tn), lambda j, i: (rows.cond_set(i * tm), 0, j))],
        out_specs=pl.BlockSpec((tm, tn), lambda j, i: (i, j)),
        scratch_shapes=[pltpu.VMEM((k, tn), BF16)],
        compiler_params=_cparams("parallel", "arbitrary"),
    )(x, w, res, mod_l)


def _mm_swiglu_kernel(x_ref, wa_ref, wb_ref, o_ref, wabf_ref, wbbf_ref):
    _cache_weight(wa_ref, wabf_ref)
    _cache_weight(wb_ref, wbbf_ref)
    x = x_ref[...]
    a = _dot(x, wabf_ref[...])
    b = _dot(x, wbbf_ref[...])
    o_ref[...] = (_silu(a) * b).astype(o_ref.dtype)


def _matmul_swiglu(x, w13):
    m, k = x.shape
    f = w13.shape[1] // 2
    tm = _pick_tile(m, 512, SUBLANES)
    tn = _pick_tile(f, 512)
    nb = f // tn
    return pl.pallas_call(
        _mm_swiglu_kernel,
        out_shape=jax.ShapeDtypeStruct((m, f), BF16),
        grid=(nb, m // tm),
        in_specs=[pl.BlockSpec((tm, k), lambda j, i: (i, 0)),
                  pl.BlockSpec((k, tn), lambda j, i: (0, j)),
                  pl.BlockSpec((k, tn), lambda j, i: (0, j + nb))],
        out_specs=pl.BlockSpec((tm, tn), lambda j, i: (i, j)),
        scratch_shapes=[pltpu.VMEM((k, tn), BF16), pltpu.VMEM((k, tn), BF16)],
        compiler_params=_cparams("parallel", "arbitrary"),
    )(x, w13, w13)


def _router_kernel(x_ref, nw_ref, mod_ref, r_ref, g_ref, *, n_experts):
    h = _modulated_norm(x_ref[...], nw_ref[...], mod_ref[...], 1)
    h1, h2, _ = _split3(h)
    r1, r2, _ = _split3(r_ref[...])
    h1, h2, r1, r2 = (t.astype(BF16) for t in (h1, h2, r1, r2))
    logits = _dot(h1, r1) + (_dot(h1, r2) + _dot(h2, r1))
    lane = lax.broadcasted_iota(jnp.int32, logits.shape, 1)
    logits = jnp.where(lane < n_experts, logits, NEG_BIG)
    m1 = jnp.max(logits, axis=-1, keepdims=True)
    i1 = jnp.min(jnp.where(logits == m1, lane, LANES), axis=-1, keepdims=True)
    rest = jnp.where(lane == i1, NEG_BIG, logits)
    m2 = jnp.max(rest, axis=-1, keepdims=True)
    i2 = jnp.min(jnp.where(rest == m2, lane, LANES), axis=-1, keepdims=True)
    e2 = jnp.exp(m2 - m1)
    w1 = 1.0 / (1.0 + e2)
    w2 = e2 * w1
    g_ref[...] = jnp.where(lane == i1, w1, 0.0) + jnp.where(lane == i2, w2, 0.0)


def _router_gates(x, nw, mod_l, router, rows):
    nt, d = x.shape
    n_experts = router.shape[1]
    tm = rows.row_tile(256)
    rpad = jnp.zeros((d, LANES), F32).at[:, :n_experts].set(router)
    return pl.pallas_call(
        functools.partial(_router_kernel, n_experts=n_experts),
        out_shape=jax.ShapeDtypeStruct((nt, LANES), F32),
        grid=(nt // tm,),
        in_specs=[pl.BlockSpec((tm, d), lambda i: (i, 0)),
                  pl.BlockSpec((1, d), lambda i: (0, 0)),
                  pl.BlockSpec((None, SUBLANES, d), lambda i: (rows.cond_set(i * tm), 0, 0)),
                  pl.BlockSpec((d, LANES), lambda i: (0, 0))],
        out_specs=pl.BlockSpec((tm, LANES), lambda i: (i, 0)),
        compiler_params=_cparams("parallel"),
    )(x, nw.reshape(1, d), mod_l, rpad)


def _mm_moe_kernel(x_ref, wa_ref, wb_ref, g_ref, o_ref):
    x = x_ref[...]
    a = _dot(x, wa_ref[...])
    b = _dot(x, wb_ref[...])
    g = g_ref[...]
    lane = lax.broadcasted_iota(jnp.int32, g.shape, 1)
    gcol = jnp.sum(jnp.where(lane == pl.program_id(0), g, 0.0), axis=-1, keepdims=True)
    o_ref[...] = (_silu(a) * b * gcol).astype(o_ref.dtype)


def _moe_swiglu(x, w13, gates):
    m, k = x.shape
    n_e = w13.shape[0]
    f = w13.shape[2] // 2
    tm = _pick_tile(m, 256, SUBLANES)
    return pl.pallas_call(
        _mm_moe_kernel,
        out_shape=jax.ShapeDtypeStruct((m, n_e * f), BF16),
        grid=(n_e, m // tm),
        in_specs=[pl.BlockSpec((tm, k), lambda e, i: (i, 0)),
                  pl.BlockSpec((None, k, f), lambda e, i: (e, 0, 0)),
                  pl.BlockSpec((None, k, f), lambda e, i: (e, 0, 1)),
                  pl.BlockSpec((tm, LANES), lambda e, i: (i, 0))],
        out_specs=pl.BlockSpec((tm, f), lambda e, i: (i, e)),
        compiler_params=_cparams("parallel", "arbitrary"),
    )(x, w13, w13, gates)


def _mm_res_acc_kernel(x_ref, w_ref, r_ref, mod_ref, o_ref, acc_ref, *, gate_row, nk):
    k = pl.program_id(2)
    p = _dot(x_ref[...], w_ref[...].astype(BF16))

    @pl.when(k == 0)
    def _():
        acc_ref[...] = p

    @pl.when(k > 0)
    def _():
        acc_ref[...] += p

    @pl.when(k == nk - 1)
    def _():
        o_ref[...] = r_ref[...] + mod_ref[gate_row:gate_row + 1, :] * acc_ref[...]


def _matmul_residual_ktiled(x, w, res, mod_l, gate_row, rows, tk):
    m, k = x.shape
    n = w.shape[1]
    tm = rows.row_tile(1024)
    tn = _pick_tile(n, 1024)
    nk = k // tk
    return pl.pallas_call(
        functools.partial(_mm_res_acc_kernel, gate_row=gate_row, nk=nk),
        out_shape=jax.ShapeDtypeStruct((m, n), F32),
        grid=(n // tn, m // tm, nk),
        in_specs=[pl.BlockSpec((tm, tk), lambda j, i, kk: (i, kk)),
                  pl.BlockSpec((tk, tn), lambda j, i, kk: (kk, j)),
                  pl.BlockSpec((tm, tn), lambda j, i, kk: (i, j)),
                  pl.BlockSpec((None, SUBLANES, tn), lambda j, i, kk: (rows.cond_set(i * tm), 0, j))],
        out_specs=pl.BlockSpec((tm, tn), lambda j, i, kk: (i, j)),
        scratch_shapes=[pltpu.VMEM((tm, tn), F32)],
        compiler_params=_cparams("parallel", "parallel", "arbitrary"),
    )(x, w, res, mod_l)


def _tri_pair(n):
    lo = np.tril(np.ones((n, n), np.float32))
    return jnp.asarray(np.stack([lo, lo.T]), BF16)


def _level_halves():
    hs = []
    h = CHUNK // 2
    while h >= BASE:
        hs.append(h)
        h //= 2
    return hs


def _gla_masks():
    t = np.arange(CHUNK)[:, None]
    s = np.arange(CHUNK)[None, :]
    out = []
    for rev in (False, True):
        per = []
        for h in _level_halves():
            same = (t // (2 * h)) == (s // (2 * h))
            t_up = (t % (2 * h)) >= h
            s_up = (s % (2 * h)) >= h
            per.append(same & (~t_up & s_up if rev else t_up & ~s_up))
        same = (t // BASE) == (s // BASE)
        per.append(same & ((s >= t) if rev else (s <= t)))
        out.append(np.stack(per))
    return jnp.asarray(np.stack(out).astype(np.float32))


def _chunk_cumsum(tri, x):
    n = x.shape[1]
    p = _dot(tri, jnp.concatenate(_split3(x), axis=1).astype(BF16))
    return (p[:, 2 * n:] + p[:, n:2 * n]) + p[:, :n]


def _hgrn_chunk(q, z, v, lb, tri, masks, st, rev):
    c = q.shape[0]
    one_m_lb = 1.0 - lb
    f = lb + one_m_lb * _sigmoid(z)
    logf = jnp.log(jnp.maximum(f, GATE_FLOOR))
    key = one_m_lb * _sigmoid(-z)
    b = _chunk_cumsum(tri, logf)
    vb = v.astype(BF16)

    scores = None
    for lvl, h in enumerate(_level_halves()):
        nb = c // (2 * h)
        b3 = b.reshape(nb, 2 * h, LANES)
        r = h if rev else h - 1
        w = jnp.exp(-jnp.abs(b3 - b3[:, r:r + 1, :])).reshape(c, LANES)
        part = _dot_nt((q * w).astype(BF16), (key * w).astype(BF16)) * masks[lvl]
        scores = part if scores is None else scores + part
    nb = c // BASE
    b3 = b.reshape(nb, BASE, LANES)
    l3 = logf.reshape(nb, BASE, LANES)
    r = BASE - 1 if rev else 0
    e3 = b3 - (b3[:, r:r + 1, :] - l3[:, r:r + 1, :])
    eq = e3.reshape(c, LANES)
    qb = (q * jnp.exp(eq)).astype(BF16)
    kb = (key * jnp.exp(jnp.minimum(-eq, EXP_CLAMP))).astype(BF16)
    scores = scores + _dot_nt(qb, kb) * masks[len(_level_halves())]

    o = _dot(scores.astype(BF16), vb) + _dot_nt((q * jnp.exp(b)).astype(BF16), st.astype(BF16))
    r_last = 0 if rev else c - 1
    b_last = b[r_last:r_last + 1, :]
    kt = (key * jnp.exp(b_last - b)).astype(BF16)
    st_new = st * jnp.exp(b_last) + _dot(v.T.astype(BF16), kt)
    return o, st_new


def _hgrn_kernel(*refs, seq_len, use_s0):
    if use_s0:
        (q_ref, zf_ref, zb_ref, v_ref, g_ref, lb_ref, on_ref, tri_ref, msk_ref, s0f_ref, s0b_ref,
         y_ref, sf_ref, sb_ref, oacc_ref) = refs
    else:
        (q_ref, zf_ref, zb_ref, v_ref, g_ref, lb_ref, on_ref, tri_ref, msk_ref,
         y_ref, sf_ref, sb_ref, oacc_ref) = refs
    nc = seq_len // CHUNK
    lb = lb_ref[...]

    def rows_of(ci):
        return pl.ds(pl.multiple_of(ci * CHUNK, CHUNK), CHUNK)

    def fwd_body(ci, st):
        rows = rows_of(ci)
        o, st = _hgrn_chunk(q_ref[rows, :], zf_ref[rows, :], v_ref[rows, :], lb,
                            tri_ref[0], msk_ref.at[0], st, False)
        oacc_ref[rows, :] = o
        return st

    def bwd_body(i, st):
        ci = nc - 1 - i
        rows = rows_of(ci)
        o, st = _hgrn_chunk(q_ref[rows, :], zb_ref[rows, :], v_ref[rows, :], lb,
                            tri_ref[1], msk_ref.at[1], st, True)
        o = o + oacc_ref[rows, :]
        ms = jnp.mean(o * o, axis=-1, keepdims=True)
        y = o * lax.rsqrt(ms + NORM_EPS) * on_ref[...]
        y_ref[rows, :] = (y * _silu(g_ref[rows, :])).astype(y_ref.dtype)
        return st

    zero = jnp.zeros((LANES, LANES), F32)
    st_f = lax.fori_loop(0, nc, fwd_body, s0f_ref[...].T if use_s0 else zero)
    sf_ref[...] = st_f.T
    st_b = lax.fori_loop(0, nc, bwd_body, s0b_ref[...].T if use_s0 else zero)
    sb_ref[...] = st_b.T


def _hgrn_core(proj, lb_row, onorm, s0, n_seq, seq_len, row_block0, n_heads):
    use_s0 = s0 is not None
    hd = LANES

    def col(off):
        return pl.BlockSpec((seq_len, hd), lambda s, h, off=off: (row_block0 + s, off * n_heads + h))

    in_specs = [col(0), col(1), col(2), col(3), col(4),
                pl.BlockSpec((1, hd), lambda s, h: (0, h)),
                pl.BlockSpec((1, hd), lambda s, h: (0, 0)),
                pl.BlockSpec((2, CHUNK, CHUNK), lambda s, h: (0, 0, 0)),
                pl.BlockSpec((2, len(_level_halves()) + 1, CHUNK, CHUNK), lambda s, h: (0, 0, 0, 0))]
    args = [proj] * 5 + [lb_row, onorm.reshape(1, hd), _tri_pair(CHUNK), _gla_masks()]
    if use_s0:
        in_specs += [pl.BlockSpec((None, None, None, hd, hd), lambda s, h: (s, 0, h, 0, 0)),
                     pl.BlockSpec((None, None, None, hd, hd), lambda s, h: (s, 1, h, 0, 0))]
        args += [s0, s0]
    st_spec = pl.BlockSpec((None, None, hd, hd), lambda s, h: (s, h, 0, 0))
    y, sf, sb = pl.pallas_call(
        functools.partial(_hgrn_kernel, seq_len=seq_len, use_s0=use_s0),
        out_shape=[jax.ShapeDtypeStruct((n_seq * seq_len, n_heads * hd), BF16),
                   jax.ShapeDtypeStruct((n_seq, n_heads, hd, hd), F32),
                   jax.ShapeDtypeStruct((n_seq, n_heads, hd, hd), F32)],
        grid=(n_seq, n_heads),
        in_specs=in_specs,
        out_specs=[pl.BlockSpec((seq_len, hd), lambda s, h: (s, h)), st_spec, st_spec],
        scratch_shapes=[pltpu.VMEM((seq_len, hd), F32)],
        compiler_params=_cparams("parallel", "parallel"),
    )(*args)
    return y, jnp.stack([sf, sb], axis=1)


def _ssd_conv_kernel(x_ref, w_ref, b_ref, o_ref, *, seq_len, n_taps):
    x = x_ref[...]
    row = lax.broadcasted_iota(jnp.int32, x.shape, 0)
    half = n_taps // 2
    acc = x * w_ref[half:half + 1, :] + b_ref[...]
    for j in range(n_taps):
        d = j - half
        if d == 0:
            continue
        shifted = pltpu.roll(x, (-d) % seq_len, 0)
        valid = (row + d >= 0) & (row + d < seq_len)
        acc = acc + jnp.where(valid, shifted, 0.0) * w_ref[j:j + 1, :]
    o_ref[...] = _silu(acc)


def _ssd_conv(proj, conv_w, conv_b, col_block0, n_seq, seq_len, row_block0):
    n_taps, c = conv_w.shape
    tc = _pick_tile(c, 256)
    assert (col_block0 * LANES) % tc == 0
    cb0 = col_block0 * LANES // tc
    return pl.pallas_call(
        functools.partial(_ssd_conv_kernel, seq_len=seq_len, n_taps=n_taps),
        out_shape=jax.ShapeDtypeStruct((n_seq * seq_len, c), F32),
        grid=(n_seq, c // tc),
        in_specs=[pl.BlockSpec((seq_len, tc), lambda s, j: (row_block0 + s, cb0 + j)),
                  pl.BlockSpec((n_taps, tc), lambda s, j: (0, j)),
                  pl.BlockSpec((1, tc), lambda s, j: (0, j))],
        out_specs=pl.BlockSpec((seq_len, tc), lambda s, j: (s, j)),
        compiler_params=_cparams("parallel", "parallel"),
    )(proj, conv_w, conv_b.reshape(1, c))


def _ssd_dt_kernel(d_ref, bias_ref, a_ref, o_ref):
    x = d_ref[...] + bias_ref[...]
    dt = jnp.maximum(x, 0.0) + jnp.log1p(jnp.exp(-jnp.abs(x)))
    o_ref[0] = dt.T
    o_ref[1] = (dt * a_ref[...]).T


def _ssd_dt(proj, dt_bias, a_log, col_block, n_seq, seq_len, row_block0):
    nh2 = dt_bias.size
    assert nh2 == LANES
    a_row = (-jnp.exp(a_log.astype(F32))).reshape(1, nh2)
    return pl.pallas_call(
        _ssd_dt_kernel,
        out_shape=jax.ShapeDtypeStruct((n_seq, 2, nh2, seq_len), F32),
        grid=(n_seq,),
        in_specs=[pl.BlockSpec((seq_len, nh2), lambda s: (row_block0 + s, col_block)),
                  pl.BlockSpec((1, nh2), lambda s: (0, 0)),
                  pl.BlockSpec((1, nh2), lambda s: (0, 0))],
        out_specs=pl.BlockSpec((None, 2, nh2, seq_len), lambda s: (s, 0, 0, 0)),
        compiler_params=_cparams("parallel"),
    )(proj, dt_bias.reshape(1, nh2).astype(F32), a_row)


def _ssd_chunk(xs, bm, cm, dta, tri, h_state, rev, hpg, hdim):
    c = xs.shape[0]
    pair = LANES // hdim
    dt_t = dta[0]
    a_t = dta[1]
    acs_t = _chunk_cumsum_rows(a_t, tri)
    pad = jnp.zeros((LANES - 2 * hpg, c), F32)
    cols = jnp.concatenate([acs_t, dt_t, pad], axis=0).T
    r_last = 0 if rev else c - 1
    cb = _dot_nt(cm.astype(BF16), bm.astype(BF16))
    ti = lax.broadcasted_iota(jnp.int32, (c, c), 0)
    si = lax.broadcasted_iota(jnp.int32, (c, c), 1)
    causal = (si >= ti) if rev else (si <= ti)
    lane = lax.broadcasted_iota(jnp.int32, (c, LANES), 1)
    y_inter_all = _dot(cm.astype(BF16), h_state.astype(BF16))
    bt = bm.T.astype(BF16)

    ys = []
    new_states = []
    for p in range(hpg // pair):
        ms = []
        dt_col = None
        e_col = None
        dend_col = None
        cdec_row = None
        for u in range(pair):
            hh = p * pair + u
            acs_c = cols[:, hh:hh + 1]
            acs_r = acs_t[hh:hh + 1, :]
            seg = jnp.where(causal, acs_c - acs_r, NEG_BIG)
            ms.append((cb * jnp.exp(seg)).astype(BF16))
            sel = (lane >= u * hdim) & (lane < (u + 1) * hdim)
            dtc = cols[:, hpg + hh:hpg + hh + 1]
            a_last = cols[r_last:r_last + 1, hh:hh + 1]
            dt_col = jnp.where(sel, dtc, 0.0 if dt_col is None else dt_col)
            e_col = jnp.where(sel, jnp.exp(acs_c), 0.0 if e_col is None else e_col)
            dend_col = jnp.where(sel, jnp.exp(a_last - acs_c), 0.0 if dend_col is None else dend_col)
            cdec_row = jnp.where(sel[:1, :], jnp.exp(a_last), 0.0 if cdec_row is None else cdec_row)
        x2 = xs[:, p * LANES:(p + 1) * LANES]
        xdt = x2 * dt_col
        rhs = []
        for u in range(pair):
            sel = (lane >= u * hdim) & (lane < (u + 1) * hdim)
            rhs.append(jnp.where(sel, xdt, 0.0).astype(BF16))
        y_intra = _dot(jnp.concatenate(ms, axis=1), jnp.concatenate(rhs, axis=0))
        y_inter = y_inter_all[:, p * LANES:(p + 1) * LANES] * e_col
        ys.append(y_intra + y_inter)
        upd = _dot(bt, (xdt * dend_col).astype(BF16))
        new_states.append(h_state[:, p * LANES:(p + 1) * LANES] * cdec_row + upd)
    return jnp.concatenate(ys, axis=1), jnp.concatenate(new_states, axis=1)


def _chunk_cumsum_rows(a_t, tri):
    h = a_t.shape[0]
    parts = jnp.concatenate(_split3(a_t), axis=0).astype(BF16)
    p = _dot(parts, tri)
    return (p[2 * h:] + p[h:2 * h]) + p[:h]


def _ssd_kernel(*refs, seq_len, use_h0, hpg, hdim):
    if use_h0:
        (xs_ref, bm_ref, cm_ref, z_ref, dta_ref, dsk_ref, nw_ref, tri_ref, h0f_ref, h0b_ref,
         y_ref, hf_ref, hb_ref, yacc_ref) = refs
    else:
        (xs_ref, bm_ref, cm_ref, z_ref, dta_ref, dsk_ref, nw_ref, tri_ref,
         y_ref, hf_ref, hb_ref, yacc_ref) = refs
    nc = seq_len // CHUNK
    width = hpg * hdim
    n_state = bm_ref.shape[1]

    def rows_of(ci):
        return pl.ds(pl.multiple_of(ci * CHUNK, CHUNK), CHUNK)

    def fwd_body(ci, hs):
        rows = rows_of(ci)
        y, hs = _ssd_chunk(xs_ref[rows, :], bm_ref[rows, :], cm_ref[rows, :], dta_ref[:, 0, ci],
                           tri_ref[1], hs, False, hpg, hdim)
        yacc_ref[rows, :] = y
        return hs

    def bwd_body(i, hs):
        ci = nc - 1 - i
        rows = rows_of(ci)
        xs = xs_ref[rows, :]
        y, hs = _ssd_chunk(xs, bm_ref[rows, :], cm_ref[rows, :], dta_ref[:, 1, ci],
                           tri_ref[0], hs, True, hpg, hdim)
        y = y + yacc_ref[rows, :] + dsk_ref[...] * xs
        y = y * _silu(z_ref[rows, :])
        ms = jnp.mean(y * y, axis=-1, keepdims=True)
        y_ref[rows, :] = (y * lax.rsqrt(ms + NORM_EPS) * nw_ref[...]).astype(y_ref.dtype)
        return hs

    def load_state(ref):
        return ref[...].reshape(width, n_state).T

    zero = jnp.zeros((n_state, width), F32)
    h_f = lax.fori_loop(0, nc, fwd_body, load_state(h0f_ref) if use_h0 else zero)
    hf_ref[...] = h_f.T.reshape(hpg, hdim, n_state)
    h_b = lax.fori_loop(0, nc, bwd_body, load_state(h0b_ref) if use_h0 else zero)
    hb_ref[...] = h_b.T.reshape(hpg, hdim, n_state)


def _ssd_core(xbc, proj, dta, d_cols, norm_w, h0, n_seq, seq_len, row_block0, n_groups, n_heads,
              hdim, n_state):
    use_h0 = h0 is not None
    hpg = n_heads // n_groups
    width = hpg * hdim
    inner = n_heads * hdim
    assert n_state == LANES and width % LANES == 0 and inner % width == 0
    nc = seq_len // CHUNK
    b_blk0 = inner // n_state
    c_blk0 = b_blk0 + n_groups
    in_specs = [pl.BlockSpec((seq_len, width), lambda s, g: (s, g)),
                pl.BlockSpec((seq_len, n_state), lambda s, g: (s, b_blk0 + g)),
                pl.BlockSpec((seq_len, n_state), lambda s, g: (s, c_blk0 + g)),
                pl.BlockSpec((seq_len, width), lambda s, g: (row_block0 + s, g)),
                pl.BlockSpec((None, 2, 2, None, nc, hpg, CHUNK), lambda s, g: (s, 0, 0, g, 0, 0, 0)),
                pl.BlockSpec((1, width), lambda s, g: (0, g)),
                pl.BlockSpec((1, width), lambda s, g: (0, g)),
                pl.BlockSpec((2, CHUNK, CHUNK), lambda s, g: (0, 0, 0))]
    args = [xbc, xbc, xbc, proj, dta, d_cols, norm_w.reshape(1, inner), _tri_pair(CHUNK)]
    if use_h0:
        st_in = (None, None, hpg, hdim, n_state)
        in_specs += [pl.BlockSpec(st_in, lambda s, g: (s, 0, g, 0, 0)),
                     pl.BlockSpec(st_in, lambda s, g: (s, 1, g, 0, 0))]
        args += [h0, h0]
    st_spec = pl.BlockSpec((None, hpg, hdim, n_state), lambda s, g: (s, g, 0, 0))
    y, hf, hb = pl.pallas_call(
        functools.partial(_ssd_kernel, seq_len=seq_len, use_h0=use_h0, hpg=hpg, hdim=hdim),
        out_shape=[jax.ShapeDtypeStruct((n_seq * seq_len, inner), BF16),
                   jax.ShapeDtypeStruct((n_seq, n_heads, hdim, n_state), F32),
                   jax.ShapeDtypeStruct((n_seq, n_heads, hdim, n_state), F32)],
        grid=(n_seq, n_groups),
        in_specs=in_specs,
        out_specs=[pl.BlockSpec((seq_len, width), lambda s, g: (s, g)), st_spec, st_spec],
        scratch_shapes=[pltpu.VMEM((seq_len, width), F32)],
        compiler_params=_cparams("parallel", "parallel"),
    )(*args)
    return y, jnp.stack([hf, hb], axis=1)


def _rope_tables(rows, head_dim):
    quarter = head_dim // 4
    inv = ROPE_THETA ** (-np.arange(quarter, dtype=np.float64) / quarter)
    t = np.arange(rows.dec_seq)
    ang_r = (t // GRID_W)[:, None] * inv[None, :]
    ang_c = (t % GRID_W)[:, None] * inv[None, :]
    cos = np.concatenate([np.cos(ang_r)] * 2 + [np.cos(ang_c)] * 2, axis=1)
    sin = np.concatenate([-np.sin(ang_r), np.sin(ang_r), -np.sin(ang_c), np.sin(ang_c)], axis=1)
    n_dec = (rows.total - rows.ctx_rows) // rows.dec_seq
    cos = np.concatenate([np.ones((rows.ctx_rows, head_dim))] + [cos] * n_dec, axis=0)
    sin = np.concatenate([np.zeros((rows.ctx_rows, head_dim))] + [sin] * n_dec, axis=0)
    return jnp.asarray(cos, F32), jnp.asarray(sin, F32)


def _qk_prep_kernel(p_ref, cos_ref, sin_ref, qn_ref, kn_ref, q_ref, k_ref, *, n_q, n_kv, scale):
    cos = cos_ref[...]
    sin = sin_ref[...]
    lane = lax.broadcasted_iota(jnp.int32, cos.shape, 1)
    first = (lane % (LANES // 2)) < (LANES // 4)

    def norm_rope(x, w):
        ms = jnp.mean(x * x, axis=-1, keepdims=True)
        y = x * lax.rsqrt(ms + NORM_EPS) * w
        partner = jnp.where(first, pltpu.roll(y, LANES - LANES // 4, 1), pltpu.roll(y, LANES // 4, 1))
        return y * cos + partner * sin

    for h in range(n_q):
        x = p_ref[:, h * LANES:(h + 1) * LANES]
        q_ref[:, h * LANES:(h + 1) * LANES] = (norm_rope(x, qn_ref[...]) * scale).astype(q_ref.dtype)
    for h in range(n_kv):
        x = p_ref[:, (n_q + h) * LANES:(n_q + h + 1) * LANES]
        k_ref[:, h * LANES:(h + 1) * LANES] = norm_rope(x, kn_ref[...])


def _qk_prep(proj, cos, sin, qn, kn, n_q, n_kv, rows):
    nt = proj.shape[0]
    hd = LANES
    tm = rows.row_tile(256)
    scale = float(hd) ** -0.5
    return pl.pallas_call(
        functools.partial(_qk_prep_kernel, n_q=n_q, n_kv=n_kv, scale=scale),
        out_shape=[jax.ShapeDtypeStruct((nt, n_q * hd), BF16),
                   jax.ShapeDtypeStruct((nt, n_kv * hd), F32)],
        grid=(nt // tm,),
        in_specs=[pl.BlockSpec((tm, proj.shape[1]), lambda i: (i, 0)),
                  pl.BlockSpec((tm, hd), lambda i: (i, 0)),
                  pl.BlockSpec((tm, hd), lambda i: (i, 0)),
                  pl.BlockSpec((1, hd), lambda i: (0, 0)),
                  pl.BlockSpec((1, hd), lambda i: (0, 0))],
        out_specs=[pl.BlockSpec((tm, n_q * hd), lambda i: (i, 0)),
                   pl.BlockSpec((tm, n_kv * hd), lambda i: (i, 0))],
        compiler_params=_cparams("parallel"),
    )(proj, cos, sin, qn.reshape(1, hd), kn.reshape(1, hd))


def _attn_kernel(*refs, grp, use_ctx):
    if use_ctx:
        q_ref, k_ref, v_ref, kc_ref, vc_ref, o_ref = refs
    else:
        q_ref, k_ref, v_ref, o_ref = refs
    tq = q_ref.shape[0]
    q = jnp.concatenate([q_ref[:, g * LANES:(g + 1) * LANES] for g in range(grp)], axis=0)
    s = _dot_nt(q, k_ref[...].astype(BF16))
    m = jnp.max(s, axis=-1, keepdims=True)
    if use_ctx:
        s0 = _dot_nt(q, kc_ref[...].astype(BF16))
        m = jnp.maximum(m, jnp.max(s0, axis=-1, keepdims=True))
    p = jnp.exp(s - m)
    l = jnp.sum(p, axis=-1, keepdims=True)
    o = _dot(p.astype(BF16), v_ref[...].astype(BF16))
    if use_ctx:
        p0 = jnp.exp(s0 - m)
        l = l + jnp.sum(p0, axis=-1, keepdims=True)
        o = o + _dot(p0.astype(BF16), vc_ref[...].astype(BF16))
    o = o / l
    for g in range(grp):
        o_ref[:, g * LANES:(g + 1) * LANES] = o[g * tq:(g + 1) * tq].astype(o_ref.dtype)


def _attention(qn, kn, proj, ctx_k, ctx_v, n_seq, seq_len, row_block0, n_q, n_kv):
    use_ctx = ctx_k is not None
    grp = n_q // n_kv
    hd = LANES
    tq = min(seq_len, 128)
    nqb = seq_len // tq
    v_blk0 = n_q + n_kv
    in_specs = [pl.BlockSpec((tq, grp * hd), lambda s, kv, i: ((row_block0 + s) * nqb + i, kv)),
                pl.BlockSpec((seq_len, hd), lambda s, kv, i: (row_block0 + s, kv)),
                pl.BlockSpec((seq_len, hd), lambda s, kv, i: (row_block0 + s, v_blk0 + kv))]
    args = [qn, kn, proj]
    if use_ctx:
        past = ctx_k.shape[1]
        in_specs += [pl.BlockSpec((None, past, hd), lambda s, kv, i: (s, 0, kv)),
                     pl.BlockSpec((None, past, hd), lambda s, kv, i: (s, 0, kv))]
        args += [ctx_k, ctx_v]
    return pl.pallas_call(
        functools.partial(_attn_kernel, grp=grp, use_ctx=use_ctx),
        out_shape=jax.ShapeDtypeStruct((n_seq * seq_len, n_q * hd), BF16),
        grid=(n_seq, n_kv, nqb),
        in_specs=in_specs,
        out_specs=pl.BlockSpec((tq, grp * hd), lambda s, kv, i: (s * nqb + i, kv)),
        compiler_params=_cparams("parallel", "parallel", "arbitrary"),
    )(*args)


def kernel(x_prompt, x_sample, state_hgrn, state_ssd, cache_k, cache_v, c, c_ctx, ada_w, ada_b, norm_w,
           hg_w_in, hg_lb_logits, hg_onorm, hg_w_o, ssd_w_in, ssd_conv_w, ssd_conv_b, ssd_a_log,
           ssd_dt_bias, ssd_d, ssd_norm, ssd_w_o, at_w_qkv, at_qn, at_kn, at_w_o, ff_w13, ff_w2,
           moe_router, moe_w13, moe_w2):
    batch, seq, d = x_prompt.shape
    n_dec, dec_seq, _ = x_sample.shape
    depth = ada_w.shape[0]
    rows = _Rows(batch * seq, dec_seq, n_dec)
    ctx_rows = rows.ctx_rows
    assert seq % CHUNK == 0 and dec_seq % CHUNK == 0 and dec_seq % seq == 0 and ctx_rows % dec_seq == 0

    hg_heads = d // LANES
    ssd_heads = ssd_a_log.shape[2]
    ssd_inner = ssd_w_o.shape[1]
    ssd_hdim = ssd_inner // ssd_heads
    ssd_nstate = state_ssd.shape[-1]
    ssd_groups = (ssd_conv_w.shape[2] - ssd_inner) // (2 * ssd_nstate)
    at_kv = cache_k.shape[3]
    at_heads = at_w_o.shape[1] // LANES
    past = cache_k.shape[2]

    x = jnp.concatenate([x_prompt.reshape(ctx_rows, d), x_sample.reshape(n_dec * dec_seq, d)], axis=0)

    n_sets = 1 + n_dec
    cond = jnp.concatenate([c_ctx[None, :], c], axis=0)
    cond = jnp.pad(cond, ((0, (-n_sets) % SUBLANES), (0, 0)))
    mod = _ada_mod(cond, ada_w, ada_b)[:, :n_sets].reshape(depth, n_sets, 6, d)
    mod = jnp.pad(mod, ((0, 0), (0, 0), (0, SUBLANES - 6), (0, 0)))

    lb_all = None
    cos_t = sin_t = None
    hg_states, ssd_states, k_list, v_list = [], [], [], []
    lat_blk_seq = ctx_rows // dec_seq

    for layer in range(depth):
        j = layer // 3
        kind = layer % 3
        mod_l = mod[layer]
        h = _norm_mod(x, norm_w[layer, 0], mod_l, 0, rows)
        if kind == 0:
            if lb_all is None:
                pr = jax.nn.softmax(hg_lb_logits.astype(F32), axis=0)
                lb_all = jnp.cumsum(pr, axis=0) - pr[0]
            proj = _matmul(h, hg_w_in[j], F32)
            lb_row = lb_all[j].reshape(1, -1)
            y_c, st_c = _hgrn_core(proj, lb_row, hg_onorm[j], None, batch, seq, 0, hg_heads)
            y_l, _ = _hgrn_core(proj, lb_row, hg_onorm[j], state_hgrn[:, j], n_dec, dec_seq,
                                lat_blk_seq, hg_heads)
            hg_states.append(st_c)
            y = jnp.concatenate([y_c, y_l], axis=0)
            x = _matmul_residual(y, hg_w_o[j], x, mod_l, 2, rows)
        elif kind == 1:
            proj = _matmul(h, ssd_w_in[j], F32)
            gn2 = 2 * ssd_groups * ssd_nstate
            xbc_blk0 = ssd_inner // LANES
            dt_blk = (2 * ssd_inner + gn2) // LANES
            d_cols = jnp.repeat((ssd_d[j, 0] + ssd_d[j, 1]).astype(F32), ssd_hdim).reshape(1, ssd_inner)
            hpg = ssd_heads // ssd_groups
            ys = []
            for (n_s, s_len, rb0, h0) in ((batch, seq, 0, None),
                                          (n_dec, dec_seq, lat_blk_seq, state_ssd[:, j])):
                xbc = _ssd_conv(proj, ssd_conv_w[j], ssd_conv_b[j], xbc_blk0, n_s, s_len, rb0)
                dta = _ssd_dt(proj, ssd_dt_bias[j], ssd_a_log[j], dt_blk, n_s, s_len, rb0)
                nc = s_len // CHUNK
                dta = dta.reshape(n_s, 2, 2, ssd_groups, hpg, nc, CHUNK).transpose(0, 1, 2, 3, 5, 4, 6)
                y_p, st_p = _ssd_core(xbc, proj, dta, d_cols, ssd_norm[j], h0, n_s, s_len, rb0,
                                      ssd_groups, ssd_heads, ssd_hdim, ssd_nstate)
                ys.append(y_p)
                if h0 is None:
                    ssd_states.append(st_p)
            y = jnp.concatenate(ys, axis=0)
            x = _matmul_residual(y, ssd_w_o[j], x, mod_l, 2, rows)
        else:
            proj = _matmul(h, at_w_qkv[j], F32)
            if cos_t is None:
                cos_t, sin_t = _rope_tables(rows, LANES)
            qn, kn = _qk_prep(proj, cos_t, sin_t, at_qn[j], at_kn[j], at_heads, at_kv, rows)
            o_c = _attention(qn, kn, proj, None, None, batch, seq, 0, at_heads, at_kv)
            ck = cache_k[:, j].reshape(n_dec, past, at_kv * LANES)
            cv = cache_v[:, j].reshape(n_dec, past, at_kv * LANES)
            o_l = _attention(qn, kn, proj, ck, cv, n_dec, dec_seq, lat_blk_seq, at_heads, at_kv)
            k_list.append(kn[:ctx_rows].reshape(batch, seq, at_kv, LANES))
            v_list.append(proj[:ctx_rows, (at_heads + at_kv) * LANES:].reshape(batch, seq, at_kv, LANES))
            y = jnp.concatenate([o_c, o_l], axis=0)
            x = _matmul_residual(y, at_w_o[j], x, mod_l, 2, rows)

        if layer % 2 == 0:
            h2 = _norm_mod(x, norm_w[layer, 1], mod_l, 1, rows)
            act = _matmul_swiglu(h2, ff_w13[layer // 2])
            x = _matmul_residual(act, ff_w2[layer // 2], x, mod_l, 5, rows)
        else:
            e = layer // 2
            h2 = _norm_mod(x, norm_w[layer, 1], mod_l, 1, rows)
            gates = _router_gates(x, norm_w[layer, 1], mod_l, moe_router[e], rows)
            act = _moe_swiglu(h2, moe_w13[e].astype(BF16), gates)
            n_e, f_e, _ = moe_w2[e].shape
            w2 = moe_w2[e].astype(BF16).reshape(n_e * f_e, d)
            x = _matmul_residual_ktiled(act, w2, x, mod_l, 5, rows, f_e)

    y_prompt = x[:ctx_rows].reshape(batch, seq, d)
    y_sample = x[ctx_rows:].reshape(n_dec, dec_seq, d)
    return (y_prompt, y_sample, jnp.stack(hg_states, axis=1), jnp.stack(ssd_states, axis=1),
            jnp.stack(k_list, axis=1), jnp.stack(v_list, axis=1))
```

```python
import functools
import math

import numpy as np
import jax
import jax.numpy as jnp
from jax import lax
from jax.experimental import pallas as pl
from jax.experimental.pallas import tpu as pltpu

F32 = jnp.float32
BF16 = jnp.bfloat16

NORM_EPS = 1e-6
GATE_FLOOR = 1e-30
ROPE_THETA = 10000.0
GRID_W = 64
LANES = 128
SUBLANES = 8
CHUNK = 128
BASE = 16
EXP_CLAMP = 80.0
NEG_BIG = -1e30
VMEM_LIMIT_BYTES = 52 * 1024 * 1024


def _cparams(*sem):
    return pltpu.CompilerParams(dimension_semantics=sem, vmem_limit_bytes=VMEM_LIMIT_BYTES)


def _sigmoid(x):
    return 1.0 / (1.0 + jnp.exp(-x))


def _silu(x):
    return x * _sigmoid(x)


def _pick_tile(n, cap, quantum=LANES):
    best = None
    t = quantum
    while t <= min(n, cap):
        if n % t == 0:
            best = t
        t += quantum
    assert best is not None, (n, cap)
    return best


def _dot(a, b):
    return jnp.dot(a, b, preferred_element_type=F32)


def _dot_nt(a, b):
    return lax.dot_general(a, b, (((1,), (1,)), ((), ())), preferred_element_type=F32)


def _split3(x):
    h1 = x.astype(BF16).astype(F32)
    r1 = x - h1
    h2 = r1.astype(BF16).astype(F32)
    h3 = (r1 - h2).astype(BF16).astype(F32)
    return h1, h2, h3


def _ada_kernel(c_ref, w_ref, b_ref, o_ref):
    s = _silu(c_ref[...]).astype(BF16)
    o_ref[...] = _dot(s, w_ref[...].astype(BF16)) + b_ref[...]


def _ada_mod(cond_rows, ada_w, ada_b):
    nl, d, n6 = ada_w.shape
    tn = _pick_tile(n6, 1024)
    rows = cond_rows.shape[0]
    return pl.pallas_call(
        _ada_kernel,
        out_shape=jax.ShapeDtypeStruct((nl, rows, n6), F32),
        grid=(nl, n6 // tn),
        in_specs=[pl.BlockSpec((rows, d), lambda l, j: (0, 0)),
                  pl.BlockSpec((None, d, tn), lambda l, j: (l, 0, j)),
                  pl.BlockSpec((None, 1, tn), lambda l, j: (l, 0, j))],
        out_specs=pl.BlockSpec((None, rows, tn), lambda l, j: (l, 0, j)),
        compiler_params=_cparams("parallel", "parallel"),
    )(cond_rows, ada_w, ada_b.reshape(nl, 1, n6))


def _modulated_norm(x, nw, mod, which):
    ms = jnp.mean(x * x, axis=-1, keepdims=True)
    y = x * lax.rsqrt(ms + NORM_EPS) * nw
    sh = mod[3 * which:3 * which + 1, :]
    sc = mod[3 * which + 1:3 * which + 2, :]
    return y * (1.0 + sc) + sh


def _norm_mod_kernel(x_ref, nw_ref, mod_ref, o_ref, *, which):
    o_ref[...] = _modulated_norm(x_ref[...], nw_ref[...], mod_ref[...], which).astype(o_ref.dtype)


class _Rows:
    def __init__(self, ctx_rows, dec_seq, n_dec):
        self.ctx_rows = ctx_rows
        self.dec_seq = dec_seq
        self.total = ctx_rows + dec_seq * n_dec
        self.tile = math.gcd(ctx_rows, dec_seq)

    def row_tile(self, cap):
        t = self.tile
        while t > cap and t % 2 == 0:
            t //= 2
        return t

    def cond_set(self, row0):
        return jnp.where(row0 < self.ctx_rows, 0, 1 + (row0 - self.ctx_rows) // self.dec_seq)


def _norm_mod(x, nw, mod_l, which, rows):
    nt, d = x.shape
    tm = rows.row_tile(256)
    return pl.pallas_call(
        functools.partial(_norm_mod_kernel, which=which),
        out_shape=jax.ShapeDtypeStruct((nt, d), BF16),
        grid=(nt // tm,),
        in_specs=[pl.BlockSpec((tm, d), lambda i: (i, 0)),
                  pl.BlockSpec((1, d), lambda i: (0, 0)),
                  pl.BlockSpec((None, SUBLANES, d), lambda i: (rows.cond_set(i * tm), 0, 0))],
        out_specs=pl.BlockSpec((tm, d), lambda i: (i, 0)),
        compiler_params=_cparams("parallel"),
    )(x, nw.reshape(1, d), mod_l)


def _cache_weight(w_ref, wbf_ref):
    @pl.when(pl.program_id(1) == 0)
    def _():
        wbf_ref[...] = w_ref[...].astype(BF16)


def _mm_plain_kernel(x_ref, w_ref, o_ref, wbf_ref):
    _cache_weight(w_ref, wbf_ref)
    o_ref[...] = _dot(x_ref[...], wbf_ref[...]).astype(o_ref.dtype)


def _matmul(x, w, out_dtype, tm_cap=512, tn_cap=1152):
    m, k = x.shape
    n = w.shape[1]
    tm = _pick_tile(m, tm_cap, SUBLANES)
    tn = _pick_tile(n, tn_cap)
    return pl.pallas_call(
        _mm_plain_kernel,
        out_shape=jax.ShapeDtypeStruct((m, n), out_dtype),
        grid=(n // tn, m // tm),
        in_specs=[pl.BlockSpec((tm, k), lambda j, i: (i, 0)),
                  pl.BlockSpec((k, tn), lambda j, i: (0, j))],
        out_specs=pl.BlockSpec((tm, tn), lambda j, i: (i, j)),
        scratch_shapes=[pltpu.VMEM((k, tn), BF16)],
        compiler_params=_cparams("parallel", "arbitrary"),
    )(x, w)


def _mm_res_kernel(x_ref, w_ref, r_ref, mod_ref, o_ref, wbf_ref, *, gate_row):
    _cache_weight(w_ref, wbf_ref)
    acc = _dot(x_ref[...], wbf_ref[...])
    o_ref[...] = r_ref[...] + mod_ref[gate_row:gate_row + 1, :] * acc


def _matmul_residual(x, w, res, mod_l, gate_row, rows):
    m, k = x.shape
    n = w.shape[1]
    tm = rows.row_tile(512 if k <= 2048 else 256)
    tn = _pick_tile(n, 1024 if k <= 2048 else 512)
    return pl.pallas_call(
        functools.partial(_mm_res_kernel, gate_row=gate_row),
        out_shape=jax.ShapeDtypeStruct((m, n), F32),
        grid=(n // tn, m // tm),
        in_specs=[pl.BlockSpec((tm, k), lambda j, i: (i, 0)),
                  pl.BlockSpec((k, tn), lambda j, i: (0, j)),
                  pl.BlockSpec((tm, tn), lambda j, i: (i, j)),
                  pl.BlockSpec((None, SUBLANES, tn), lambda j, i: (rows.cond_set(i * tm), 0, j))],
        out_specs=pl.BlockSpec((tm, tn), lambda j, i: (i, j)),
        scratch_shapes=[pltpu.VMEM((k, tn), BF16)],
        compiler_params=_cparams("parallel", "arbitrary"),
    )(x, w, res, mod_l)


def _mm_swiglu_kernel(x_ref, wa_ref, wb_ref, o_ref, wabf_ref, wbbf_ref):
    _cache_weight(wa_ref, wabf_ref)
    _cache_weight(wb_ref, wbbf_ref)
    x = x_ref[...]
    a = _dot(x, wabf_ref[...])
    b = _dot(x, wbbf_ref[...])
    o_ref[...] = (_silu(a) * b).astype(o_ref.dtype)


def _matmul_swiglu(x, w13):
    m, k = x.shape
    f = w13.shape[1] // 2
    tm = _pick_tile(m, 512, SUBLANES)
    tn = _pick_tile(f, 512)
    nb = f // tn
    return pl.pallas_call(
        _mm_swiglu_kernel,
        out_shape=jax.ShapeDtypeStruct((m, f), BF16),
        grid=(nb, m // tm),
        in_specs=[pl.BlockSpec((tm, k), lambda j, i: (i, 0)),
                  pl.BlockSpec((k, tn), lambda j, i: (0, j)),
                  pl.BlockSpec((k, tn), lambda j, i: (0, j + nb))],
        out_specs=pl.BlockSpec((tm, tn), lambda j, i: (i, j)),
        scratch_shapes=[pltpu.VMEM((k, tn), BF16), pltpu.VMEM((k, tn), BF16)],
        compiler_params=_cparams("parallel", "arbitrary"),
    )(x, w13, w13)


MOE_TILE = 512
HI16 = 0xFFFF0000


def _router_kernel(x_ref, nw_ref, mod_ref, r_ref, hp_ref, g_ref, *, n_experts):
    h = _modulated_norm(x_ref[...], nw_ref[...], mod_ref[...], 1)
    half = h.shape[1] // 2
    bits = lax.bitcast_convert_type(h.astype(BF16).astype(F32), jnp.uint32)
    hp_ref[...] = (bits[:, :half] >> 16) | (bits[:, half:] & jnp.uint32(HI16))
    h1, h2, _ = _split3(h)
    r1, r2, _ = _split3(r_ref[...])
    h1, h2, r1, r2 = (t.astype(BF16) for t in (h1, h2, r1, r2))
    logits = _dot(h1, r1) + (_dot(h1, r2) + _dot(h2, r1))
    lane = lax.broadcasted_iota(jnp.int32, logits.shape, 1)
    logits = jnp.where(lane < n_experts, logits, NEG_BIG)
    m1 = jnp.max(logits, axis=-1, keepdims=True)
    i1 = jnp.min(jnp.where(logits == m1, lane, LANES), axis=-1, keepdims=True)
    rest = jnp.where(lane == i1, NEG_BIG, logits)
    m2 = jnp.max(rest, axis=-1, keepdims=True)
    i2 = jnp.min(jnp.where(rest == m2, lane, LANES), axis=-1, keepdims=True)
    e2 = jnp.exp(m2 - m1)
    w1 = 1.0 / (1.0 + e2)
    w2 = e2 * w1
    g_ref[...] = (jnp.where(lane == 0, w1, 0.0) + jnp.where(lane == 1, w2, 0.0)
                  + jnp.where(lane == 2, i1.astype(F32), 0.0) + jnp.where(lane == 3, i2.astype(F32), 0.0))


def _router(x, nw, mod_l, router, rows):
    nt, d = x.shape
    n_experts = router.shape[1]
    tm = rows.row_tile(256)
    rpad = jnp.zeros((d, LANES), F32).at[:, :n_experts].set(router)
    return pl.pallas_call(
        functools.partial(_router_kernel, n_experts=n_experts),
        out_shape=[jax.ShapeDtypeStruct((nt, d // 2), jnp.uint32),
                   jax.ShapeDtypeStruct((nt, LANES), F32)],
        grid=(nt // tm,),
        in_specs=[pl.BlockSpec((tm, d), lambda i: (i, 0)),
                  pl.BlockSpec((1, d), lambda i: (0, 0)),
                  pl.BlockSpec((None, SUBLANES, d), lambda i: (rows.cond_set(i * tm), 0, 0)),
                  pl.BlockSpec((d, LANES), lambda i: (0, 0))],
        out_specs=[pl.BlockSpec((tm, d // 2), lambda i: (i, 0)),
                   pl.BlockSpec((tm, LANES), lambda i: (i, 0))],
        compiler_params=_cparams("parallel"),
    )(x, nw.reshape(1, d), mod_l, rpad)


def _route_tables(rinfo, n_experts, tm):
    nt = rinfo.shape[0]
    e_flat = jnp.concatenate([rinfo[:, 2], rinfo[:, 3]]).astype(jnp.int32)
    onehot = (e_flat[:, None] == jnp.arange(n_experts, dtype=jnp.int32)[None, :]).astype(jnp.int32)
    csum = jnp.cumsum(onehot, axis=0)
    rank = jnp.sum(csum * onehot, axis=1) - 1
    padded = ((csum[-1] + tm - 1) // tm) * tm
    ends = jnp.cumsum(padded)
    dest = jnp.sum(onehot * (ends - padded)[None, :], axis=1) + rank
    p_rows = 2 * nt + n_experts * tm
    tok = jnp.tile(jnp.arange(nt, dtype=jnp.int32), 2)
    src = jnp.zeros((p_rows,), jnp.int32).at[dest].set(tok, unique_indices=True)
    tile_start = jnp.arange(p_rows // tm, dtype=jnp.int32) * tm
    tile_e = jnp.sum((tile_start[:, None] >= ends[None, :]).astype(jnp.int32), axis=1)
    tile_e = jnp.minimum(tile_e, n_experts - 1)
    n_used = (ends[-1] // tm).astype(jnp.int32).reshape(1)
    return src, dest, tile_e, n_used


def _moe_expert_kernel(te_ref, nu_ref, src_ref, hp_ref, w13_ref, w2_ref, ys_ref, xbuf, sem, *, tm):
    del te_ref
    i = pl.program_id(0)
    n_used = nu_ref[0]

    def row_copy(tile, slot, r):
        return pltpu.make_async_copy(hp_ref.at[pl.ds(src_ref[tile * tm + r], 1)],
                                     xbuf.at[slot, pl.ds(r, 1)], sem.at[slot])

    def issue(tile, slot):
        def body(r, carry):
            row_copy(tile, slot, r).start()
            return carry
        lax.fori_loop(0, tm, body, 0)

    @pl.when(i == 0)
    def _():
        issue(0, 0)

    @pl.when(i + 1 < n_used)
    def _():
        issue(i + 1, (i + 1) % 2)

    @pl.when(i < n_used)
    def _():
        slot = i % 2

        def wait_body(r, carry):
            row_copy(i, slot, r).wait()
            return carry
        lax.fori_loop(0, tm, wait_body, 0)
        u = xbuf[slot]
        lo = lax.bitcast_convert_type(u << 16, F32)
        hi = lax.bitcast_convert_type(u & jnp.uint32(HI16), F32)
        x = jnp.concatenate([lo, hi], axis=1).astype(BF16)
        f = w2_ref.shape[0]
        a = _dot(x, w13_ref[:, :f])
        b = _dot(x, w13_ref[:, f:])
        ys_ref[...] = _dot((_silu(a) * b).astype(BF16), w2_ref[...])

    @pl.when(i >= n_used)
    def _():
        ys_ref[...] = jnp.zeros(ys_ref.shape, ys_ref.dtype)


def _moe_experts(hp, w13, w2, src, tile_e, n_used, tm):
    n_e, d, f2 = w13.shape
    f = f2 // 2
    p_rows = src.shape[0]
    grid_spec = pltpu.PrefetchScalarGridSpec(
        num_scalar_prefetch=3,
        grid=(p_rows // tm,),
        in_specs=[pl.BlockSpec(memory_space=pl.ANY),
                  pl.BlockSpec((None, d, f2), lambda i, te, nu, sr: (te[i], 0, 0),
                               pipeline_mode=pl.Buffered(1)),
                  pl.BlockSpec((None, f, d), lambda i, te, nu, sr: (te[i], 0, 0),
                               pipeline_mode=pl.Buffered(1))],
        out_specs=pl.BlockSpec((tm, d), lambda i, te, nu, sr: (i, 0)),
        scratch_shapes=[pltpu.VMEM((2, tm, d // 2), jnp.uint32), pltpu.SemaphoreType.DMA((2,))])
    return pl.pallas_call(
        functools.partial(_moe_expert_kernel, tm=tm),
        out_shape=jax.ShapeDtypeStruct((p_rows, d), F32),
        grid_spec=grid_spec,
        compiler_params=_cparams("arbitrary"),
    )(tile_e, n_used, src, hp, w13, w2)


def _moe_combine_kernel(dest_ref, ys_ref, x_ref, g_ref, mod_ref, o_ref, ybuf, sem, *, tm, nt, n_steps):
    i = pl.program_id(0)

    def row_copy(tile, slot, k, r):
        return pltpu.make_async_copy(ys_ref.at[pl.ds(dest_ref[k * nt + tile * tm + r], 1)],
                                     ybuf.at[slot, k, pl.ds(r, 1)], sem.at[slot])

    def issue(tile, slot):
        def body(r, carry):
            row_copy(tile, slot, 0, r).start()
            row_copy(tile, slot, 1, r).start()
            return carry
        lax.fori_loop(0, tm, body, 0)

    @pl.when(i == 0)
    def _():
        issue(0, 0)

    @pl.when(i + 1 < n_steps)
    def _():
        issue(i + 1, (i + 1) % 2)

    slot = i % 2

    def wait_body(r, carry):
        row_copy(i, slot, 0, r).wait()
        row_copy(i, slot, 1, r).wait()
        return carry
    lax.fori_loop(0, tm, wait_body, 0)
    g = g_ref[...]
    mix = g[:, 0:1] * ybuf[slot, 0] + g[:, 1:2] * ybuf[slot, 1]
    o_ref[...] = x_ref[...] + mod_ref[5:6, :] * mix


def _moe_combine(ys, dest, x, rinfo, mod_l, rows):
    nt, d = x.shape
    tm = rows.row_tile(256)
    n_steps = nt // tm
    grid_spec = pltpu.PrefetchScalarGridSpec(
        num_scalar_prefetch=1,
        grid=(n_steps,),
        in_specs=[pl.BlockSpec(memory_space=pl.ANY),
                  pl.BlockSpec((tm, d), lambda i, ds: (i, 0)),
                  pl.BlockSpec((tm, LANES), lambda i, ds: (i, 0)),
                  pl.BlockSpec((None, SUBLANES, d), lambda i, ds: (rows.cond_set(i * tm), 0, 0))],
        out_specs=pl.BlockSpec((tm, d), lambda i, ds: (i, 0)),
        scratch_shapes=[pltpu.VMEM((2, 2, tm, d), F32), pltpu.SemaphoreType.DMA((2,))])
    return pl.pallas_call(
        functools.partial(_moe_combine_kernel, tm=tm, nt=nt, n_steps=n_steps),
        out_shape=jax.ShapeDtypeStruct((nt, d), F32),
        grid_spec=grid_spec,
        compiler_params=_cparams("arbitrary"),
    )(dest, ys, x, rinfo, mod_l)


def _tri_pair(n):
    lo = np.tril(np.ones((n, n), np.float32))
    return jnp.asarray(np.stack([lo, lo.T]), BF16)


def _level_halves():
    hs = []
    h = CHUNK // 2
    while h >= BASE:
        hs.append(h)
        h //= 2
    return hs


def _gla_masks():
    t = np.arange(CHUNK)[:, None]
    s = np.arange(CHUNK)[None, :]
    out = []
    for rev in (False, True):
        per = []
        for h in _level_halves():
            same = (t // (2 * h)) == (s // (2 * h))
            t_up = (t % (2 * h)) >= h
            s_up = (s % (2 * h)) >= h
            per.append(same & (~t_up & s_up if rev else t_up & ~s_up))
        same = (t // BASE) == (s // BASE)
        per.append(same & ((s >= t) if rev else (s <= t)))
        out.append(np.stack(per))
    return jnp.asarray(np.stack(out).astype(np.float32))


def _chunk_cumsum(tri, x):
    n = x.shape[1]
    h1 = x.astype(BF16)
    h2 = (x - h1.astype(F32)).astype(BF16)
    p = _dot(tri, jnp.concatenate([h1, h2], axis=1))
    return p[:, n:] + p[:, :n]


def _hgrn_chunks(chains):
    halves = _level_halves()
    c = chains[0]["q"].shape[0]
    for ch in chains:
        z, lb = ch["z"], ch["lb"]
        one_m_lb = 1.0 - lb
        e = jnp.exp(-jnp.abs(z))
        r = 1.0 / (1.0 + e)
        er = e * r
        pos = z >= 0
        f = lb + one_m_lb * jnp.where(pos, r, er)
        ch["logf"] = jnp.log(jnp.maximum(f, GATE_FLOOR))
        ch["key"] = one_m_lb * jnp.where(pos, er, r)
    for ch in chains:
        ch["b"] = _chunk_cumsum(ch["tri"], ch["logf"])
    for ch in chains:
        ch["scores"] = None
    for lvl, h in enumerate(halves):
        for ch in chains:
            b = ch["b"]
            b3 = b.reshape(c // (2 * h), 2 * h, LANES)
            r = h if ch["rev"] else h - 1
            w = jnp.exp(-jnp.abs(b3 - b3[:, r:r + 1, :])).reshape(c, LANES)
            part = _dot_nt((ch["q"] * w).astype(BF16), (ch["key"] * w).astype(BF16)) * ch["masks"][lvl]
            ch["scores"] = part if ch["scores"] is None else ch["scores"] + part
    for ch in chains:
        b3 = ch["b"].reshape(c // BASE, BASE, LANES)
        l3 = ch["logf"].reshape(c // BASE, BASE, LANES)
        r = BASE - 1 if ch["rev"] else 0
        eq = (b3 - (b3[:, r:r + 1, :] - l3[:, r:r + 1, :])).reshape(c, LANES)
        qb = (ch["q"] * jnp.exp(eq)).astype(BF16)
        kb = (ch["key"] * jnp.exp(jnp.minimum(-eq, EXP_CLAMP))).astype(BF16)
        ch["scores"] = ch["scores"] + _dot_nt(qb, kb) * ch["masks"][len(halves)]
    out = []
    for ch in chains:
        b, st, v = ch["b"], ch["st"], ch["v"]
        o = _dot(ch["scores"].astype(BF16), v.astype(BF16)) + \
            _dot_nt((ch["q"] * jnp.exp(b)).astype(BF16), st.astype(BF16))
        r_last = 0 if ch["rev"] else c - 1
        b_last = b[r_last:r_last + 1, :]
        kt = (ch["key"] * jnp.exp(b_last - b)).astype(BF16)
        out.append((o, st * jnp.exp(b_last) + _dot(v.T.astype(BF16), kt)))
    return out


HG_HEADS_PER_STEP = 2


def _hgrn_kernel(*refs, seq_len, use_s0, hps):
    if use_s0:
        (q_ref, zf_ref, zb_ref, v_ref, g_ref, lb_ref, on_ref, tri_ref, msk_ref, s0f_ref, s0b_ref,
         y_ref, sf_ref, sb_ref, of_ref, ob_ref) = refs
    else:
        (q_ref, zf_ref, zb_ref, v_ref, g_ref, lb_ref, on_ref, tri_ref, msk_ref,
         y_ref, sf_ref, sb_ref, of_ref, ob_ref) = refs
    nc = seq_len // CHUNK
    lanes = [slice(hh * LANES, (hh + 1) * LANES) for hh in range(hps)]

    def rows_of(ci):
        return pl.ds(pl.multiple_of(ci * CHUNK, CHUNK), CHUNK)

    def body(i, sts):
        rf = rows_of(i)
        rb = rows_of(nc - 1 - i)
        chains = []
        for hh, ln in enumerate(lanes):
            chains.append(dict(q=q_ref[rf, ln], z=zf_ref[rf, ln], v=v_ref[rf, ln], lb=lb_ref[:, ln],
                               tri=tri_ref[0], masks=msk_ref.at[0], st=sts[hh], rev=False))
        for hh, ln in enumerate(lanes):
            chains.append(dict(q=q_ref[rb, ln], z=zb_ref[rb, ln], v=v_ref[rb, ln], lb=lb_ref[:, ln],
                               tri=tri_ref[1], masks=msk_ref.at[1], st=sts[hps + hh], rev=True))
        res = _hgrn_chunks(chains)
        for hh, ln in enumerate(lanes):
            of_ref[rf, ln] = res[hh][0]
            ob_ref[rb, ln] = res[hps + hh][0]
        return tuple(st for _, st in res)

    zero = jnp.zeros((LANES, LANES), F32)
    init = tuple((s0f_ref[hh].T if use_s0 else zero) for hh in range(hps)) + \
        tuple((s0b_ref[hh].T if use_s0 else zero) for hh in range(hps))
    sts = lax.fori_loop(0, nc, body, init)
    for hh in range(hps):
        sf_ref[hh] = sts[hh].T
        sb_ref[hh] = sts[hps + hh].T

    def finish(ci, carry):
        rows = rows_of(ci)
        for ln in lanes:
            o = of_ref[rows, ln] + ob_ref[rows, ln]
            ms = jnp.mean(o * o, axis=-1, keepdims=True)
            y = o * lax.rsqrt(ms + NORM_EPS) * on_ref[...]
            y_ref[rows, ln] = (y * _silu(g_ref[rows, ln])).astype(y_ref.dtype)
        return carry

    lax.fori_loop(0, nc, finish, 0)


def _hgrn_core(proj, lb_row, onorm, s0, n_seq, seq_len, row_block0, n_heads):
    use_s0 = s0 is not None
    hd = LANES
    hps = HG_HEADS_PER_STEP
    assert n_heads % hps == 0
    nhb = n_heads // hps
    wd = hps * hd

    def col(off):
        return pl.BlockSpec((seq_len, wd), lambda s, h, off=off: (row_block0 + s, off * nhb + h))

    in_specs = [col(0), col(1), col(2), col(3), col(4),
                pl.BlockSpec((1, wd), lambda s, h: (0, h)),
                pl.BlockSpec((1, hd), lambda s, h: (0, 0)),
                pl.BlockSpec((2, CHUNK, CHUNK), lambda s, h: (0, 0, 0)),
                pl.BlockSpec((2, len(_level_halves()) + 1, CHUNK, CHUNK), lambda s, h: (0, 0, 0, 0))]
    args = [proj] * 5 + [lb_row, onorm.reshape(1, hd), _tri_pair(CHUNK), _gla_masks()]
    if use_s0:
        in_specs += [pl.BlockSpec((None, None, hps, hd, hd), lambda s, h: (s, 0, h, 0, 0)),
                     pl.BlockSpec((None, None, hps, hd, hd), lambda s, h: (s, 1, h, 0, 0))]
        args += [s0, s0]
    st_spec = pl.BlockSpec((None, hps, hd, hd), lambda s, h: (s, h, 0, 0))
    y, sf, sb = pl.pallas_call(
        functools.partial(_hgrn_kernel, seq_len=seq_len, use_s0=use_s0, hps=hps),
        out_shape=[jax.ShapeDtypeStruct((n_seq * seq_len, n_heads * hd), BF16),
                   jax.ShapeDtypeStruct((n_seq, n_heads, hd, hd), F32),
                   jax.ShapeDtypeStruct((n_seq, n_heads, hd, hd), F32)],
        grid=(n_seq, nhb),
        in_specs=in_specs,
        out_specs=[pl.BlockSpec((seq_len, wd), lambda s, h: (s, h)), st_spec, st_spec],
        scratch_shapes=[pltpu.VMEM((seq_len, wd), F32), pltpu.VMEM((seq_len, wd), F32)],
        compiler_params=_cparams("parallel", "parallel"),
    )(*args)
    return y, jnp.stack([sf, sb], axis=1)


def _ssd_conv_kernel(x_ref, w_ref, b_ref, o_ref, *, seq_len, n_taps):
    x = x_ref[...]
    row = lax.broadcasted_iota(jnp.int32, x.shape, 0)
    half = n_taps // 2
    acc = x * w_ref[half:half + 1, :] + b_ref[...]
    for j in range(n_taps):
        d = j - half
        if d == 0:
            continue
        shifted = pltpu.roll(x, (-d) % seq_len, 0)
        valid = (row + d >= 0) & (row + d < seq_len)
        acc = acc + jnp.where(valid, shifted, 0.0) * w_ref[j:j + 1, :]
    o_ref[...] = _silu(acc)


def _ssd_conv(proj, conv_w, conv_b, col_block0, n_seq, seq_len, row_block0):
    n_taps, c = conv_w.shape
    tc = _pick_tile(c, 256)
    assert (col_block0 * LANES) % tc == 0
    cb0 = col_block0 * LANES // tc
    return pl.pallas_call(
        functools.partial(_ssd_conv_kernel, seq_len=seq_len, n_taps=n_taps),
        out_shape=jax.ShapeDtypeStruct((n_seq * seq_len, c), F32),
        grid=(n_seq, c // tc),
        in_specs=[pl.BlockSpec((seq_len, tc), lambda s, j: (row_block0 + s, cb0 + j)),
                  pl.BlockSpec((n_taps, tc), lambda s, j: (0, j)),
                  pl.BlockSpec((1, tc), lambda s, j: (0, j))],
        out_specs=pl.BlockSpec((seq_len, tc), lambda s, j: (s, j)),
        compiler_params=_cparams("parallel", "parallel"),
    )(proj, conv_w, conv_b.reshape(1, c))


def _ssd_dt_kernel(d_ref, bias_ref, a_ref, o_ref):
    x = d_ref[...] + bias_ref[...]
    dt = jnp.maximum(x, 0.0) + jnp.log1p(jnp.exp(-jnp.abs(x)))
    o_ref[0] = dt.T
    o_ref[1] = (dt * a_ref[...]).T


def _ssd_dt(proj, dt_bias, a_log, col_block, n_seq, seq_len, row_block0):
    nh2 = dt_bias.size
    assert nh2 == LANES
    a_row = (-jnp.exp(a_log.astype(F32))).reshape(1, nh2)
    return pl.pallas_call(
        _ssd_dt_kernel,
        out_shape=jax.ShapeDtypeStruct((n_seq, 2, nh2, seq_len), F32),
        grid=(n_seq,),
        in_specs=[pl.BlockSpec((seq_len, nh2), lambda s: (row_block0 + s, col_block)),
                  pl.BlockSpec((1, nh2), lambda s: (0, 0)),
                  pl.BlockSpec((1, nh2), lambda s: (0, 0))],
        out_specs=pl.BlockSpec((None, 2, nh2, seq_len), lambda s: (s, 0, 0, 0)),
        compiler_params=_cparams("parallel"),
    )(proj, dt_bias.reshape(1, nh2).astype(F32), a_row)


def _ssd_chunks(chains, hpg, hdim):
    c = chains[0]["xs"].shape[0]
    pair = LANES // hdim
    ti = lax.broadcasted_iota(jnp.int32, (c, c), 0)
    si = lax.broadcasted_iota(jnp.int32, (c, c), 1)
    lane = lax.broadcasted_iota(jnp.int32, (c, LANES), 1)
    sels = [(lane >= u * hdim) & (lane < (u + 1) * hdim) for u in range(pair)]
    pad = jnp.zeros((LANES - 3 * hpg, c), F32)
    for ch in chains:
        dt_t, a_t = ch["dta"][0], ch["dta"][1]
        acs_t = _chunk_cumsum_rows(a_t, ch["tri"])
        r_last = 0 if ch["rev"] else c - 1
        a_last = acs_t[:, r_last:r_last + 1]
        ch["acs_t"] = acs_t
        ch["dt_t"] = dt_t
        ch["cdec"] = jnp.exp(a_last)
        ch["cols"] = jnp.concatenate([acs_t, dt_t * jnp.exp(a_last - acs_t), jnp.exp(acs_t), pad],
                                     axis=0).T
        ch["causal"] = (si >= ti) if ch["rev"] else (si <= ti)
    for ch in chains:
        cmb = ch["cm"].astype(BF16)
        ch["cb"] = _dot_nt(cmb, ch["bm"].astype(BF16))
        ch["y_inter"] = _dot(cmb, ch["h"].astype(BF16))
        ch["bt"] = ch["bm"].T.astype(BF16)
        ch["ys"] = []
        ch["hs"] = []
    for p in range(hpg // pair):
        for ch in chains:
            cols, acs_t, dt_t = ch["cols"], ch["acs_t"], ch["dt_t"]
            ms = []
            dtd_col = e_col = cdec_row = None
            for u in range(pair):
                hh = p * pair + u
                seg = jnp.where(ch["causal"], cols[:, hh:hh + 1] - acs_t[hh:hh + 1, :], NEG_BIG)
                ms.append((ch["cb"] * jnp.exp(seg) * dt_t[hh:hh + 1, :]).astype(BF16))
                dtd = cols[:, hpg + hh:hpg + hh + 1]
                ec = cols[:, 2 * hpg + hh:2 * hpg + hh + 1]
                cd = ch["cdec"][hh:hh + 1, :]
                dtd_col = dtd if dtd_col is None else jnp.where(sels[u], dtd, dtd_col)
                e_col = ec if e_col is None else jnp.where(sels[u], ec, e_col)
                cdec_row = cd if cdec_row is None else jnp.where(sels[u][:1, :], cd, cdec_row)
            x2 = ch["xs"][:, p * LANES:(p + 1) * LANES]
            rhs = [jnp.where(sels[u], x2, 0.0).astype(BF16) for u in range(pair)]
            y_intra = _dot(jnp.concatenate(ms, axis=1), jnp.concatenate(rhs, axis=0))
            ch["ys"].append(y_intra + ch["y_inter"][:, p * LANES:(p + 1) * LANES] * e_col)
            upd = _dot(ch["bt"], (x2 * dtd_col).astype(BF16))
            ch["hs"].append(ch["h"][:, p * LANES:(p + 1) * LANES] * cdec_row + upd)
    return [(jnp.concatenate(ch["ys"], axis=1), jnp.concatenate(ch["hs"], axis=1)) for ch in chains]


def _chunk_cumsum_rows(a_t, tri):
    h = a_t.shape[0]
    parts = jnp.concatenate(_split3(a_t), axis=0).astype(BF16)
    p = _dot(parts, tri)
    return (p[2 * h:] + p[h:2 * h]) + p[:h]


def _ssd_kernel(*refs, seq_len, use_h0, hpg, hdim):
    if use_h0:
        (xs_ref, bm_ref, cm_ref, z_ref, dta_ref, dsk_ref, nw_ref, tri_ref, h0f_ref, h0b_ref,
         y_ref, hf_ref, hb_ref, yf_ref, yb_ref) = refs
    else:
        (xs_ref, bm_ref, cm_ref, z_ref, dta_ref, dsk_ref, nw_ref, tri_ref,
         y_ref, hf_ref, hb_ref, yf_ref, yb_ref) = refs
    nc = seq_len // CHUNK
    width = hpg * hdim
    n_state = bm_ref.shape[1]

    def rows_of(ci):
        return pl.ds(pl.multiple_of(ci * CHUNK, CHUNK), CHUNK)

    def body(i, carry):
        h_f, h_b = carry
        rf = rows_of(i)
        cb = nc - 1 - i
        rb = rows_of(cb)
        (y_f, h_f), (y_b, h_b) = _ssd_chunks(
            [dict(xs=xs_ref[rf, :], bm=bm_ref[rf, :], cm=cm_ref[rf, :], dta=dta_ref[:, 0, i],
                  tri=tri_ref[1], h=h_f, rev=False),
             dict(xs=xs_ref[rb, :], bm=bm_ref[rb, :], cm=cm_ref[rb, :], dta=dta_ref[:, 1, cb],
                  tri=tri_ref[0], h=h_b, rev=True)], hpg, hdim)
        yf_ref[rf, :] = y_f
        yb_ref[rb, :] = y_b
        return h_f, h_b

    def load_state(ref):
        return ref[...].reshape(width, n_state).T

    zero = jnp.zeros((n_state, width), F32)
    init = (load_state(h0f_ref), load_state(h0b_ref)) if use_h0 else (zero, zero)
    h_f, h_b = lax.fori_loop(0, nc, body, init)
    hf_ref[...] = h_f.T.reshape(hpg, hdim, n_state)
    hb_ref[...] = h_b.T.reshape(hpg, hdim, n_state)

    def finish(ci, carry):
        rows = rows_of(ci)
        y = yf_ref[rows, :] + yb_ref[rows, :] + dsk_ref[...] * xs_ref[rows, :]
        y = y * _silu(z_ref[rows, :])
        ms = jnp.mean(y * y, axis=-1, keepdims=True)
        y_ref[rows, :] = (y * lax.rsqrt(ms + NORM_EPS) * nw_ref[...]).astype(y_ref.dtype)
        return carry

    lax.fori_loop(0, nc, finish, 0)


def _ssd_core(xbc, proj, dta, d_cols, norm_w, h0, n_seq, seq_len, row_block0, n_groups, n_heads,
              hdim, n_state):
    use_h0 = h0 is not None
    hpg = n_heads // n_groups
    width = hpg * hdim
    inner = n_heads * hdim
    assert n_state == LANES and width % LANES == 0 and inner % width == 0
    nc = seq_len // CHUNK
    b_blk0 = inner // n_state
    c_blk0 = b_blk0 + n_groups
    in_specs = [pl.BlockSpec((seq_len, width), lambda s, g: (s, g)),
                pl.BlockSpec((seq_len, n_state), lambda s, g: (s, b_blk0 + g)),
                pl.BlockSpec((seq_len, n_state), lambda s, g: (s, c_blk0 + g)),
                pl.BlockSpec((seq_len, width), lambda s, g: (row_block0 + s, g)),
                pl.BlockSpec((None, 2, 2, None, nc, hpg, CHUNK), lambda s, g: (s, 0, 0, g, 0, 0, 0)),
                pl.BlockSpec((1, width), lambda s, g: (0, g)),
                pl.BlockSpec((1, width), lambda s, g: (0, g)),
                pl.BlockSpec((2, CHUNK, CHUNK), lambda s, g: (0, 0, 0))]
    args = [xbc, xbc, xbc, proj, dta, d_cols, norm_w.reshape(1, inner), _tri_pair(CHUNK)]
    if use_h0:
        st_in = (None, None, hpg, hdim, n_state)
        in_specs += [pl.BlockSpec(st_in, lambda s, g: (s, 0, g, 0, 0)),
                     pl.BlockSpec(st_in, lambda s, g: (s, 1, g, 0, 0))]
        args += [h0, h0]
    st_spec = pl.BlockSpec((None, hpg, hdim, n_state), lambda s, g: (s, g, 0, 0))
    y, hf, hb = pl.pallas_call(
        functools.partial(_ssd_kernel, seq_len=seq_len, use_h0=use_h0, hpg=hpg, hdim=hdim),
        out_shape=[jax.ShapeDtypeStruct((n_seq * seq_len, inner), BF16),
                   jax.ShapeDtypeStruct((n_seq, n_heads, hdim, n_state), F32),
                   jax.ShapeDtypeStruct((n_seq, n_heads, hdim, n_state), F32)],
        grid=(n_seq, n_groups),
        in_specs=in_specs,
        out_specs=[pl.BlockSpec((seq_len, width), lambda s, g: (s, g)), st_spec, st_spec],
        scratch_shapes=[pltpu.VMEM((seq_len, width), F32), pltpu.VMEM((seq_len, width), F32)],
        compiler_params=_cparams("parallel", "parallel"),
    )(*args)
    return y, jnp.stack([hf, hb], axis=1)


def _rope_tables(rows, head_dim):
    quarter = head_dim // 4
    inv = ROPE_THETA ** (-np.arange(quarter, dtype=np.float64) / quarter)
    t = np.arange(rows.dec_seq)
    ang_r = (t // GRID_W)[:, None] * inv[None, :]
    ang_c = (t % GRID_W)[:, None] * inv[None, :]
    cos = np.concatenate([np.cos(ang_r)] * 2 + [np.cos(ang_c)] * 2, axis=1)
    sin = np.concatenate([-np.sin(ang_r), np.sin(ang_r), -np.sin(ang_c), np.sin(ang_c)], axis=1)
    n_dec = (rows.total - rows.ctx_rows) // rows.dec_seq
    cos = np.concatenate([np.ones((rows.ctx_rows, head_dim))] + [cos] * n_dec, axis=0)
    sin = np.concatenate([np.zeros((rows.ctx_rows, head_dim))] + [sin] * n_dec, axis=0)
    return jnp.asarray(cos, F32), jnp.asarray(sin, F32)


def _qk_prep_kernel(p_ref, cos_ref, sin_ref, qn_ref, kn_ref, q_ref, k_ref, *, n_q, n_kv, scale):
    cos = cos_ref[...]
    sin = sin_ref[...]
    lane = lax.broadcasted_iota(jnp.int32, cos.shape, 1)
    first = (lane % (LANES // 2)) < (LANES // 4)

    def norm_rope(x, w):
        ms = jnp.mean(x * x, axis=-1, keepdims=True)
        y = x * lax.rsqrt(ms + NORM_EPS) * w
        partner = jnp.where(first, pltpu.roll(y, LANES - LANES // 4, 1), pltpu.roll(y, LANES // 4, 1))
        return y * cos + partner * sin

    for h in range(n_q):
        x = p_ref[:, h * LANES:(h + 1) * LANES]
        q_ref[:, h * LANES:(h + 1) * LANES] = (norm_rope(x, qn_ref[...]) * scale).astype(q_ref.dtype)
    for h in range(n_kv):
        x = p_ref[:, (n_q + h) * LANES:(n_q + h + 1) * LANES]
        k_ref[:, h * LANES:(h + 1) * LANES] = norm_rope(x, kn_ref[...])


def _qk_prep(proj, cos, sin, qn, kn, n_q, n_kv, rows):
    nt = proj.shape[0]
    hd = LANES
    tm = rows.row_tile(256)
    scale = float(hd) ** -0.5
    return pl.pallas_call(
        functools.partial(_qk_prep_kernel, n_q=n_q, n_kv=n_kv, scale=scale),
        out_shape=[jax.ShapeDtypeStruct((nt, n_q * hd), BF16),
                   jax.ShapeDtypeStruct((nt, n_kv * hd), F32)],
        grid=(nt // tm,),
        in_specs=[pl.BlockSpec((tm, proj.shape[1]), lambda i: (i, 0)),
                  pl.BlockSpec((tm, hd), lambda i: (i, 0)),
                  pl.BlockSpec((tm, hd), lambda i: (i, 0)),
                  pl.BlockSpec((1, hd), lambda i: (0, 0)),
                  pl.BlockSpec((1, hd), lambda i: (0, 0))],
        out_specs=[pl.BlockSpec((tm, n_q * hd), lambda i: (i, 0)),
                   pl.BlockSpec((tm, n_kv * hd), lambda i: (i, 0))],
        compiler_params=_cparams("parallel"),
    )(proj, cos, sin, qn.reshape(1, hd), kn.reshape(1, hd))


def _attn_kernel(*refs, grp, use_ctx):
    if use_ctx:
        q_ref, k_ref, v_ref, kc_ref, vc_ref, o_ref = refs
    else:
        q_ref, k_ref, v_ref, o_ref = refs
    tq = q_ref.shape[0]
    q = jnp.concatenate([q_ref[:, g * LANES:(g + 1) * LANES] for g in range(grp)], axis=0)
    s = _dot_nt(q, k_ref[...].astype(BF16))
    m = jnp.max(s, axis=-1, keepdims=True)
    if use_ctx:
        s0 = _dot_nt(q, kc_ref[...].astype(BF16))
        m = jnp.maximum(m, jnp.max(s0, axis=-1, keepdims=True))
    p = jnp.exp(s - m)
    l = jnp.sum(p, axis=-1, keepdims=True)
    o = _dot(p.astype(BF16), v_ref[...].astype(BF16))
    if use_ctx:
        p0 = jnp.exp(s0 - m)
        l = l + jnp.sum(p0, axis=-1, keepdims=True)
        o = o + _dot(p0.astype(BF16), vc_ref[...].astype(BF16))
    o = o / l
    for g in range(grp):
        o_ref[:, g * LANES:(g + 1) * LANES] = o[g * tq:(g + 1) * tq].astype(o_ref.dtype)


def _attention(qn, kn, proj, ctx_k, ctx_v, n_seq, seq_len, row_block0, n_q, n_kv):
    use_ctx = ctx_k is not None
    grp = n_q // n_kv
    hd = LANES
    tq = min(seq_len, 128)
    nqb = seq_len // tq
    v_blk0 = n_q + n_kv
    in_specs = [pl.BlockSpec((tq, grp * hd), lambda s, kv, i: ((row_block0 + s) * nqb + i, kv)),
                pl.BlockSpec((seq_len, hd), lambda s, kv, i: (row_block0 + s, kv)),
                pl.BlockSpec((seq_len, hd), lambda s, kv, i: (row_block0 + s, v_blk0 + kv))]
    args = [qn, kn, proj]
    if use_ctx:
        past = ctx_k.shape[1]
        in_specs += [pl.BlockSpec((None, past, hd), lambda s, kv, i: (s, 0, kv)),
                     pl.BlockSpec((None, past, hd), lambda s, kv, i: (s, 0, kv))]
        args += [ctx_k, ctx_v]
    return pl.pallas_call(
        functools.partial(_attn_kernel, grp=grp, use_ctx=use_ctx),
        out_shape=jax.ShapeDtypeStruct((n_seq * seq_len, n_q * hd), BF16),
        grid=(n_seq, n_kv, nqb),
        in_specs=in_specs,
        out_specs=pl.BlockSpec((tq, grp * hd), lambda s, kv, i: (s * nqb + i, kv)),
        compiler_params=_cparams("parallel", "parallel", "arbitrary"),
    )(*args)


def kernel(x_prompt, x_sample, state_hgrn, state_ssd, cache_k, cache_v, c, c_ctx, ada_w, ada_b, norm_w,
           hg_w_in, hg_lb_logits, hg_onorm, hg_w_o, ssd_w_in, ssd_conv_w, ssd_conv_b, ssd_a_log,
           ssd_dt_bias, ssd_d, ssd_norm, ssd_w_o, at_w_qkv, at_qn, at_kn, at_w_o, ff_w13, ff_w2,
           moe_router, moe_w13, moe_w2):
    batch, seq, d = x_prompt.shape
    n_dec, dec_seq, _ = x_sample.shape
    depth = ada_w.shape[0]
    rows = _Rows(batch * seq, dec_seq, n_dec)
    ctx_rows = rows.ctx_rows
    assert seq % CHUNK == 0 and dec_seq % CHUNK == 0 and dec_seq % seq == 0 and ctx_rows % dec_seq == 0

    hg_heads = d // LANES
    ssd_heads = ssd_a_log.shape[2]
    ssd_inner = ssd_w_o.shape[1]
    ssd_hdim = ssd_inner // ssd_heads
    ssd_nstate = state_ssd.shape[-1]
    ssd_groups = (ssd_conv_w.shape[2] - ssd_inner) // (2 * ssd_nstate)
    at_kv = cache_k.shape[3]
    at_heads = at_w_o.shape[1] // LANES
    past = cache_k.shape[2]

    x = jnp.concatenate([x_prompt.reshape(ctx_rows, d), x_sample.reshape(n_dec * dec_seq, d)], axis=0)

    n_sets = 1 + n_dec
    cond = jnp.concatenate([c_ctx[None, :], c], axis=0)
    cond = jnp.pad(cond, ((0, (-n_sets) % SUBLANES), (0, 0)))
    mod = _ada_mod(cond, ada_w, ada_b)[:, :n_sets].reshape(depth, n_sets, 6, d)
    mod = jnp.pad(mod, ((0, 0), (0, 0), (0, SUBLANES - 6), (0, 0)))

    lb_all = None
    cos_t = sin_t = None
    hg_states, ssd_states, k_list, v_list = [], [], [], []
    lat_blk_seq = ctx_rows // dec_seq

    for layer in range(depth):
        j = layer // 3
        kind = layer % 3
        mod_l = mod[layer]
        h = _norm_mod(x, norm_w[layer, 0], mod_l, 0, rows)
        if kind == 0:
            if lb_all is None:
                pr = jax.nn.softmax(hg_lb_logits.astype(F32), axis=0)
                lb_all = jnp.cumsum(pr, axis=0) - pr[0]
            proj = _matmul(h, hg_w_in[j], F32)
            lb_row = lb_all[j].reshape(1, -1)
            y_c, st_c = _hgrn_core(proj, lb_row, hg_onorm[j], None, batch, seq, 0, hg_heads)
            y_l, _ = _hgrn_core(proj, lb_row, hg_onorm[j], state_hgrn[:, j], n_dec, dec_seq,
                                lat_blk_seq, hg_heads)
            hg_states.append(st_c)
            y = jnp.concatenate([y_c, y_l], axis=0)
            x = _matmul_residual(y, hg_w_o[j], x, mod_l, 2, rows)
        elif kind == 1:
            proj = _matmul(h, ssd_w_in[j], F32)
            gn2 = 2 * ssd_groups * ssd_nstate
            xbc_blk0 = ssd_inner // LANES
            dt_blk = (2 * ssd_inner + gn2) // LANES
            d_cols = jnp.repeat((ssd_d[j, 0] + ssd_d[j, 1]).astype(F32), ssd_hdim).reshape(1, ssd_inner)
            hpg = ssd_heads // ssd_groups
            ys = []
            for (n_s, s_len, rb0, h0) in ((batch, seq, 0, None),
                                          (n_dec, dec_seq, lat_blk_seq, state_ssd[:, j])):
                xbc = _ssd_conv(proj, ssd_conv_w[j], ssd_conv_b[j], xbc_blk0, n_s, s_len, rb0)
                dta = _ssd_dt(proj, ssd_dt_bias[j], ssd_a_log[j], dt_blk, n_s, s_len, rb0)
                nc = s_len // CHUNK
                dta = dta.reshape(n_s, 2, 2, ssd_groups, hpg, nc, CHUNK).transpose(0, 1, 2, 3, 5, 4, 6)
                y_p, st_p = _ssd_core(xbc, proj, dta, d_cols, ssd_norm[j], h0, n_s, s_len, rb0,
                                      ssd_groups, ssd_heads, ssd_hdim, ssd_nstate)
                ys.append(y_p)
                if h0 is None:
                    ssd_states.append(st_p)
            y = jnp.concatenate(ys, axis=0)
            x = _matmul_residual(y, ssd_w_o[j], x, mod_l, 2, rows)
        else:
            proj = _matmul(h, at_w_qkv[j], F32)
            if cos_t is None:
                cos_t, sin_t = _rope_tables(rows, LANES)
            qn, kn = _qk_prep(proj, cos_t, sin_t, at_qn[j], at_kn[j], at_heads, at_kv, rows)
            o_c = _attention(qn, kn, proj, None, None, batch, seq, 0, at_heads, at_kv)
            ck = cache_k[:, j].reshape(n_dec, past, at_kv * LANES)
            cv = cache_v[:, j].reshape(n_dec, past, at_kv * LANES)
            o_l = _attention(qn, kn, proj, ck, cv, n_dec, dec_seq, lat_blk_seq, at_heads, at_kv)
            k_list.append(kn[:ctx_rows].reshape(batch, seq, at_kv, LANES))
            v_list.append(proj[:ctx_rows, (at_heads + at_kv) * LANES:].reshape(batch, seq, at_kv, LANES))
            y = jnp.concatenate([o_c, o_l], axis=0)
            x = _matmul_residual(y, at_w_o[j], x, mod_l, 2, rows)

        if layer % 2 == 0:
            h2 = _norm_mod(x, norm_w[layer, 1], mod_l, 1, rows)
            act = _matmul_swiglu(h2, ff_w13[layer // 2])
            x = _matmul_residual(act, ff_w2[layer // 2], x, mod_l, 5, rows)
        else:
            e = layer // 2
            n_e = moe_router.shape[2]
            tm_e = min(MOE_TILE, rows.total)
            hp, rinfo = _router(x, norm_w[layer, 1], mod_l, moe_router[e], rows)
            src, dest, tile_e, n_used = _route_tables(rinfo, n_e, tm_e)
            ys = _moe_experts(hp, moe_w13[e].astype(BF16), moe_w2[e].astype(BF16), src, tile_e,
                              n_used, tm_e)
            x = _moe_combine(ys, dest, x, rinfo, mod_l, rows)

    y_prompt = x[:ctx_rows].reshape(batch, seq, d)
    y_sample = x[ctx_rows:].reshape(n_dec, dec_seq, d)
    return (y_prompt, y_sample, jnp.stack(hg_states, axis=1), jnp.stack(ssd_states, axis=1),
            jnp.stack(k_list, axis=1), jnp.stack(v_list, axis=1))
```

```python
import functools
import math

import numpy as np
import jax
import jax.numpy as jnp
from jax import lax
from jax.experimental import pallas as pl
from jax.experimental.pallas import tpu as pltpu

F32 = jnp.float32
BF16 = jnp.bfloat16

NORM_EPS = 1e-6
GATE_FLOOR = 1e-30
ROPE_THETA = 10000.0
GRID_W = 64
LANES = 128
SUBLANES = 8
CHUNK = 128
BASE = 16
EXP_CLAMP = 80.0
NEG_BIG = -1e30
VMEM_LIMIT_BYTES = 52 * 1024 * 1024


def _cparams(*sem):
    return pltpu.CompilerParams(dimension_semantics=sem, vmem_limit_bytes=VMEM_LIMIT_BYTES)


def _sigmoid(x):
    return 1.0 / (1.0 + jnp.exp(-x))


def _silu(x):
    return x * _sigmoid(x)


def _pick_tile(n, cap, quantum=LANES):
    best = None
    t = quantum
    while t <= min(n, cap):
        if n % t == 0:
            best = t
        t += quantum
    assert best is not None, (n, cap)
    return best


def _dot(a, b):
    return jnp.dot(a, b, preferred_element_type=F32)


def _dot_nt(a, b):
    return lax.dot_general(a, b, (((1,), (1,)), ((), ())), preferred_element_type=F32)


def _split3(x):
    h1 = x.astype(BF16).astype(F32)
    r1 = x - h1
    h2 = r1.astype(BF16).astype(F32)
    h3 = (r1 - h2).astype(BF16).astype(F32)
    return h1, h2, h3


def _ada_kernel(c_ref, w_ref, b_ref, o_ref):
    s = _silu(c_ref[...]).astype(BF16)
    o_ref[...] = _dot(s, w_ref[...].astype(BF16)) + b_ref[...]


def _ada_mod(cond_rows, ada_w, ada_b):
    nl, d, n6 = ada_w.shape
    tn = _pick_tile(n6, 1024)
    rows = cond_rows.shape[0]
    return pl.pallas_call(
        _ada_kernel,
        out_shape=jax.ShapeDtypeStruct((nl, rows, n6), F32),
        grid=(nl, n6 // tn),
        in_specs=[pl.BlockSpec((rows, d), lambda l, j: (0, 0)),
                  pl.BlockSpec((None, d, tn), lambda l, j: (l, 0, j)),
                  pl.BlockSpec((None, 1, tn), lambda l, j: (l, 0, j))],
        out_specs=pl.BlockSpec((None, rows, tn), lambda l, j: (l, 0, j)),
        compiler_params=_cparams("parallel", "parallel"),
    )(cond_rows, ada_w, ada_b.reshape(nl, 1, n6))


def _modulated_norm(x, nw, mod, which):
    ms = jnp.mean(x * x, axis=-1, keepdims=True)
    y = x * lax.rsqrt(ms + NORM_EPS) * nw
    sh = mod[3 * which:3 * which + 1, :]
    sc = mod[3 * which + 1:3 * which + 2, :]
    return y * (1.0 + sc) + sh


class _Rows:
    def __init__(self, ctx_rows, dec_seq, n_dec):
        self.ctx_rows = ctx_rows
        self.dec_seq = dec_seq
        self.total = ctx_rows + dec_seq * n_dec
        self.tile = math.gcd(ctx_rows, dec_seq)

    def row_tile(self, cap):
        t = self.tile
        while t > cap and t % 2 == 0:
            t //= 2
        return t

    def cond_set(self, row0):
        return jnp.where(row0 < self.ctx_rows, 0, 1 + (row0 - self.ctx_rows) // self.dec_seq)


def _cache_weight(w_ref, wbf_ref):
    @pl.when(pl.program_id(1) == 0)
    def _():
        wbf_ref[...] = w_ref[...].astype(BF16)


def _normed_lhs(x_ref, nw_ref, mod_ref, which):
    return _modulated_norm(x_ref[...], nw_ref[...], mod_ref[...], which).astype(BF16)


def _norm_specs(rows, tm, k):
    return [pl.BlockSpec((tm, k), lambda j, i: (i, 0)),
            pl.BlockSpec((1, k), lambda j, i: (0, 0)),
            pl.BlockSpec((None, SUBLANES, k), lambda j, i: (rows.cond_set(i * tm), 0, 0))]


def _mm_plain_kernel(x_ref, nw_ref, mod_ref, w_ref, o_ref, wbf_ref, *, which):
    _cache_weight(w_ref, wbf_ref)
    h = _normed_lhs(x_ref, nw_ref, mod_ref, which)
    o_ref[...] = _dot(h, wbf_ref[...]).astype(o_ref.dtype)


def _matmul(x, nw, mod_l, which, w, layer, out_dtype, rows, tn_cap=1152):
    m, k = x.shape
    n = w.shape[2]
    tm = rows.row_tile(512)
    tn = _pick_tile(n, tn_cap)
    return pl.pallas_call(
        functools.partial(_mm_plain_kernel, which=which),
        out_shape=jax.ShapeDtypeStruct((m, n), out_dtype),
        grid=(n // tn, m // tm),
        in_specs=_norm_specs(rows, tm, k) + [pl.BlockSpec((None, k, tn), lambda j, i: (layer, 0, j))],
        out_specs=pl.BlockSpec((tm, tn), lambda j, i: (i, j)),
        scratch_shapes=[pltpu.VMEM((k, tn), BF16)],
        compiler_params=_cparams("parallel", "arbitrary"),
    )(x, nw.reshape(1, k), mod_l, w)


def _mm_res_kernel(xa_ref, xb_ref, w_ref, r_ref, mod_ref, o_ref, wbf_ref, *, gate_row, n_a):
    _cache_weight(w_ref, wbf_ref)

    def finish(x_ref):
        acc = _dot(x_ref[...], wbf_ref[...])
        o_ref[...] = r_ref[...] + mod_ref[gate_row:gate_row + 1, :] * acc

    @pl.when(pl.program_id(1) < n_a)
    def _():
        finish(xa_ref)

    @pl.when(pl.program_id(1) >= n_a)
    def _():
        finish(xb_ref)


def _matmul_residual(x_ctx, x_lat, w, layer, res, mod_l, gate_row, rows):
    if x_lat is None:
        x_lat = x_ctx
    k = x_ctx.shape[1]
    m = rows.total
    n = w.shape[2]
    tm = rows.row_tile(512 if k <= 2048 else 256)
    tn = _pick_tile(n, 1024 if k <= 2048 else 512)
    n_a = x_ctx.shape[0] // tm
    return pl.pallas_call(
        functools.partial(_mm_res_kernel, gate_row=gate_row, n_a=n_a),
        out_shape=jax.ShapeDtypeStruct((m, n), F32),
        grid=(n // tn, m // tm),
        in_specs=[pl.BlockSpec((tm, k), lambda j, i: (jnp.minimum(i, n_a - 1), 0)),
                  pl.BlockSpec((tm, k), lambda j, i: (jnp.maximum(i - n_a, 0), 0)),
                  pl.BlockSpec((None, k, tn), lambda j, i: (layer, 0, j)),
                  pl.BlockSpec((tm, tn), lambda j, i: (i, j)),
                  pl.BlockSpec((None, SUBLANES, tn), lambda j, i: (rows.cond_set(i * tm), 0, j))],
        out_specs=pl.BlockSpec((tm, tn), lambda j, i: (i, j)),
        scratch_shapes=[pltpu.VMEM((k, tn), BF16)],
        compiler_params=_cparams("parallel", "arbitrary"),
    )(x_ctx, x_lat, w, res, mod_l)


def _norm_mod_kernel(x_ref, nw_ref, mod_ref, o_ref, *, which):
    o_ref[...] = _normed_lhs(x_ref, nw_ref, mod_ref, which)


def _norm_mod(x, nw, mod_l, which, rows):
    nt, d = x.shape
    tm = rows.row_tile(256)
    return pl.pallas_call(
        functools.partial(_norm_mod_kernel, which=which),
        out_shape=jax.ShapeDtypeStruct((nt, d), BF16),
        grid=(nt // tm,),
        in_specs=[pl.BlockSpec((tm, d), lambda i: (i, 0)),
                  pl.BlockSpec((1, d), lambda i: (0, 0)),
                  pl.BlockSpec((None, SUBLANES, d), lambda i: (rows.cond_set(i * tm), 0, 0))],
        out_specs=pl.BlockSpec((tm, d), lambda i: (i, 0)),
        compiler_params=_cparams("parallel"),
    )(x, nw.reshape(1, d), mod_l)


def _mm_swiglu_kernel(x_ref, wa_ref, wb_ref, o_ref, wabf_ref, wbbf_ref):
    _cache_weight(wa_ref, wabf_ref)
    _cache_weight(wb_ref, wbbf_ref)
    x = x_ref[...]
    a = _dot(x, wabf_ref[...])
    b = _dot(x, wbbf_ref[...])
    o_ref[...] = (_silu(a) * b).astype(o_ref.dtype)


def _matmul_swiglu(x, w13, layer):
    m, k = x.shape
    f = w13.shape[2] // 2
    tm = _pick_tile(m, 512, SUBLANES)
    tn = _pick_tile(f, 512)
    nb = f // tn
    return pl.pallas_call(
        _mm_swiglu_kernel,
        out_shape=jax.ShapeDtypeStruct((m, f), BF16),
        grid=(nb, m // tm),
        in_specs=[pl.BlockSpec((tm, k), lambda j, i: (i, 0)),
                  pl.BlockSpec((None, k, tn), lambda j, i: (layer, 0, j)),
                  pl.BlockSpec((None, k, tn), lambda j, i: (layer, 0, j + nb))],
        out_specs=pl.BlockSpec((tm, tn), lambda j, i: (i, j)),
        scratch_shapes=[pltpu.VMEM((k, tn), BF16), pltpu.VMEM((k, tn), BF16)],
        compiler_params=_cparams("parallel", "arbitrary"),
    )(x, w13, w13)


MOE_TILE = 512
HI16 = 0xFFFF0000


def _router_kernel(x_ref, nw_ref, mod_ref, r_ref, hp_ref, g_ref, *, n_experts):
    h = _modulated_norm(x_ref[...], nw_ref[...], mod_ref[...], 1)
    tm = h.shape[0]
    half = h.shape[1] // 2
    nseg = half // LANES
    bits = lax.bitcast_convert_type(h.astype(BF16).astype(F32), jnp.uint32)
    packed = (bits[:, :half] >> 16) | (bits[:, half:] & jnp.uint32(HI16))
    for j in range(nseg):
        hp_ref[pl.ds(j, tm, stride=nseg), :] = packed[:, j * LANES:(j + 1) * LANES]
    h1, h2, _ = _split3(h)
    r1, r2, _ = _split3(r_ref[...])
    h1, h2, r1, r2 = (t.astype(BF16) for t in (h1, h2, r1, r2))
    logits = _dot(h1, r1) + (_dot(h1, r2) + _dot(h2, r1))
    lane = lax.broadcasted_iota(jnp.int32, logits.shape, 1)
    logits = jnp.where(lane < n_experts, logits, NEG_BIG)
    m1 = jnp.max(logits, axis=-1, keepdims=True)
    i1 = jnp.min(jnp.where(logits == m1, lane, LANES), axis=-1, keepdims=True)
    rest = jnp.where(lane == i1, NEG_BIG, logits)
    m2 = jnp.max(rest, axis=-1, keepdims=True)
    i2 = jnp.min(jnp.where(rest == m2, lane, LANES), axis=-1, keepdims=True)
    e2 = jnp.exp(m2 - m1)
    w1 = 1.0 / (1.0 + e2)
    w2 = e2 * w1
    g_ref[...] = (jnp.where(lane == 0, w1, 0.0) + jnp.where(lane == 1, w2, 0.0)
                  + jnp.where(lane == 2, i1.astype(F32), 0.0) + jnp.where(lane == 3, i2.astype(F32), 0.0))


def _router(x, nw, mod_l, router, rows):
    nt, d = x.shape
    n_experts = router.shape[1]
    tm = rows.row_tile(256)
    nseg = d // 2 // LANES
    rpad = jnp.zeros((d, LANES), F32).at[:, :n_experts].set(router)
    return pl.pallas_call(
        functools.partial(_router_kernel, n_experts=n_experts),
        out_shape=[jax.ShapeDtypeStruct((nt * nseg, LANES), jnp.uint32),
                   jax.ShapeDtypeStruct((nt, LANES), F32)],
        grid=(nt // tm,),
        in_specs=[pl.BlockSpec((tm, d), lambda i: (i, 0)),
                  pl.BlockSpec((1, d), lambda i: (0, 0)),
                  pl.BlockSpec((None, SUBLANES, d), lambda i: (rows.cond_set(i * tm), 0, 0)),
                  pl.BlockSpec((d, LANES), lambda i: (0, 0))],
        out_specs=[pl.BlockSpec((tm * nseg, LANES), lambda i: (i, 0)),
                   pl.BlockSpec((tm, LANES), lambda i: (i, 0))],
        compiler_params=_cparams("parallel"),
    )(x, nw.reshape(1, d), mod_l, rpad)


def _route_tables(rinfo, n_experts, tm):
    nt = rinfo.shape[0]
    e_flat = jnp.concatenate([rinfo[:, 2], rinfo[:, 3]]).astype(jnp.int32)
    onehot = (e_flat[:, None] == jnp.arange(n_experts, dtype=jnp.int32)[None, :]).astype(jnp.int32)
    csum = jnp.cumsum(onehot, axis=0)
    rank = jnp.sum(csum * onehot, axis=1) - 1
    padded = ((csum[-1] + tm - 1) // tm) * tm
    ends = jnp.cumsum(padded)
    dest = jnp.sum(onehot * (ends - padded)[None, :], axis=1) + rank
    p_rows = 2 * nt + n_experts * tm
    tok = jnp.tile(jnp.arange(nt, dtype=jnp.int32), 2)
    src = jnp.zeros((p_rows,), jnp.int32).at[dest].set(tok, unique_indices=True)
    tile_start = jnp.arange(p_rows // tm, dtype=jnp.int32) * tm
    tile_e = jnp.sum((tile_start[:, None] >= ends[None, :]).astype(jnp.int32), axis=1)
    tile_e = jnp.minimum(tile_e, n_experts - 1)
    n_used = (ends[-1] // tm).astype(jnp.int32).reshape(1)
    return src, dest, tile_e, n_used


def _moe_expert_kernel(te_ref, nu_ref, src_ref, hp_ref, w13_ref, w2_ref, ys_ref, xbuf, sem, *, tm):
    del te_ref
    i = pl.program_id(0)
    n_used = nu_ref[0]
    nseg = xbuf.shape[1] // tm
    nout = ys_ref.shape[0] // tm

    def row_copy(tile, slot, r):
        s0 = pl.multiple_of(src_ref[tile * tm + r] * nseg, nseg)
        d0 = pl.multiple_of(r * nseg, nseg)
        return pltpu.make_async_copy(hp_ref.at[pl.ds(s0, nseg)], xbuf.at[slot, pl.ds(d0, nseg)],
                                     sem.at[slot])

    def issue(tile, slot):
        def body(r, carry):
            row_copy(tile, slot, r).start()
            return carry
        lax.fori_loop(0, tm, body, 0, unroll=8)

    @pl.when(i == 0)
    def _():
        issue(0, 0)

    @pl.when(i + 1 < n_used)
    def _():
        issue(i + 1, (i + 1) % 2)

    @pl.when(i < n_used)
    def _():
        slot = i % 2

        def wait_body(r, carry):
            row_copy(i, slot, r).wait()
            return carry
        lax.fori_loop(0, tm, wait_body, 0, unroll=8)
        xb = xbuf.at[slot]
        segs = [xb[pl.ds(j, tm, stride=nseg), :] for j in range(nseg)]
        lo = [lax.bitcast_convert_type(u << 16, F32) for u in segs]
        hi = [lax.bitcast_convert_type(u & jnp.uint32(HI16), F32) for u in segs]
        x = jnp.concatenate(lo + hi, axis=1).astype(BF16)
        f = w2_ref.shape[0]
        a = _dot(x, w13_ref[:, :f])
        b = _dot(x, w13_ref[:, f:])
        y = _dot((_silu(a) * b).astype(BF16), w2_ref[...])
        for j in range(nout):
            ys_ref[pl.ds(j, tm, stride=nout), :] = y[:, j * LANES:(j + 1) * LANES]

    @pl.when(i >= n_used)
    def _():
        ys_ref[...] = jnp.zeros(ys_ref.shape, ys_ref.dtype)


def _moe_experts(hp, w13, w2, src, tile_e, n_used, tm):
    n_e, d, f2 = w13.shape
    f = f2 // 2
    p_rows = src.shape[0]
    nseg = d // 2 // LANES
    nout = d // LANES
    grid_spec = pltpu.PrefetchScalarGridSpec(
        num_scalar_prefetch=3,
        grid=(p_rows // tm,),
        in_specs=[pl.BlockSpec(memory_space=pl.ANY),
                  pl.BlockSpec((None, d, f2), lambda i, te, nu, sr: (te[i], 0, 0),
                               pipeline_mode=pl.Buffered(1)),
                  pl.BlockSpec((None, f, d), lambda i, te, nu, sr: (te[i], 0, 0),
                               pipeline_mode=pl.Buffered(1))],
        out_specs=pl.BlockSpec((tm * nout, LANES), lambda i, te, nu, sr: (i, 0)),
        scratch_shapes=[pltpu.VMEM((2, tm * nseg, LANES), jnp.uint32), pltpu.SemaphoreType.DMA((2,))])
    return pl.pallas_call(
        functools.partial(_moe_expert_kernel, tm=tm),
        out_shape=jax.ShapeDtypeStruct((p_rows * nout, LANES), F32),
        grid_spec=grid_spec,
        compiler_params=_cparams("arbitrary"),
    )(tile_e, n_used, src, hp, w13, w2)


def _moe_combine_kernel(dest_ref, ys_ref, x_ref, g_ref, mod_ref, o_ref, ybuf, sem, *, tm, nt, n_steps):
    i = pl.program_id(0)
    nout = ybuf.shape[2] // tm

    def row_copy(tile, slot, k, r):
        s0 = pl.multiple_of(dest_ref[k * nt + tile * tm + r] * nout, nout)
        d0 = pl.multiple_of(r * nout, nout)
        return pltpu.make_async_copy(ys_ref.at[pl.ds(s0, nout)], ybuf.at[slot, k, pl.ds(d0, nout)],
                                     sem.at[slot])

    def issue(tile, slot):
        def body(r, carry):
            row_copy(tile, slot, 0, r).start()
            row_copy(tile, slot, 1, r).start()
            return carry
        lax.fori_loop(0, tm, body, 0, unroll=4)

    @pl.when(i == 0)
    def _():
        issue(0, 0)

    @pl.when(i + 1 < n_steps)
    def _():
        issue(i + 1, (i + 1) % 2)

    slot = i % 2

    def wait_body(r, carry):
        row_copy(i, slot, 0, r).wait()
        row_copy(i, slot, 1, r).wait()
        return carry
    lax.fori_loop(0, tm, wait_body, 0, unroll=4)
    g = g_ref[...]
    w0 = g[:, 0:1]
    w1 = g[:, 1:2]
    y0 = ybuf.at[slot, 0]
    y1 = ybuf.at[slot, 1]
    for j in range(nout):
        cols = slice(j * LANES, (j + 1) * LANES)
        mix = w0 * y0[pl.ds(j, tm, stride=nout), :] + w1 * y1[pl.ds(j, tm, stride=nout), :]
        o_ref[:, cols] = x_ref[:, cols] + mod_ref[5:6, cols] * mix


def _moe_combine(ys, dest, x, rinfo, mod_l, rows):
    nt, d = x.shape
    tm = rows.row_tile(256)
    n_steps = nt // tm
    grid_spec = pltpu.PrefetchScalarGridSpec(
        num_scalar_prefetch=1,
        grid=(n_steps,),
        in_specs=[pl.BlockSpec(memory_space=pl.ANY),
                  pl.BlockSpec((tm, d), lambda i, ds: (i, 0)),
                  pl.BlockSpec((tm, LANES), lambda i, ds: (i, 0)),
                  pl.BlockSpec((None, SUBLANES, d), lambda i, ds: (rows.cond_set(i * tm), 0, 0))],
        out_specs=pl.BlockSpec((tm, d), lambda i, ds: (i, 0)),
        scratch_shapes=[pltpu.VMEM((2, 2, tm * (d // LANES), LANES), F32),
                        pltpu.SemaphoreType.DMA((2,))])
    return pl.pallas_call(
        functools.partial(_moe_combine_kernel, tm=tm, nt=nt, n_steps=n_steps),
        out_shape=jax.ShapeDtypeStruct((nt, d), F32),
        grid_spec=grid_spec,
        compiler_params=_cparams("arbitrary"),
    )(dest, ys, x, rinfo, mod_l)


def _tri_pair(n):
    lo = np.tril(np.ones((n, n), np.float32))
    return jnp.asarray(np.stack([lo, lo.T]), BF16)


def _level_halves():
    hs = []
    h = CHUNK // 2
    while h >= BASE:
        hs.append(h)
        h //= 2
    return hs


def _gla_masks():
    t = np.arange(CHUNK)[:, None]
    s = np.arange(CHUNK)[None, :]
    out = []
    for rev in (False, True):
        per = []
        for h in _level_halves():
            same = (t // (2 * h)) == (s // (2 * h))
            t_up = (t % (2 * h)) >= h
            s_up = (s % (2 * h)) >= h
            per.append(same & (~t_up & s_up if rev else t_up & ~s_up))
        same = (t // BASE) == (s // BASE)
        per.append(same & ((s >= t) if rev else (s <= t)))
        out.append(np.stack(per))
    return jnp.asarray(np.stack(out).astype(np.float32))


def _chunk_cumsum(tri, x):
    n = x.shape[1]
    h1 = x.astype(BF16)
    h2 = (x - h1.astype(F32)).astype(BF16)
    p = _dot(tri, jnp.concatenate([h1, h2], axis=1))
    return p[:, n:] + p[:, :n]


def _hgrn_chunks(chains):
    halves = _level_halves()
    c = chains[0]["q"].shape[0]
    for ch in chains:
        z, lb = ch["z"], ch["lb"]
        one_m_lb = 1.0 - lb
        e = jnp.exp(-jnp.abs(z))
        r = 1.0 / (1.0 + e)
        er = e * r
        pos = z >= 0
        f = lb + one_m_lb * jnp.where(pos, r, er)
        ch["logf"] = jnp.log(jnp.maximum(f, GATE_FLOOR))
        ch["key"] = one_m_lb * jnp.where(pos, er, r)
    for ch in chains:
        ch["b"] = _chunk_cumsum(ch["tri"], ch["logf"])
    for ch in chains:
        ch["scores"] = None
    for lvl, h in enumerate(halves):
        for ch in chains:
            b = ch["b"]
            b3 = b.reshape(c // (2 * h), 2 * h, LANES)
            r = h if ch["rev"] else h - 1
            w = jnp.exp(-jnp.abs(b3 - b3[:, r:r + 1, :])).reshape(c, LANES)
            part = _dot_nt((ch["q"] * w).astype(BF16), (ch["key"] * w).astype(BF16)) * ch["masks"][lvl]
            ch["scores"] = part if ch["scores"] is None else ch["scores"] + part
    for ch in chains:
        b3 = ch["b"].reshape(c // BASE, BASE, LANES)
        l3 = ch["logf"].reshape(c // BASE, BASE, LANES)
        r = BASE - 1 if ch["rev"] else 0
        eq = (b3 - (b3[:, r:r + 1, :] - l3[:, r:r + 1, :])).reshape(c, LANES)
        qb = (ch["q"] * jnp.exp(eq)).astype(BF16)
        kb = (ch["key"] * jnp.exp(jnp.minimum(-eq, EXP_CLAMP))).astype(BF16)
        ch["scores"] = ch["scores"] + _dot_nt(qb, kb) * ch["masks"][len(halves)]
    out = []
    for ch in chains:
        b, st, v = ch["b"], ch["st"], ch["v"]
        o = _dot(ch["scores"].astype(BF16), v.astype(BF16)) + \
            _dot_nt((ch["q"] * jnp.exp(b)).astype(BF16), st.astype(BF16))
        r_last = 0 if ch["rev"] else c - 1
        b_last = b[r_last:r_last + 1, :]
        kt = (ch["key"] * jnp.exp(b_last - b)).astype(BF16)
        out.append((o, st * jnp.exp(b_last) + _dot(v.T.astype(BF16), kt)))
    return out


HG_HEADS_PER_STEP = 2


def _hgrn_kernel(*refs, seq_len, use_s0, hps):
    if use_s0:
        (q_ref, zf_ref, zb_ref, v_ref, g_ref, lb_ref, on_ref, tri_ref, msk_ref, s0f_ref, s0b_ref,
         y_ref, sf_ref, sb_ref, of_ref, ob_ref) = refs
    else:
        (q_ref, zf_ref, zb_ref, v_ref, g_ref, lb_ref, on_ref, tri_ref, msk_ref,
         y_ref, sf_ref, sb_ref, of_ref, ob_ref) = refs
    nc = seq_len // CHUNK
    lanes = [slice(hh * LANES, (hh + 1) * LANES) for hh in range(hps)]

    def rows_of(ci):
        return pl.ds(pl.multiple_of(ci * CHUNK, CHUNK), CHUNK)

    def body(i, sts):
        rf = rows_of(i)
        rb = rows_of(nc - 1 - i)
        chains = []
        for hh, ln in enumerate(lanes):
            chains.append(dict(q=q_ref[rf, ln], z=zf_ref[rf, ln], v=v_ref[rf, ln], lb=lb_ref[:, ln],
                               tri=tri_ref[0], masks=msk_ref.at[0], st=sts[hh], rev=False))
        for hh, ln in enumerate(lanes):
            chains.append(dict(q=q_ref[rb, ln], z=zb_ref[rb, ln], v=v_ref[rb, ln], lb=lb_ref[:, ln],
                               tri=tri_ref[1], masks=msk_ref.at[1], st=sts[hps + hh], rev=True))
        res = _hgrn_chunks(chains)
        for hh, ln in enumerate(lanes):
            of_ref[rf, ln] = res[hh][0]
            ob_ref[rb, ln] = res[hps + hh][0]
        return tuple(st for _, st in res)

    zero = jnp.zeros((LANES, LANES), F32)
    init = tuple((s0f_ref[hh].T if use_s0 else zero) for hh in range(hps)) + \
        tuple((s0b_ref[hh].T if use_s0 else zero) for hh in range(hps))
    sts = lax.fori_loop(0, nc, body, init)
    for hh in range(hps):
        sf_ref[hh] = sts[hh].T
        sb_ref[hh] = sts[hps + hh].T

    def finish(ci, carry):
        rows = rows_of(ci)
        for ln in lanes:
            o = of_ref[rows, ln] + ob_ref[rows, ln]
            ms = jnp.mean(o * o, axis=-1, keepdims=True)
            y = o * lax.rsqrt(ms + NORM_EPS) * on_ref[...]
            y_ref[rows, ln] = (y * _silu(g_ref[rows, ln])).astype(y_ref.dtype)
        return carry

    lax.fori_loop(0, nc, finish, 0)


def _hgrn_core(proj, lb_row, onorm, s0, n_seq, seq_len, row_block0, n_heads):
    use_s0 = s0 is not None
    hd = LANES
    hps = HG_HEADS_PER_STEP * (2 if seq_len * 4 <= 1024 else 1)
    assert n_heads % hps == 0
    nhb = n_heads // hps
    wd = hps * hd

    def col(off):
        return pl.BlockSpec((seq_len, wd), lambda s, h, off=off: (row_block0 + s, off * nhb + h))

    in_specs = [col(0), col(1), col(2), col(3), col(4),
                pl.BlockSpec((1, wd), lambda s, h: (0, h)),
                pl.BlockSpec((1, hd), lambda s, h: (0, 0)),
                pl.BlockSpec((2, CHUNK, CHUNK), lambda s, h: (0, 0, 0)),
                pl.BlockSpec((2, len(_level_halves()) + 1, CHUNK, CHUNK), lambda s, h: (0, 0, 0, 0))]
    args = [proj] * 5 + [lb_row, onorm.reshape(1, hd), _tri_pair(CHUNK), _gla_masks()]
    if use_s0:
        in_specs += [pl.BlockSpec((None, None, hps, hd, hd), lambda s, h: (s, 0, h, 0, 0)),
                     pl.BlockSpec((None, None, hps, hd, hd), lambda s, h: (s, 1, h, 0, 0))]
        args += [s0, s0]
    st_spec = pl.BlockSpec((None, hps, hd, hd), lambda s, h: (s, h, 0, 0))
    y, sf, sb = pl.pallas_call(
        functools.partial(_hgrn_kernel, seq_len=seq_len, use_s0=use_s0, hps=hps),
        out_shape=[jax.ShapeDtypeStruct((n_seq * seq_len, n_heads * hd), BF16),
                   jax.ShapeDtypeStruct((n_seq, n_heads, hd, hd), F32),
                   jax.ShapeDtypeStruct((n_seq, n_heads, hd, hd), F32)],
        grid=(n_seq, nhb),
        in_specs=in_specs,
        out_specs=[pl.BlockSpec((seq_len, wd), lambda s, h: (s, h)), st_spec, st_spec],
        scratch_shapes=[pltpu.VMEM((seq_len, wd), F32), pltpu.VMEM((seq_len, wd), F32)],
        compiler_params=_cparams("parallel", "parallel"),
    )(*args)
    return y, jnp.stack([sf, sb], axis=1)


def _ssd_conv_kernel(x_ref, w_ref, b_ref, o_ref, *, seq_len, n_taps):
    x = x_ref[...]
    row = lax.broadcasted_iota(jnp.int32, x.shape, 0)
    half = n_taps // 2
    acc = x * w_ref[half:half + 1, :] + b_ref[...]
    for j in range(n_taps):
        d = j - half
        if d == 0:
            continue
        shifted = pltpu.roll(x, (-d) % seq_len, 0)
        valid = (row + d >= 0) & (row + d < seq_len)
        acc = acc + jnp.where(valid, shifted, 0.0) * w_ref[j:j + 1, :]
    o_ref[...] = _silu(acc)


def _ssd_conv(proj, conv_w, conv_b, col_block0, n_seq, seq_len, row_block0):
    n_taps, c = conv_w.shape
    tc = _pick_tile(math.gcd(c, col_block0 * LANES), max(256, (512 * 1024) // seq_len))
    assert (col_block0 * LANES) % tc == 0
    cb0 = col_block0 * LANES // tc
    return pl.pallas_call(
        functools.partial(_ssd_conv_kernel, seq_len=seq_len, n_taps=n_taps),
        out_shape=jax.ShapeDtypeStruct((n_seq * seq_len, c), F32),
        grid=(n_seq, c // tc),
        in_specs=[pl.BlockSpec((seq_len, tc), lambda s, j: (row_block0 + s, cb0 + j)),
                  pl.BlockSpec((n_taps, tc), lambda s, j: (0, j)),
                  pl.BlockSpec((1, tc), lambda s, j: (0, j))],
        out_specs=pl.BlockSpec((seq_len, tc), lambda s, j: (s, j)),
        compiler_params=_cparams("parallel", "parallel"),
    )(proj, conv_w, conv_b.reshape(1, c))


def _ssd_dt_kernel(d_ref, bias_ref, a_ref, o_ref):
    x = d_ref[...] + bias_ref[...]
    dt = jnp.maximum(x, 0.0) + jnp.log1p(jnp.exp(-jnp.abs(x)))
    o_ref[0] = dt.T
    o_ref[1] = (dt * a_ref[...]).T


def _ssd_dt(proj, dt_bias, a_log, col_block, n_seq, seq_len, row_block0):
    nh2 = dt_bias.size
    assert nh2 == LANES
    a_row = (-jnp.exp(a_log.astype(F32))).reshape(1, nh2)
    return pl.pallas_call(
        _ssd_dt_kernel,
        out_shape=jax.ShapeDtypeStruct((n_seq, 2, nh2, seq_len), F32),
        grid=(n_seq,),
        in_specs=[pl.BlockSpec((seq_len, nh2), lambda s: (row_block0 + s, col_block)),
                  pl.BlockSpec((1, nh2), lambda s: (0, 0)),
                  pl.BlockSpec((1, nh2), lambda s: (0, 0))],
        out_specs=pl.BlockSpec((None, 2, nh2, seq_len), lambda s: (s, 0, 0, 0)),
        compiler_params=_cparams("parallel"),
    )(proj, dt_bias.reshape(1, nh2).astype(F32), a_row)


def _ssd_chunks(chains, hpg, hdim):
    c = chains[0]["xs"].shape[0]
    pair = LANES // hdim
    ti = lax.broadcasted_iota(jnp.int32, (c, c), 0)
    si = lax.broadcasted_iota(jnp.int32, (c, c), 1)
    lane = lax.broadcasted_iota(jnp.int32, (c, LANES), 1)
    sels = [(lane >= u * hdim) & (lane < (u + 1) * hdim) for u in range(pair)]
    pad = jnp.zeros((LANES - 3 * hpg, c), F32)
    for ch in chains:
        dt_t, a_t = ch["dta"][0], ch["dta"][1]
        acs_t = _chunk_cumsum_rows(a_t, ch["tri"])
        r_last = 0 if ch["rev"] else c - 1
        a_last = acs_t[:, r_last:r_last + 1]
        ch["acs_t"] = acs_t
        ch["dt_t"] = dt_t
        ch["cdec"] = jnp.exp(a_last)
        ch["cols"] = jnp.concatenate([acs_t, dt_t * jnp.exp(a_last - acs_t), jnp.exp(acs_t), pad],
                                     axis=0).T
        ch["causal"] = (si >= ti) if ch["rev"] else (si <= ti)
    for ch in chains:
        cmb = ch["cm"].astype(BF16)
        ch["cb"] = _dot_nt(cmb, ch["bm"].astype(BF16))
        ch["y_inter"] = _dot(cmb, ch["h"].astype(BF16))
        ch["bt"] = ch["bm"].T.astype(BF16)
        ch["ys"] = []
        ch["hs"] = []
    for p in range(hpg // pair):
        for ch in chains:
            cols, acs_t, dt_t = ch["cols"], ch["acs_t"], ch["dt_t"]
            ms = []
            dtd_col = e_col = cdec_row = None
            for u in range(pair):
                hh = p * pair + u
                seg = jnp.where(ch["causal"], cols[:, hh:hh + 1] - acs_t[hh:hh + 1, :], NEG_BIG)
                ms.append((ch["cb"] * jnp.exp(seg) * dt_t[hh:hh + 1, :]).astype(BF16))
                dtd = cols[:, hpg + hh:hpg + hh + 1]
                ec = cols[:, 2 * hpg + hh:2 * hpg + hh + 1]
                cd = ch["cdec"][hh:hh + 1, :]
                dtd_col = dtd if dtd_col is None else jnp.where(sels[u], dtd, dtd_col)
                e_col = ec if e_col is None else jnp.where(sels[u], ec, e_col)
                cdec_row = cd if cdec_row is None else jnp.where(sels[u][:1, :], cd, cdec_row)
            x2 = ch["xs"][:, p * LANES:(p + 1) * LANES]
            rhs = [jnp.where(sels[u], x2, 0.0).astype(BF16) for u in range(pair)]
            y_intra = _dot(jnp.concatenate(ms, axis=1), jnp.concatenate(rhs, axis=0))
            ch["ys"].append(y_intra + ch["y_inter"][:, p * LANES:(p + 1) * LANES] * e_col)
            upd = _dot(ch["bt"], (x2 * dtd_col).astype(BF16))
            ch["hs"].append(ch["h"][:, p * LANES:(p + 1) * LANES] * cdec_row + upd)
    return [(jnp.concatenate(ch["ys"], axis=1), jnp.concatenate(ch["hs"], axis=1)) for ch in chains]


def _chunk_cumsum_rows(a_t, tri):
    h = a_t.shape[0]
    parts = jnp.concatenate(_split3(a_t), axis=0).astype(BF16)
    p = _dot(parts, tri)
    return (p[2 * h:] + p[h:2 * h]) + p[:h]


def _ssd_kernel(*refs, seq_len, use_h0, hpg, hdim):
    if use_h0:
        (xs_ref, bm_ref, cm_ref, z_ref, dta_ref, dsk_ref, nw_ref, tri_ref, h0f_ref, h0b_ref,
         y_ref, hf_ref, hb_ref, yf_ref, yb_ref) = refs
    else:
        (xs_ref, bm_ref, cm_ref, z_ref, dta_ref, dsk_ref, nw_ref, tri_ref,
         y_ref, hf_ref, hb_ref, yf_ref, yb_ref) = refs
    nc = seq_len // CHUNK
    width = hpg * hdim
    n_state = bm_ref.shape[1]

    def rows_of(ci):
        return pl.ds(pl.multiple_of(ci * CHUNK, CHUNK), CHUNK)

    def body(i, carry):
        h_f, h_b = carry
        rf = rows_of(i)
        cb = nc - 1 - i
        rb = rows_of(cb)
        (y_f, h_f), (y_b, h_b) = _ssd_chunks(
            [dict(xs=xs_ref[rf, :], bm=bm_ref[rf, :], cm=cm_ref[rf, :], dta=dta_ref[:, 0, i],
                  tri=tri_ref[1], h=h_f, rev=False),
             dict(xs=xs_ref[rb, :], bm=bm_ref[rb, :], cm=cm_ref[rb, :], dta=dta_ref[:, 1, cb],
                  tri=tri_ref[0], h=h_b, rev=True)], hpg, hdim)
        yf_ref[rf, :] = y_f
        yb_ref[rb, :] = y_b
        return h_f, h_b

    def load_state(ref):
        return ref[...].reshape(width, n_state).T

    zero = jnp.zeros((n_state, width), F32)
    init = (load_state(h0f_ref), load_state(h0b_ref)) if use_h0 else (zero, zero)
    h_f, h_b = lax.fori_loop(0, nc, body, init)
    hf_ref[...] = h_f.T.reshape(hpg, hdim, n_state)
    hb_ref[...] = h_b.T.reshape(hpg, hdim, n_state)

    def finish(ci, carry):
        rows = rows_of(ci)
        y = yf_ref[rows, :] + yb_ref[rows, :] + dsk_ref[...] * xs_ref[rows, :]
        y = y * _silu(z_ref[rows, :])
        ms = jnp.mean(y * y, axis=-1, keepdims=True)
        y_ref[rows, :] = (y * lax.rsqrt(ms + NORM_EPS) * nw_ref[...]).astype(y_ref.dtype)
        return carry

    lax.fori_loop(0, nc, finish, 0)


def _ssd_core(xbc, proj, dta, d_cols, norm_w, h0, n_seq, seq_len, row_block0, n_groups, n_heads,
              hdim, n_state):
    use_h0 = h0 is not None
    hpg = n_heads // n_groups
    width = hpg * hdim
    inner = n_heads * hdim
    assert n_state == LANES and width % LANES == 0 and inner % width == 0
    nc = seq_len // CHUNK
    b_blk0 = inner // n_state
    c_blk0 = b_blk0 + n_groups
    in_specs = [pl.BlockSpec((seq_len, width), lambda s, g: (s, g)),
                pl.BlockSpec((seq_len, n_state), lambda s, g: (s, b_blk0 + g)),
                pl.BlockSpec((seq_len, n_state), lambda s, g: (s, c_blk0 + g)),
                pl.BlockSpec((seq_len, width), lambda s, g: (row_block0 + s, g)),
                pl.BlockSpec((None, 2, 2, None, nc, hpg, CHUNK), lambda s, g: (s, 0, 0, g, 0, 0, 0)),
                pl.BlockSpec((1, width), lambda s, g: (0, g)),
                pl.BlockSpec((1, width), lambda s, g: (0, g)),
                pl.BlockSpec((2, CHUNK, CHUNK), lambda s, g: (0, 0, 0))]
    args = [xbc, xbc, xbc, proj, dta, d_cols, norm_w.reshape(1, inner), _tri_pair(CHUNK)]
    if use_h0:
        st_in = (None, None, hpg, hdim, n_state)
        in_specs += [pl.BlockSpec(st_in, lambda s, g: (s, 0, g, 0, 0)),
                     pl.BlockSpec(st_in, lambda s, g: (s, 1, g, 0, 0))]
        args += [h0, h0]
    st_spec = pl.BlockSpec((None, hpg, hdim, n_state), lambda s, g: (s, g, 0, 0))
    y, hf, hb = pl.pallas_call(
        functools.partial(_ssd_kernel, seq_len=seq_len, use_h0=use_h0, hpg=hpg, hdim=hdim),
        out_shape=[jax.ShapeDtypeStruct((n_seq * seq_len, inner), BF16),
                   jax.ShapeDtypeStruct((n_seq, n_heads, hdim, n_state), F32),
                   jax.ShapeDtypeStruct((n_seq, n_heads, hdim, n_state), F32)],
        grid=(n_seq, n_groups),
        in_specs=in_specs,
        out_specs=[pl.BlockSpec((seq_len, width), lambda s, g: (s, g)), st_spec, st_spec],
        scratch_shapes=[pltpu.VMEM((seq_len, width), F32), pltpu.VMEM((seq_len, width), F32)],
        compiler_params=_cparams("parallel", "parallel"),
    )(*args)
    return y, jnp.stack([hf, hb], axis=1)


def _rope_tables(rows, head_dim):
    quarter = head_dim // 4
    inv = ROPE_THETA ** (-np.arange(quarter, dtype=np.float64) / quarter)
    t = np.arange(rows.dec_seq)
    ang_r = (t // GRID_W)[:, None] * inv[None, :]
    ang_c = (t % GRID_W)[:, None] * inv[None, :]
    cos = np.concatenate([np.cos(ang_r)] * 2 + [np.cos(ang_c)] * 2, axis=1)
    sin = np.concatenate([-np.sin(ang_r), np.sin(ang_r), -np.sin(ang_c), np.sin(ang_c)], axis=1)
    n_dec = (rows.total - rows.ctx_rows) // rows.dec_seq
    cos = np.concatenate([np.ones((rows.ctx_rows, head_dim))] + [cos] * n_dec, axis=0)
    sin = np.concatenate([np.zeros((rows.ctx_rows, head_dim))] + [sin] * n_dec, axis=0)
    return jnp.asarray(cos, F32), jnp.asarray(sin, F32)


def _qk_prep_kernel(p_ref, cos_ref, sin_ref, qn_ref, kn_ref, q_ref, k_ref, *, n_q, n_kv, scale):
    cos = cos_ref[...]
    sin = sin_ref[...]
    lane = lax.broadcasted_iota(jnp.int32, cos.shape, 1)
    first = (lane % (LANES // 2)) < (LANES // 4)

    def norm_rope(x, w):
        ms = jnp.mean(x * x, axis=-1, keepdims=True)
        y = x * lax.rsqrt(ms + NORM_EPS) * w
        partner = jnp.where(first, pltpu.roll(y, LANES - LANES // 4, 1), pltpu.roll(y, LANES // 4, 1))
        return y * cos + partner * sin

    for h in range(n_q):
        x = p_ref[:, h * LANES:(h + 1) * LANES]
        q_ref[:, h * LANES:(h + 1) * LANES] = (norm_rope(x, qn_ref[...]) * scale).astype(q_ref.dtype)
    for h in range(n_kv):
        x = p_ref[:, (n_q + h) * LANES:(n_q + h + 1) * LANES]
        k_ref[:, h * LANES:(h + 1) * LANES] = norm_rope(x, kn_ref[...])


def _qk_prep(proj, cos, sin, qn, kn, n_q, n_kv, rows):
    nt = proj.shape[0]
    hd = LANES
    tm = rows.row_tile(256)
    scale = float(hd) ** -0.5
    return pl.pallas_call(
        functools.partial(_qk_prep_kernel, n_q=n_q, n_kv=n_kv, scale=scale),
        out_shape=[jax.ShapeDtypeStruct((nt, n_q * hd), BF16),
                   jax.ShapeDtypeStruct((nt, n_kv * hd), F32)],
        grid=(nt // tm,),
        in_specs=[pl.BlockSpec((tm, proj.shape[1]), lambda i: (i, 0)),
                  pl.BlockSpec((tm, hd), lambda i: (i, 0)),
                  pl.BlockSpec((tm, hd), lambda i: (i, 0)),
                  pl.BlockSpec((1, hd), lambda i: (0, 0)),
                  pl.BlockSpec((1, hd), lambda i: (0, 0))],
        out_specs=[pl.BlockSpec((tm, n_q * hd), lambda i: (i, 0)),
                   pl.BlockSpec((tm, n_kv * hd), lambda i: (i, 0))],
        compiler_params=_cparams("parallel"),
    )(proj, cos, sin, qn.reshape(1, hd), kn.reshape(1, hd))


def _attn_kernel(*refs, grp, use_ctx):
    if use_ctx:
        q_ref, k_ref, v_ref, kc_ref, vc_ref, o_ref = refs
    else:
        q_ref, k_ref, v_ref, o_ref = refs
    tq = q_ref.shape[0]
    q = jnp.concatenate([q_ref[:, g * LANES:(g + 1) * LANES] for g in range(grp)], axis=0)
    s = _dot_nt(q, k_ref[...].astype(BF16))
    m = jnp.max(s, axis=-1, keepdims=True)
    if use_ctx:
        s0 = _dot_nt(q, kc_ref[...].astype(BF16))
        m = jnp.maximum(m, jnp.max(s0, axis=-1, keepdims=True))
    p = jnp.exp(s - m)
    l = jnp.sum(p, axis=-1, keepdims=True)
    o = _dot(p.astype(BF16), v_ref[...].astype(BF16))
    if use_ctx:
        p0 = jnp.exp(s0 - m)
        l = l + jnp.sum(p0, axis=-1, keepdims=True)
        o = o + _dot(p0.astype(BF16), vc_ref[...].astype(BF16))
    o = o / l
    for g in range(grp):
        o_ref[:, g * LANES:(g + 1) * LANES] = o[g * tq:(g + 1) * tq].astype(o_ref.dtype)


def _attention(qn, kn, proj, ctx_k, ctx_v, n_seq, seq_len, row_block0, n_q, n_kv):
    use_ctx = ctx_k is not None
    grp = n_q // n_kv
    hd = LANES
    tq = min(seq_len, 128)
    nqb = seq_len // tq
    v_blk0 = n_q + n_kv
    in_specs = [pl.BlockSpec((tq, grp * hd), lambda s, kv, i: ((row_block0 + s) * nqb + i, kv)),
                pl.BlockSpec((seq_len, hd), lambda s, kv, i: (row_block0 + s, kv)),
                pl.BlockSpec((seq_len, hd), lambda s, kv, i: (row_block0 + s, v_blk0 + kv))]
    args = [qn, kn, proj]
    if use_ctx:
        past = ctx_k.shape[1]
        in_specs += [pl.BlockSpec((None, past, hd), lambda s, kv, i: (s, 0, kv)),
                     pl.BlockSpec((None, past, hd), lambda s, kv, i: (s, 0, kv))]
        args += [ctx_k, ctx_v]
    return pl.pallas_call(
        functools.partial(_attn_kernel, grp=grp, use_ctx=use_ctx),
        out_shape=jax.ShapeDtypeStruct((n_seq * seq_len, n_q * hd), BF16),
        grid=(n_seq, n_kv, nqb),
        in_specs=in_specs,
        out_specs=pl.BlockSpec((tq, grp * hd), lambda s, kv, i: (s * nqb + i, kv)),
        compiler_params=_cparams("parallel", "parallel", "arbitrary"),
    )(*args)


def kernel(x_prompt, x_sample, state_hgrn, state_ssd, cache_k, cache_v, c, c_ctx, ada_w, ada_b, norm_w,
           hg_w_in, hg_lb_logits, hg_onorm, hg_w_o, ssd_w_in, ssd_conv_w, ssd_conv_b, ssd_a_log,
           ssd_dt_bias, ssd_d, ssd_norm, ssd_w_o, at_w_qkv, at_qn, at_kn, at_w_o, ff_w13, ff_w2,
           moe_router, moe_w13, moe_w2):
    batch, seq, d = x_prompt.shape
    n_dec, dec_seq, _ = x_sample.shape
    depth = ada_w.shape[0]
    rows = _Rows(batch * seq, dec_seq, n_dec)
    ctx_rows = rows.ctx_rows
    assert seq % CHUNK == 0 and dec_seq % CHUNK == 0 and dec_seq % seq == 0 and ctx_rows % dec_seq == 0

    hg_heads = d // LANES
    ssd_heads = ssd_a_log.shape[2]
    ssd_inner = ssd_w_o.shape[1]
    ssd_hdim = ssd_inner // ssd_heads
    ssd_nstate = state_ssd.shape[-1]
    ssd_groups = (ssd_conv_w.shape[2] - ssd_inner) // (2 * ssd_nstate)
    at_kv = cache_k.shape[3]
    at_heads = at_w_o.shape[1] // LANES
    past = cache_k.shape[2]

    x = jnp.concatenate([x_prompt.reshape(ctx_rows, d), x_sample.reshape(n_dec * dec_seq, d)], axis=0)

    n_sets = 1 + n_dec
    cond = jnp.concatenate([c_ctx[None, :], c], axis=0)
    cond = jnp.pad(cond, ((0, (-n_sets) % SUBLANES), (0, 0)))
    mod = _ada_mod(cond, ada_w, ada_b)[:, :n_sets].reshape(depth, n_sets, 6, d)
    mod = jnp.pad(mod, ((0, 0), (0, 0), (0, SUBLANES - 6), (0, 0)))

    lb_all = None
    cos_t = sin_t = None
    hg_states, ssd_states, k_list, v_list = [], [], [], []
    lat_blk_seq = ctx_rows // dec_seq

    for layer in range(depth):
        j = layer // 3
        kind = layer % 3
        mod_l = mod[layer]
        nw1 = norm_w[layer, 0]
        if kind == 0:
            if lb_all is None:
                pr = jax.nn.softmax(hg_lb_logits.astype(F32), axis=0)
                lb_all = jnp.cumsum(pr, axis=0) - pr[0]
            proj = _matmul(x, nw1, mod_l, 0, hg_w_in, j, F32, rows)
            lb_row = lb_all[j].reshape(1, -1)
            y_c, st_c = _hgrn_core(proj, lb_row, hg_onorm[j], None, batch, seq, 0, hg_heads)
            y_l, _ = _hgrn_core(proj, lb_row, hg_onorm[j], state_hgrn[:, j], n_dec, dec_seq,
                                lat_blk_seq, hg_heads)
            hg_states.append(st_c)
            x = _matmul_residual(y_c, y_l, hg_w_o, j, x, mod_l, 2, rows)
        elif kind == 1:
            proj = _matmul(x, nw1, mod_l, 0, ssd_w_in, j, F32, rows)
            gn2 = 2 * ssd_groups * ssd_nstate
            xbc_blk0 = ssd_inner // LANES
            dt_blk = (2 * ssd_inner + gn2) // LANES
            d_cols = jnp.repeat((ssd_d[j, 0] + ssd_d[j, 1]).astype(F32), ssd_hdim).reshape(1, ssd_inner)
            hpg = ssd_heads // ssd_groups
            ys = []
            for (n_s, s_len, rb0, h0) in ((batch, seq, 0, None),
                                          (n_dec, dec_seq, lat_blk_seq, state_ssd[:, j])):
                xbc = _ssd_conv(proj, ssd_conv_w[j], ssd_conv_b[j], xbc_blk0, n_s, s_len, rb0)
                dta = _ssd_dt(proj, ssd_dt_bias[j], ssd_a_log[j], dt_blk, n_s, s_len, rb0)
                nc = s_len // CHUNK
                dta = dta.reshape(n_s, 2, 2, ssd_groups, hpg, nc, CHUNK).transpose(0, 1, 2, 3, 5, 4, 6)
                y_p, st_p = _ssd_core(xbc, proj, dta, d_cols, ssd_norm[j], h0, n_s, s_len, rb0,
                                      ssd_groups, ssd_heads, ssd_hdim, ssd_nstate)
                ys.append(y_p)
                if h0 is None:
                    ssd_states.append(st_p)
            x = _matmul_residual(ys[0], ys[1], ssd_w_o, j, x, mod_l, 2, rows)
        else:
            proj = _matmul(x, nw1, mod_l, 0, at_w_qkv, j, F32, rows)
            if cos_t is None:
                cos_t, sin_t = _rope_tables(rows, LANES)
            qn, kn = _qk_prep(proj, cos_t, sin_t, at_qn[j], at_kn[j], at_heads, at_kv, rows)
            o_c = _attention(qn, kn, proj, None, None, batch, seq, 0, at_heads, at_kv)
            ck = cache_k[:, j].reshape(n_dec, past, at_kv * LANES)
            cv = cache_v[:, j].reshape(n_dec, past, at_kv * LANES)
            o_l = _attention(qn, kn, proj, ck, cv, n_dec, dec_seq, lat_blk_seq, at_heads, at_kv)
            k_list.append(kn[:ctx_rows].reshape(batch, seq, at_kv, LANES))
            v_list.append(proj[:ctx_rows, (at_heads + at_kv) * LANES:].reshape(batch, seq, at_kv, LANES))
            x = _matmul_residual(o_c, o_l, at_w_o, j, x, mod_l, 2, rows)

        if layer % 2 == 0:
            h2 = _norm_mod(x, norm_w[layer, 1], mod_l, 1, rows)
            act = _matmul_swiglu(h2, ff_w13, layer // 2)
            x = _matmul_residual(act, None, ff_w2, layer // 2, x, mod_l, 5, rows)
        else:
            e = layer // 2
            n_e = moe_router.shape[2]
            tm_e = min(MOE_TILE, rows.total)
            hp, rinfo = _router(x, norm_w[layer, 1], mod_l, moe_router[e], rows)
            src, dest, tile_e, n_used = _route_tables(rinfo, n_e, tm_e)
            ys = _moe_experts(hp, moe_w13[e].astype(BF16), moe_w2[e].astype(BF16), src, tile_e,
                              n_used, tm_e)
            x = _moe_combine(ys, dest, x, rinfo, mod_l, rows)

    y_prompt = x[:ctx_rows].reshape(batch, seq, d)
    y_sample = x[ctx_rows:].reshape(n_dec, dec_seq, d)
    return (y_prompt, y_sample, jnp.stack(hg_states, axis=1), jnp.stack(ssd_states, axis=1),
            jnp.stack(k_list, axis=1), jnp.stack(v_list, axis=1))
```

```python
import functools
import math

import numpy as np
import jax
import jax.numpy as jnp
from jax import lax
from jax.experimental import pallas as pl
from jax.experimental.pallas import tpu as pltpu

F32 = jnp.float32
BF16 = jnp.bfloat16

NORM_EPS = 1e-6
GATE_FLOOR = 1e-30
ROPE_THETA = 10000.0
GRID_W = 64
LANES = 128
SUBLANES = 8
CHUNK = 128
BASE = 16
EXP_CLAMP = 80.0
NEG_BIG = -1e30
VMEM_LIMIT_BYTES = 52 * 1024 * 1024


def _cparams(*sem):
    return pltpu.CompilerParams(dimension_semantics=sem, vmem_limit_bytes=VMEM_LIMIT_BYTES)


def _sigmoid(x):
    return 1.0 / (1.0 + jnp.exp(-x))


def _silu(x):
    return x * _sigmoid(x)


def _pick_tile(n, cap, quantum=LANES):
    best = None
    t = quantum
    while t <= min(n, cap):
        if n % t == 0:
            best = t
        t += quantum
    assert best is not None, (n, cap)
    return best


def _dot(a, b):
    return jnp.dot(a, b, preferred_element_type=F32)


def _dot_nt(a, b):
    return lax.dot_general(a, b, (((1,), (1,)), ((), ())), preferred_element_type=F32)


def _split3(x):
    h1 = x.astype(BF16).astype(F32)
    r1 = x - h1
    h2 = r1.astype(BF16).astype(F32)
    h3 = (r1 - h2).astype(BF16).astype(F32)
    return h1, h2, h3


def _ada_kernel(c_ref, w_ref, b_ref, o_ref):
    s = _silu(c_ref[...]).astype(BF16)
    o_ref[...] = _dot(s, w_ref[...].astype(BF16)) + b_ref[...]


def _ada_mod(cond_rows, ada_w, ada_b):
    nl, d, n6 = ada_w.shape
    tn = _pick_tile(n6, 1024)
    rows = cond_rows.shape[0]
    return pl.pallas_call(
        _ada_kernel,
        out_shape=jax.ShapeDtypeStruct((nl, rows, n6), F32),
        grid=(nl, n6 // tn),
        in_specs=[pl.BlockSpec((rows, d), lambda l, j: (0, 0)),
                  pl.BlockSpec((None, d, tn), lambda l, j: (l, 0, j)),
                  pl.BlockSpec((None, 1, tn), lambda l, j: (l, 0, j))],
        out_specs=pl.BlockSpec((None, rows, tn), lambda l, j: (l, 0, j)),
        compiler_params=_cparams("parallel", "parallel"),
    )(cond_rows, ada_w, ada_b.reshape(nl, 1, n6))


def _modulated_norm(x, nw, mod, which):
    ms = jnp.mean(x * x, axis=-1, keepdims=True)
    y = x * lax.rsqrt(ms + NORM_EPS) * nw
    sh = mod[3 * which:3 * which + 1, :]
    sc = mod[3 * which + 1:3 * which + 2, :]
    return y * (1.0 + sc) + sh


class _Rows:
    def __init__(self, ctx_rows, dec_seq, n_dec):
        self.ctx_rows = ctx_rows
        self.dec_seq = dec_seq
        self.total = ctx_rows + dec_seq * n_dec
        self.tile = math.gcd(ctx_rows, dec_seq)

    def row_tile(self, cap):
        t = self.tile
        while t > cap and t % 2 == 0:
            t //= 2
        return t

    def cond_set(self, row0):
        return jnp.where(row0 < self.ctx_rows, 0, 1 + (row0 - self.ctx_rows) // self.dec_seq)


def _cache_weight(w_ref, wbf_ref):
    @pl.when(pl.program_id(1) == 0)
    def _():
        wbf_ref[...] = w_ref[...].astype(BF16)


def _normed_lhs(x_ref, nw_ref, mod_ref, which):
    return _modulated_norm(x_ref[...], nw_ref[...], mod_ref[...], which).astype(BF16)


def _norm_specs(rows, tm, k):
    return [pl.BlockSpec((tm, k), lambda j, i: (i, 0)),
            pl.BlockSpec((1, k), lambda j, i: (0, 0)),
            pl.BlockSpec((None, SUBLANES, k), lambda j, i: (rows.cond_set(i * tm), 0, 0))]


def _mm_plain_kernel(x_ref, nw_ref, mod_ref, w_ref, o_ref, wbf_ref, *, which):
    _cache_weight(w_ref, wbf_ref)
    h = _normed_lhs(x_ref, nw_ref, mod_ref, which)
    o_ref[...] = _dot(h, wbf_ref[...]).astype(o_ref.dtype)


def _matmul(x, nw, mod_l, which, w, layer, out_dtype, rows, tn_cap=1152):
    m, k = x.shape
    n = w.shape[2]
    tm = rows.row_tile(512)
    tn = _pick_tile(n, tn_cap)
    return pl.pallas_call(
        functools.partial(_mm_plain_kernel, which=which),
        out_shape=jax.ShapeDtypeStruct((m, n), out_dtype),
        grid=(n // tn, m // tm),
        in_specs=_norm_specs(rows, tm, k) + [pl.BlockSpec((None, k, tn), lambda j, i: (layer, 0, j))],
        out_specs=pl.BlockSpec((tm, tn), lambda j, i: (i, j)),
        scratch_shapes=[pltpu.VMEM((k, tn), BF16)],
        compiler_params=_cparams("parallel", "arbitrary"),
    )(x, nw.reshape(1, k), mod_l, w)


def _mm_res_kernel(xa_ref, xb_ref, w_ref, r_ref, mod_ref, o_ref, wbf_ref, *, gate_row, n_a):
    _cache_weight(w_ref, wbf_ref)

    def finish(x_ref):
        acc = _dot(x_ref[...], wbf_ref[...])
        o_ref[...] = r_ref[...] + mod_ref[gate_row:gate_row + 1, :] * acc

    @pl.when(pl.program_id(1) < n_a)
    def _():
        finish(xa_ref)

    @pl.when(pl.program_id(1) >= n_a)
    def _():
        finish(xb_ref)


def _matmul_residual(x_ctx, x_lat, w, layer, res, mod_l, gate_row, rows):
    if x_lat is None:
        x_lat = x_ctx
    k = x_ctx.shape[1]
    m = rows.total
    n = w.shape[2]
    tm = rows.row_tile(512 if k <= 2048 else 256)
    tn = _pick_tile(n, 1024 if k <= 2048 else 512)
    n_a = x_ctx.shape[0] // tm
    return pl.pallas_call(
        functools.partial(_mm_res_kernel, gate_row=gate_row, n_a=n_a),
        out_shape=jax.ShapeDtypeStruct((m, n), F32),
        grid=(n // tn, m // tm),
        in_specs=[pl.BlockSpec((tm, k), lambda j, i: (jnp.minimum(i, n_a - 1), 0)),
                  pl.BlockSpec((tm, k), lambda j, i: (jnp.maximum(i - n_a, 0), 0)),
                  pl.BlockSpec((None, k, tn), lambda j, i: (layer, 0, j)),
                  pl.BlockSpec((tm, tn), lambda j, i: (i, j)),
                  pl.BlockSpec((None, SUBLANES, tn), lambda j, i: (rows.cond_set(i * tm), 0, j))],
        out_specs=pl.BlockSpec((tm, tn), lambda j, i: (i, j)),
        scratch_shapes=[pltpu.VMEM((k, tn), BF16)],
        compiler_params=_cparams("parallel", "arbitrary"),
    )(x_ctx, x_lat, w, res, mod_l)


def _norm_mod_kernel(x_ref, nw_ref, mod_ref, o_ref, *, which):
    o_ref[...] = _normed_lhs(x_ref, nw_ref, mod_ref, which)


def _norm_mod(x, nw, mod_l, which, rows):
    nt, d = x.shape
    tm = rows.row_tile(256)
    return pl.pallas_call(
        functools.partial(_norm_mod_kernel, which=which),
        out_shape=jax.ShapeDtypeStruct((nt, d), BF16),
        grid=(nt // tm,),
        in_specs=[pl.BlockSpec((tm, d), lambda i: (i, 0)),
                  pl.BlockSpec((1, d), lambda i: (0, 0)),
                  pl.BlockSpec((None, SUBLANES, d), lambda i: (rows.cond_set(i * tm), 0, 0))],
        out_specs=pl.BlockSpec((tm, d), lambda i: (i, 0)),
        compiler_params=_cparams("parallel"),
    )(x, nw.reshape(1, d), mod_l)


def _mm_swiglu_kernel(x_ref, wa_ref, wb_ref, o_ref, wabf_ref, wbbf_ref):
    _cache_weight(wa_ref, wabf_ref)
    _cache_weight(wb_ref, wbbf_ref)
    x = x_ref[...]
    a = _dot(x, wabf_ref[...])
    b = _dot(x, wbbf_ref[...])
    o_ref[...] = (_silu(a) * b).astype(o_ref.dtype)


def _matmul_swiglu(x, w13, layer):
    m, k = x.shape
    f = w13.shape[2] // 2
    tm = _pick_tile(m, 512, SUBLANES)
    tn = _pick_tile(f, 512)
    nb = f // tn
    return pl.pallas_call(
        _mm_swiglu_kernel,
        out_shape=jax.ShapeDtypeStruct((m, f), BF16),
        grid=(nb, m // tm),
        in_specs=[pl.BlockSpec((tm, k), lambda j, i: (i, 0)),
                  pl.BlockSpec((None, k, tn), lambda j, i: (layer, 0, j)),
                  pl.BlockSpec((None, k, tn), lambda j, i: (layer, 0, j + nb))],
        out_specs=pl.BlockSpec((tm, tn), lambda j, i: (i, j)),
        scratch_shapes=[pltpu.VMEM((k, tn), BF16), pltpu.VMEM((k, tn), BF16)],
        compiler_params=_cparams("parallel", "arbitrary"),
    )(x, w13, w13)


MOE_TILE = 512
HI16 = 0xFFFF0000


def _router_kernel(x_ref, nw_ref, mod_ref, r_ref, hp_ref, g_ref, *, n_experts):
    h = _modulated_norm(x_ref[...], nw_ref[...], mod_ref[...], 1)
    tm = h.shape[0]
    half = h.shape[1] // 2
    nseg = half // LANES
    bits = lax.bitcast_convert_type(h.astype(BF16).astype(F32), jnp.uint32)
    packed = (bits[:, :half] >> 16) | (bits[:, half:] & jnp.uint32(HI16))
    for j in range(nseg):
        hp_ref[pl.ds(j, tm, stride=nseg), :] = packed[:, j * LANES:(j + 1) * LANES]
    h1, h2, _ = _split3(h)
    r1, r2, _ = _split3(r_ref[...])
    h1, h2, r1, r2 = (t.astype(BF16) for t in (h1, h2, r1, r2))
    logits = _dot(h1, r1) + (_dot(h1, r2) + _dot(h2, r1))
    lane = lax.broadcasted_iota(jnp.int32, logits.shape, 1)
    logits = jnp.where(lane < n_experts, logits, NEG_BIG)
    m1 = jnp.max(logits, axis=-1, keepdims=True)
    i1 = jnp.min(jnp.where(logits == m1, lane, LANES), axis=-1, keepdims=True)
    rest = jnp.where(lane == i1, NEG_BIG, logits)
    m2 = jnp.max(rest, axis=-1, keepdims=True)
    i2 = jnp.min(jnp.where(rest == m2, lane, LANES), axis=-1, keepdims=True)
    e2 = jnp.exp(m2 - m1)
    w1 = 1.0 / (1.0 + e2)
    w2 = e2 * w1
    g_ref[...] = (jnp.where(lane == 0, w1, 0.0) + jnp.where(lane == 1, w2, 0.0)
                  + jnp.where(lane == 2, i1.astype(F32), 0.0) + jnp.where(lane == 3, i2.astype(F32), 0.0))


def _router(x, nw, mod_l, router, rows):
    nt, d = x.shape
    n_experts = router.shape[1]
    tm = rows.row_tile(256)
    nseg = d // 2 // LANES
    rpad = jnp.zeros((d, LANES), F32).at[:, :n_experts].set(router)
    return pl.pallas_call(
        functools.partial(_router_kernel, n_experts=n_experts),
        out_shape=[jax.ShapeDtypeStruct((nt * nseg, LANES), jnp.uint32),
                   jax.ShapeDtypeStruct((nt, LANES), F32)],
        grid=(nt // tm,),
        in_specs=[pl.BlockSpec((tm, d), lambda i: (i, 0)),
                  pl.BlockSpec((1, d), lambda i: (0, 0)),
                  pl.BlockSpec((None, SUBLANES, d), lambda i: (rows.cond_set(i * tm), 0, 0)),
                  pl.BlockSpec((d, LANES), lambda i: (0, 0))],
        out_specs=[pl.BlockSpec((tm * nseg, LANES), lambda i: (i, 0)),
                   pl.BlockSpec((tm, LANES), lambda i: (i, 0))],
        compiler_params=_cparams("parallel"),
    )(x, nw.reshape(1, d), mod_l, rpad)


def _route_tables(rinfo, n_experts, tm):
    nt = rinfo.shape[0]
    e_flat = jnp.concatenate([rinfo[:, 2], rinfo[:, 3]]).astype(jnp.int32)
    onehot = (e_flat[:, None] == jnp.arange(n_experts, dtype=jnp.int32)[None, :]).astype(jnp.int32)
    csum = jnp.cumsum(onehot, axis=0)
    rank = jnp.sum(csum * onehot, axis=1) - 1
    padded = ((csum[-1] + tm - 1) // tm) * tm
    ends = jnp.cumsum(padded)
    dest = jnp.sum(onehot * (ends - padded)[None, :], axis=1) + rank
    p_rows = 2 * nt + n_experts * tm
    tok = jnp.tile(jnp.arange(nt, dtype=jnp.int32), 2)
    src = jnp.zeros((p_rows,), jnp.int32).at[dest].set(tok, unique_indices=True)
    tile_start = jnp.arange(p_rows // tm, dtype=jnp.int32) * tm
    tile_e = jnp.sum((tile_start[:, None] >= ends[None, :]).astype(jnp.int32), axis=1)
    tile_e = jnp.minimum(tile_e, n_experts - 1)
    n_used = (ends[-1] // tm).astype(jnp.int32).reshape(1)
    return src, dest, tile_e, n_used


def _moe_expert_kernel(te_ref, nu_ref, src_ref, hp_ref, w13_ref, w2_ref, ys_ref, xbuf, sem, *, tm):
    del te_ref
    i = pl.program_id(0)
    n_used = nu_ref[0]
    nseg = xbuf.shape[1] // tm
    nout = ys_ref.shape[0] // tm

    def row_copy(tile, slot, r):
        s0 = pl.multiple_of(src_ref[tile * tm + r] * nseg, nseg)
        d0 = pl.multiple_of(r * nseg, nseg)
        return pltpu.make_async_copy(hp_ref.at[pl.ds(s0, nseg)], xbuf.at[slot, pl.ds(d0, nseg)],
                                     sem.at[slot])

    def issue(tile, slot, part, n_parts):
        rows_per = tm // n_parts

        def body(r2, carry):
            r = part * rows_per + 2 * r2
            row_copy(tile, slot, r).start(priority=0)
            row_copy(tile, slot, r + 1).start(priority=1)
            return carry
        lax.fori_loop(0, rows_per // 2, body, 0, unroll=4)

    def prefetch_next(part):
        @pl.when(i + 1 < n_used)
        def _():
            issue(i + 1, (i + 1) % 2, part, 4)

    @pl.when(i == 0)
    def _():
        issue(0, 0, 0, 1)

    @pl.when(i < n_used)
    def _():
        slot = i % 2

        def wait_body(r, carry):
            row_copy(i, slot, r).wait()
            return carry
        lax.fori_loop(0, tm, wait_body, 0, unroll=8)
        prefetch_next(0)
        xb = xbuf.at[slot]
        segs = [xb[pl.ds(j, tm, stride=nseg), :] for j in range(nseg)]
        lo = [lax.bitcast_convert_type(u << 16, F32) for u in segs]
        hi = [lax.bitcast_convert_type(u & jnp.uint32(HI16), F32) for u in segs]
        x = jnp.concatenate(lo + hi, axis=1).astype(BF16)
        f = w2_ref.shape[0]
        a = _dot(x, w13_ref[:, :f])
        prefetch_next(1)
        b = _dot(x, w13_ref[:, f:])
        prefetch_next(2)
        y = _dot((_silu(a) * b).astype(BF16), w2_ref[...])
        prefetch_next(3)
        for j in range(nout):
            ys_ref[pl.ds(j, tm, stride=nout), :] = y[:, j * LANES:(j + 1) * LANES]

    @pl.when(i >= n_used)
    def _():
        ys_ref[...] = jnp.zeros(ys_ref.shape, ys_ref.dtype)


def _moe_experts(hp, w13, w2, src, tile_e, n_used, tm):
    n_e, d, f2 = w13.shape
    f = f2 // 2
    p_rows = src.shape[0]
    nseg = d // 2 // LANES
    nout = d // LANES
    grid_spec = pltpu.PrefetchScalarGridSpec(
        num_scalar_prefetch=3,
        grid=(p_rows // tm,),
        in_specs=[pl.BlockSpec(memory_space=pl.ANY),
                  pl.BlockSpec((None, d, f2), lambda i, te, nu, sr: (te[i], 0, 0),
                               pipeline_mode=pl.Buffered(1)),
                  pl.BlockSpec((None, f, d), lambda i, te, nu, sr: (te[i], 0, 0),
                               pipeline_mode=pl.Buffered(1))],
        out_specs=pl.BlockSpec((tm * nout, LANES), lambda i, te, nu, sr: (i, 0)),
        scratch_shapes=[pltpu.VMEM((2, tm * nseg, LANES), jnp.uint32), pltpu.SemaphoreType.DMA((2,))])
    return pl.pallas_call(
        functools.partial(_moe_expert_kernel, tm=tm),
        out_shape=jax.ShapeDtypeStruct((p_rows * nout, LANES), F32),
        grid_spec=grid_spec,
        compiler_params=_cparams("arbitrary"),
    )(tile_e, n_used, src, hp, w13, w2)


def _moe_combine_kernel(dest_ref, ys_ref, x_ref, g_ref, mod_ref, o_ref, ybuf, sem, *, tm, nt, n_steps):
    i = pl.program_id(0)
    nout = ybuf.shape[2] // tm

    def row_copy(tile, slot, k, r):
        s0 = pl.multiple_of(dest_ref[k * nt + tile * tm + r] * nout, nout)
        d0 = pl.multiple_of(r * nout, nout)
        return pltpu.make_async_copy(ys_ref.at[pl.ds(s0, nout)], ybuf.at[slot, k, pl.ds(d0, nout)],
                                     sem.at[slot])

    def issue(tile, slot):
        def body(r, carry):
            row_copy(tile, slot, 0, r).start(priority=0)
            row_copy(tile, slot, 1, r).start(priority=1)
            return carry
        lax.fori_loop(0, tm, body, 0, unroll=4)

    @pl.when(i == 0)
    def _():
        issue(0, 0)

    @pl.when(i + 1 < n_steps)
    def _():
        issue(i + 1, (i + 1) % 2)

    slot = i % 2

    def wait_body(r, carry):
        row_copy(i, slot, 0, r).wait()
        row_copy(i, slot, 1, r).wait()
        return carry
    lax.fori_loop(0, tm, wait_body, 0, unroll=4)
    g = g_ref[...]
    w0 = g[:, 0:1]
    w1 = g[:, 1:2]
    y0 = ybuf.at[slot, 0]
    y1 = ybuf.at[slot, 1]
    for j in range(nout):
        cols = slice(j * LANES, (j + 1) * LANES)
        mix = w0 * y0[pl.ds(j, tm, stride=nout), :] + w1 * y1[pl.ds(j, tm, stride=nout), :]
        o_ref[:, cols] = x_ref[:, cols] + mod_ref[5:6, cols] * mix


def _moe_combine(ys, dest, x, rinfo, mod_l, rows):
    nt, d = x.shape
    tm = rows.row_tile(256)
    n_steps = nt // tm
    grid_spec = pltpu.PrefetchScalarGridSpec(
        num_scalar_prefetch=1,
        grid=(n_steps,),
        in_specs=[pl.BlockSpec(memory_space=pl.ANY),
                  pl.BlockSpec((tm, d), lambda i, ds: (i, 0)),
                  pl.BlockSpec((tm, LANES), lambda i, ds: (i, 0)),
                  pl.BlockSpec((None, SUBLANES, d), lambda i, ds: (rows.cond_set(i * tm), 0, 0))],
        out_specs=pl.BlockSpec((tm, d), lambda i, ds: (i, 0)),
        scratch_shapes=[pltpu.VMEM((2, 2, tm * (d // LANES), LANES), F32),
                        pltpu.SemaphoreType.DMA((2,))])
    return pl.pallas_call(
        functools.partial(_moe_combine_kernel, tm=tm, nt=nt, n_steps=n_steps),
        out_shape=jax.ShapeDtypeStruct((nt, d), F32),
        grid_spec=grid_spec,
        compiler_params=_cparams("arbitrary"),
    )(dest, ys, x, rinfo, mod_l)


def _tri_pair(n):
    lo = np.tril(np.ones((n, n), np.float32))
    return jnp.asarray(np.stack([lo, lo.T]), BF16)


def _level_halves():
    hs = []
    h = CHUNK // 2
    while h >= BASE:
        hs.append(h)
        h //= 2
    return hs


def _gla_masks():
    t = np.arange(CHUNK)[:, None]
    s = np.arange(CHUNK)[None, :]
    out = []
    for rev in (False, True):
        per = []
        for h in _level_halves():
            same = (t // (2 * h)) == (s // (2 * h))
            t_up = (t % (2 * h)) >= h
            s_up = (s % (2 * h)) >= h
            per.append(same & (~t_up & s_up if rev else t_up & ~s_up))
        same = (t // BASE) == (s // BASE)
        per.append(same & ((s >= t) if rev else (s <= t)))
        out.append(np.stack(per))
    return jnp.asarray(np.stack(out).astype(np.float32))


def _chunk_cumsum(tri, x):
    n = x.shape[1]
    h1 = x.astype(BF16)
    h2 = (x - h1.astype(F32)).astype(BF16)
    p = _dot(tri, jnp.concatenate([h1, h2], axis=1))
    return p[:, n:] + p[:, :n]


def _hgrn_chunks(chains):
    halves = _level_halves()
    c = chains[0]["q"].shape[0]
    for ch in chains:
        z, lb = ch["z"], ch["lb"]
        one_m_lb = 1.0 - lb
        e = jnp.exp(-jnp.abs(z))
        r = 1.0 / (1.0 + e)
        er = e * r
        pos = z >= 0
        f = lb + one_m_lb * jnp.where(pos, r, er)
        ch["logf"] = jnp.log(jnp.maximum(f, GATE_FLOOR))
        ch["key"] = one_m_lb * jnp.where(pos, er, r)
    for ch in chains:
        ch["b"] = _chunk_cumsum(ch["tri"], ch["logf"])
    for ch in chains:
        ch["scores"] = None
    for lvl, h in enumerate(halves):
        for ch in chains:
            b = ch["b"]
            b3 = b.reshape(c // (2 * h), 2 * h, LANES)
            r = h if ch["rev"] else h - 1
            w = jnp.exp(-jnp.abs(b3 - b3[:, r:r + 1, :])).reshape(c, LANES)
            part = _dot_nt((ch["q"] * w).astype(BF16), (ch["key"] * w).astype(BF16)) * ch["masks"][lvl]
            ch["scores"] = part if ch["scores"] is None else ch["scores"] + part
    for ch in chains:
        b3 = ch["b"].reshape(c // BASE, BASE, LANES)
        l3 = ch["logf"].reshape(c // BASE, BASE, LANES)
        r = BASE - 1 if ch["rev"] else 0
        eq = (b3 - (b3[:, r:r + 1, :] - l3[:, r:r + 1, :])).reshape(c, LANES)
        qb = (ch["q"] * jnp.exp(eq)).astype(BF16)
        kb = (ch["key"] * jnp.exp(jnp.minimum(-eq, EXP_CLAMP))).astype(BF16)
        ch["scores"] = ch["scores"] + _dot_nt(qb, kb) * ch["masks"][len(halves)]
    out = []
    for ch in chains:
        b, st, v = ch["b"], ch["st"], ch["v"]
        o = _dot(ch["scores"].astype(BF16), v.astype(BF16)) + \
            _dot_nt((ch["q"] * jnp.exp(b)).astype(BF16), st.astype(BF16))
        r_last = 0 if ch["rev"] else c - 1
        b_last = b[r_last:r_last + 1, :]
        kt = (ch["key"] * jnp.exp(b_last - b)).astype(BF16)
        out.append((o, st * jnp.exp(b_last) + _dot(v.T.astype(BF16), kt)))
    return out


HG_HEADS_PER_STEP = 2


def _hgrn_kernel(*refs, seq_len, use_s0, hps):
    if use_s0:
        (q_ref, zf_ref, zb_ref, v_ref, g_ref, lb_ref, on_ref, tri_ref, msk_ref, s0f_ref, s0b_ref,
         y_ref, st_ref, of_ref, ob_ref) = refs
    else:
        (q_ref, zf_ref, zb_ref, v_ref, g_ref, lb_ref, on_ref, tri_ref, msk_ref,
         y_ref, st_ref, of_ref, ob_ref) = refs
    nc = seq_len // CHUNK
    lanes = [slice(hh * LANES, (hh + 1) * LANES) for hh in range(hps)]

    def rows_of(ci):
        return pl.ds(pl.multiple_of(ci * CHUNK, CHUNK), CHUNK)

    def body(i, sts):
        rf = rows_of(i)
        rb = rows_of(nc - 1 - i)
        chains = []
        for hh, ln in enumerate(lanes):
            chains.append(dict(q=q_ref[rf, ln], z=zf_ref[rf, ln], v=v_ref[rf, ln], lb=lb_ref[:, ln],
                               tri=tri_ref[0], masks=msk_ref.at[0], st=sts[hh], rev=False))
        for hh, ln in enumerate(lanes):
            chains.append(dict(q=q_ref[rb, ln], z=zb_ref[rb, ln], v=v_ref[rb, ln], lb=lb_ref[:, ln],
                               tri=tri_ref[1], masks=msk_ref.at[1], st=sts[hps + hh], rev=True))
        res = _hgrn_chunks(chains)
        for hh, ln in enumerate(lanes):
            of_ref[rf, ln] = res[hh][0]
            ob_ref[rb, ln] = res[hps + hh][0]
        return tuple(st for _, st in res)

    zero = jnp.zeros((LANES, LANES), F32)
    init = tuple((s0f_ref[hh].T if use_s0 else zero) for hh in range(hps)) + \
        tuple((s0b_ref[hh].T if use_s0 else zero) for hh in range(hps))
    sts = lax.fori_loop(0, nc, body, init)
    for hh in range(hps):
        st_ref[0, hh] = sts[hh].T
        st_ref[1, hh] = sts[hps + hh].T

    def finish(ci, carry):
        rows = rows_of(ci)
        for ln in lanes:
            o = of_ref[rows, ln] + ob_ref[rows, ln]
            ms = jnp.mean(o * o, axis=-1, keepdims=True)
            y = o * lax.rsqrt(ms + NORM_EPS) * on_ref[...]
            y_ref[rows, ln] = (y * _silu(g_ref[rows, ln])).astype(y_ref.dtype)
        return carry

    lax.fori_loop(0, nc, finish, 0)


def _hgrn_core(proj, lb_row, onorm, s0, n_seq, seq_len, row_block0, n_heads):
    use_s0 = s0 is not None
    hd = LANES
    hps = HG_HEADS_PER_STEP * (2 if seq_len * 4 <= 1024 else 1)
    hps = math.gcd(hps, n_heads)
    nhb = n_heads // hps
    wd = hps * hd

    def col(off):
        return pl.BlockSpec((seq_len, wd), lambda s, h, off=off: (row_block0 + s, off * nhb + h))

    in_specs = [col(0), col(1), col(2), col(3), col(4),
                pl.BlockSpec((1, wd), lambda s, h: (0, h)),
                pl.BlockSpec((1, hd), lambda s, h: (0, 0)),
                pl.BlockSpec((2, CHUNK, CHUNK), lambda s, h: (0, 0, 0)),
                pl.BlockSpec((2, len(_level_halves()) + 1, CHUNK, CHUNK), lambda s, h: (0, 0, 0, 0))]
    args = [proj] * 5 + [lb_row, onorm.reshape(1, hd), _tri_pair(CHUNK), _gla_masks()]
    if use_s0:
        in_specs += [pl.BlockSpec((None, None, hps, hd, hd), lambda s, h: (s, 0, h, 0, 0)),
                     pl.BlockSpec((None, None, hps, hd, hd), lambda s, h: (s, 1, h, 0, 0))]
        args += [s0, s0]
    y, st = pl.pallas_call(
        functools.partial(_hgrn_kernel, seq_len=seq_len, use_s0=use_s0, hps=hps),
        out_shape=[jax.ShapeDtypeStruct((n_seq * seq_len, n_heads * hd), BF16),
                   jax.ShapeDtypeStruct((n_seq, 2, n_heads, hd, hd), F32)],
        grid=(n_seq, nhb),
        in_specs=in_specs,
        out_specs=[pl.BlockSpec((seq_len, wd), lambda s, h: (s, h)),
                   pl.BlockSpec((None, 2, hps, hd, hd), lambda s, h: (s, 0, h, 0, 0))],
        scratch_shapes=[pltpu.VMEM((seq_len, wd), F32), pltpu.VMEM((seq_len, wd), F32)],
        compiler_params=_cparams("parallel", "parallel"),
    )(*args)
    return y, st


def _ssd_conv_kernel(x_ref, w_ref, b_ref, o_ref, *, seq_len, n_taps):
    x = x_ref[...]
    row = lax.broadcasted_iota(jnp.int32, x.shape, 0)
    half = n_taps // 2
    acc = x * w_ref[half:half + 1, :] + b_ref[...]
    for j in range(n_taps):
        d = j - half
        if d == 0:
            continue
        shifted = pltpu.roll(x, (-d) % seq_len, 0)
        valid = (row + d >= 0) & (row + d < seq_len)
        acc = acc + jnp.where(valid, shifted, 0.0) * w_ref[j:j + 1, :]
    o_ref[...] = _silu(acc)


def _ssd_conv(proj, conv_w, conv_b, col_block0, n_seq, seq_len, row_block0):
    n_taps, c = conv_w.shape
    tc = _pick_tile(math.gcd(c, col_block0 * LANES), max(256, (512 * 1024) // seq_len))
    assert (col_block0 * LANES) % tc == 0
    cb0 = col_block0 * LANES // tc
    return pl.pallas_call(
        functools.partial(_ssd_conv_kernel, seq_len=seq_len, n_taps=n_taps),
        out_shape=jax.ShapeDtypeStruct((n_seq * seq_len, c), F32),
        grid=(n_seq, c // tc),
        in_specs=[pl.BlockSpec((seq_len, tc), lambda s, j: (row_block0 + s, cb0 + j)),
                  pl.BlockSpec((n_taps, tc), lambda s, j: (0, j)),
                  pl.BlockSpec((1, tc), lambda s, j: (0, j))],
        out_specs=pl.BlockSpec((seq_len, tc), lambda s, j: (s, j)),
        compiler_params=_cparams("parallel", "parallel"),
    )(proj, conv_w, conv_b.reshape(1, c))


def _ssd_dt_kernel(d_ref, bias_ref, a_ref, o_ref):
    x = d_ref[...] + bias_ref[...]
    dt = jnp.maximum(x, 0.0) + jnp.log1p(jnp.exp(-jnp.abs(x)))
    o_ref[0] = dt.T
    o_ref[1] = (dt * a_ref[...]).T


def _ssd_dt(proj, dt_bias, a_log, col_block, n_seq, seq_len, row_block0):
    nh2 = dt_bias.size
    assert nh2 == LANES
    a_row = (-jnp.exp(a_log.astype(F32))).reshape(1, nh2)
    return pl.pallas_call(
        _ssd_dt_kernel,
        out_shape=jax.ShapeDtypeStruct((n_seq, 2, nh2, seq_len), F32),
        grid=(n_seq,),
        in_specs=[pl.BlockSpec((seq_len, nh2), lambda s: (row_block0 + s, col_block)),
                  pl.BlockSpec((1, nh2), lambda s: (0, 0)),
                  pl.BlockSpec((1, nh2), lambda s: (0, 0))],
        out_specs=pl.BlockSpec((None, 2, nh2, seq_len), lambda s: (s, 0, 0, 0)),
        compiler_params=_cparams("parallel"),
    )(proj, dt_bias.reshape(1, nh2).astype(F32), a_row)


def _ssd_chunks(chains, hpg, hdim):
    c = chains[0]["xs"].shape[0]
    pair = LANES // hdim
    ti = lax.broadcasted_iota(jnp.int32, (c, c), 0)
    si = lax.broadcasted_iota(jnp.int32, (c, c), 1)
    lane = lax.broadcasted_iota(jnp.int32, (c, LANES), 1)
    sels = [(lane >= u * hdim) & (lane < (u + 1) * hdim) for u in range(pair)]
    pad = jnp.zeros((LANES - 3 * hpg, c), F32)
    for ch in chains:
        dt_t, a_t = ch["dta"][0], ch["dta"][1]
        acs_t = _chunk_cumsum_rows(a_t, ch["tri"])
        r_last = 0 if ch["rev"] else c - 1
        a_last = acs_t[:, r_last:r_last + 1]
        ch["acs_t"] = acs_t
        ch["dt_t"] = dt_t
        ch["cdec"] = jnp.exp(a_last)
        ch["cols"] = jnp.concatenate([acs_t, dt_t * jnp.exp(a_last - acs_t), jnp.exp(acs_t), pad],
                                     axis=0).T
        ch["causal"] = (si >= ti) if ch["rev"] else (si <= ti)
    for ch in chains:
        cmb = ch["cm"].astype(BF16)
        ch["cb"] = _dot_nt(cmb, ch["bm"].astype(BF16))
        ch["y_inter"] = _dot(cmb, ch["h"].astype(BF16))
        ch["bt"] = ch["bm"].T.astype(BF16)
        ch["ys"] = []
        ch["hs"] = []
    for p in range(hpg // pair):
        for ch in chains:
            cols, acs_t, dt_t = ch["cols"], ch["acs_t"], ch["dt_t"]
            ms = []
            dtd_col = e_col = cdec_row = None
            for u in range(pair):
                hh = p * pair + u
                seg = jnp.where(ch["causal"], cols[:, hh:hh + 1] - acs_t[hh:hh + 1, :], NEG_BIG)
                ms.append((ch["cb"] * jnp.exp(seg) * dt_t[hh:hh + 1, :]).astype(BF16))
                dtd = cols[:, hpg + hh:hpg + hh + 1]
                ec = cols[:, 2 * hpg + hh:2 * hpg + hh + 1]
                cd = ch["cdec"][hh:hh + 1, :]
                dtd_col = dtd if dtd_col is None else jnp.where(sels[u], dtd, dtd_col)
                e_col = ec if e_col is None else jnp.where(sels[u], ec, e_col)
                cdec_row = cd if cdec_row is None else jnp.where(sels[u][:1, :], cd, cdec_row)
            x2 = ch["xs"][:, p * LANES:(p + 1) * LANES]
            rhs = [jnp.where(sels[u], x2, 0.0).astype(BF16) for u in range(pair)]
            y_intra = _dot(jnp.concatenate(ms, axis=1), jnp.concatenate(rhs, axis=0))
            ch["ys"].append(y_intra + ch["y_inter"][:, p * LANES:(p + 1) * LANES] * e_col)
            upd = _dot(ch["bt"], (x2 * dtd_col).astype(BF16))
            ch["hs"].append(ch["h"][:, p * LANES:(p + 1) * LANES] * cdec_row + upd)
    return [(jnp.concatenate(ch["ys"], axis=1), jnp.concatenate(ch["hs"], axis=1)) for ch in chains]


def _chunk_cumsum_rows(a_t, tri):
    h = a_t.shape[0]
    parts = jnp.concatenate(_split3(a_t), axis=0).astype(BF16)
    p = _dot(parts, tri)
    return (p[2 * h:] + p[h:2 * h]) + p[:h]


def _ssd_kernel(*refs, seq_len, use_h0, hpg, hdim):
    if use_h0:
        (xs_ref, bm_ref, cm_ref, z_ref, dta_ref, dsk_ref, nw_ref, tri_ref, h0f_ref, h0b_ref,
         y_ref, st_ref, yf_ref, yb_ref) = refs
    else:
        (xs_ref, bm_ref, cm_ref, z_ref, dta_ref, dsk_ref, nw_ref, tri_ref,
         y_ref, st_ref, yf_ref, yb_ref) = refs
    nc = seq_len // CHUNK
    width = hpg * hdim
    n_state = bm_ref.shape[1]

    def rows_of(ci):
        return pl.ds(pl.multiple_of(ci * CHUNK, CHUNK), CHUNK)

    def body(i, carry):
        h_f, h_b = carry
        rf = rows_of(i)
        cb = nc - 1 - i
        rb = rows_of(cb)
        (y_f, h_f), (y_b, h_b) = _ssd_chunks(
            [dict(xs=xs_ref[rf, :], bm=bm_ref[rf, :], cm=cm_ref[rf, :], dta=dta_ref[:, 0, i],
                  tri=tri_ref[1], h=h_f, rev=False),
             dict(xs=xs_ref[rb, :], bm=bm_ref[rb, :], cm=cm_ref[rb, :], dta=dta_ref[:, 1, cb],
                  tri=tri_ref[0], h=h_b, rev=True)], hpg, hdim)
        yf_ref[rf, :] = y_f
        yb_ref[rb, :] = y_b
        return h_f, h_b

    def load_state(ref):
        return ref[...].reshape(width, n_state).T

    zero = jnp.zeros((n_state, width), F32)
    init = (load_state(h0f_ref), load_state(h0b_ref)) if use_h0 else (zero, zero)
    h_f, h_b = lax.fori_loop(0, nc, body, init)
    st_ref[0] = h_f.T.reshape(hpg, hdim, n_state)
    st_ref[1] = h_b.T.reshape(hpg, hdim, n_state)

    def finish(ci, carry):
        rows = rows_of(ci)
        y = yf_ref[rows, :] + yb_ref[rows, :] + dsk_ref[...] * xs_ref[rows, :]
        y = y * _silu(z_ref[rows, :])
        ms = jnp.mean(y * y, axis=-1, keepdims=True)
        y_ref[rows, :] = (y * lax.rsqrt(ms + NORM_EPS) * nw_ref[...]).astype(y_ref.dtype)
        return carry

    lax.fori_loop(0, nc, finish, 0)


def _ssd_core(xbc, proj, dta, d_cols, norm_w, h0, n_seq, seq_len, row_block0, n_groups, n_heads,
              hdim, n_state):
    use_h0 = h0 is not None
    hpg = n_heads // n_groups
    width = hpg * hdim
    inner = n_heads * hdim
    assert n_state == LANES and width % LANES == 0 and inner % width == 0
    nc = seq_len // CHUNK
    b_blk0 = inner // n_state
    c_blk0 = b_blk0 + n_groups
    in_specs = [pl.BlockSpec((seq_len, width), lambda s, g: (s, g)),
                pl.BlockSpec((seq_len, n_state), lambda s, g: (s, b_blk0 + g)),
                pl.BlockSpec((seq_len, n_state), lambda s, g: (s, c_blk0 + g)),
                pl.BlockSpec((seq_len, width), lambda s, g: (row_block0 + s, g)),
                pl.BlockSpec((None, 2, 2, None, nc, hpg, CHUNK), lambda s, g: (s, 0, 0, g, 0, 0, 0)),
                pl.BlockSpec((1, width), lambda s, g: (0, g)),
                pl.BlockSpec((1, width), lambda s, g: (0, g)),
                pl.BlockSpec((2, CHUNK, CHUNK), lambda s, g: (0, 0, 0))]
    args = [xbc, xbc, xbc, proj, dta, d_cols, norm_w.reshape(1, inner), _tri_pair(CHUNK)]
    if use_h0:
        st_in = (None, None, hpg, hdim, n_state)
        in_specs += [pl.BlockSpec(st_in, lambda s, g: (s, 0, g, 0, 0)),
                     pl.BlockSpec(st_in, lambda s, g: (s, 1, g, 0, 0))]
        args += [h0, h0]
    y, st = pl.pallas_call(
        functools.partial(_ssd_kernel, seq_len=seq_len, use_h0=use_h0, hpg=hpg, hdim=hdim),
        out_shape=[jax.ShapeDtypeStruct((n_seq * seq_len, inner), BF16),
                   jax.ShapeDtypeStruct((n_seq, 2, n_heads, hdim, n_state), F32)],
        grid=(n_seq, n_groups),
        in_specs=in_specs,
        out_specs=[pl.BlockSpec((seq_len, width), lambda s, g: (s, g)),
                   pl.BlockSpec((None, 2, hpg, hdim, n_state), lambda s, g: (s, 0, g, 0, 0))],
        scratch_shapes=[pltpu.VMEM((seq_len, width), F32), pltpu.VMEM((seq_len, width), F32)],
        compiler_params=_cparams("parallel", "parallel"),
    )(*args)
    return y, st


def _rope_tables(rows, head_dim):
    quarter = head_dim // 4
    inv = ROPE_THETA ** (-np.arange(quarter, dtype=np.float64) / quarter)
    t = np.arange(rows.dec_seq)
    ang_r = (t // GRID_W)[:, None] * inv[None, :]
    ang_c = (t % GRID_W)[:, None] * inv[None, :]
    cos = np.concatenate([np.cos(ang_r)] * 2 + [np.cos(ang_c)] * 2, axis=1)
    sin = np.concatenate([-np.sin(ang_r), np.sin(ang_r), -np.sin(ang_c), np.sin(ang_c)], axis=1)
    n_dec = (rows.total - rows.ctx_rows) // rows.dec_seq
    cos = np.concatenate([np.ones((rows.ctx_rows, head_dim))] + [cos] * n_dec, axis=0)
    sin = np.concatenate([np.zeros((rows.ctx_rows, head_dim))] + [sin] * n_dec, axis=0)
    return jnp.asarray(cos, F32), jnp.asarray(sin, F32)


def _qk_prep_kernel(p_ref, cos_ref, sin_ref, qn_ref, kn_ref, q_ref, k_ref, *, n_q, n_kv, scale):
    cos = cos_ref[...]
    sin = sin_ref[...]
    lane = lax.broadcasted_iota(jnp.int32, cos.shape, 1)
    first = (lane % (LANES // 2)) < (LANES // 4)

    def norm_rope(x, w):
        ms = jnp.mean(x * x, axis=-1, keepdims=True)
        y = x * lax.rsqrt(ms + NORM_EPS) * w
        partner = jnp.where(first, pltpu.roll(y, LANES - LANES // 4, 1), pltpu.roll(y, LANES // 4, 1))
        return y * cos + partner * sin

    for h in range(n_q):
        x = p_ref[:, h * LANES:(h + 1) * LANES]
        q_ref[:, h * LANES:(h + 1) * LANES] = (norm_rope(x, qn_ref[...]) * scale).astype(q_ref.dtype)
    for h in range(n_kv):
        x = p_ref[:, (n_q + h) * LANES:(n_q + h + 1) * LANES]
        k_ref[:, h * LANES:(h + 1) * LANES] = norm_rope(x, kn_ref[...])


def _qk_prep(proj, cos, sin, qn, kn, n_q, n_kv, rows):
    nt = proj.shape[0]
    hd = LANES
    tm = rows.row_tile(256)
    scale = float(hd) ** -0.5
    return pl.pallas_call(
        functools.partial(_qk_prep_kernel, n_q=n_q, n_kv=n_kv, scale=scale),
        out_shape=[jax.ShapeDtypeStruct((nt, n_q * hd), BF16),
                   jax.ShapeDtypeStruct((nt, n_kv * hd), F32)],
        grid=(nt // tm,),
        in_specs=[pl.BlockSpec((tm, proj.shape[1]), lambda i: (i, 0)),
                  pl.BlockSpec((tm, hd), lambda i: (i, 0)),
                  pl.BlockSpec((tm, hd), lambda i: (i, 0)),
                  pl.BlockSpec((1, hd), lambda i: (0, 0)),
                  pl.BlockSpec((1, hd), lambda i: (0, 0))],
        out_specs=[pl.BlockSpec((tm, n_q * hd), lambda i: (i, 0)),
                   pl.BlockSpec((tm, n_kv * hd), lambda i: (i, 0))],
        compiler_params=_cparams("parallel"),
    )(proj, cos, sin, qn.reshape(1, hd), kn.reshape(1, hd))


ATTN_KEY_CHUNK = 512


def _attn_kernel(*refs, grp, use_ctx):
    if use_ctx:
        q_ref, k_ref, v_ref, kc_ref, vc_ref, o_ref = refs
    else:
        q_ref, k_ref, v_ref, o_ref = refs
    tq = q_ref.shape[0]
    q = jnp.concatenate([q_ref[:, g * LANES:(g + 1) * LANES] for g in range(grp)], axis=0)
    chunks = [(kc_ref, vc_ref, 0, kc_ref.shape[0])] if use_ctx else []
    n_keys = k_ref.shape[0]
    kc = min(ATTN_KEY_CHUNK, n_keys)
    chunks += [(k_ref, v_ref, c0, kc) for c0 in range(0, n_keys, kc)]

    def scores(chunk):
        kr, _, c0, n = chunk
        return _dot_nt(q, kr[c0:c0 + n, :].astype(BF16))

    s_next = scores(chunks[0])
    m = l = acc = None
    for idx, (_, vr, c0, n) in enumerate(chunks):
        s = s_next
        if idx + 1 < len(chunks):
            s_next = scores(chunks[idx + 1])
        mj = jnp.max(s, axis=-1, keepdims=True)
        m_new = mj if m is None else jnp.maximum(m, mj)
        p = jnp.exp(s - m_new)
        pv = _dot(p.astype(BF16), vr[c0:c0 + n, :].astype(BF16))
        pl_sum = jnp.sum(p, axis=-1, keepdims=True)
        if m is None:
            l, acc = pl_sum, pv
        else:
            alpha = jnp.exp(m - m_new)
            l = alpha * l + pl_sum
            acc = alpha * acc + pv
        m = m_new
    o = acc / l
    for g in range(grp):
        o_ref[:, g * LANES:(g + 1) * LANES] = o[g * tq:(g + 1) * tq].astype(o_ref.dtype)


def _attention(qn, kn, proj, ctx_k, ctx_v, n_seq, seq_len, row_block0, n_q, n_kv):
    use_ctx = ctx_k is not None
    grp = n_q // n_kv
    hd = LANES
    tq = min(seq_len, 128)
    nqb = seq_len // tq
    v_blk0 = n_q + n_kv
    in_specs = [pl.BlockSpec((tq, grp * hd), lambda s, kv, i: ((row_block0 + s) * nqb + i, kv)),
                pl.BlockSpec((seq_len, hd), lambda s, kv, i: (row_block0 + s, kv)),
                pl.BlockSpec((seq_len, hd), lambda s, kv, i: (row_block0 + s, v_blk0 + kv))]
    args = [qn, kn, proj]
    if use_ctx:
        past = ctx_k.shape[1]
        in_specs += [pl.BlockSpec((None, past, hd), lambda s, kv, i: (s, 0, kv)),
                     pl.BlockSpec((None, past, hd), lambda s, kv, i: (s, 0, kv))]
        args += [ctx_k, ctx_v]
    return pl.pallas_call(
        functools.partial(_attn_kernel, grp=grp, use_ctx=use_ctx),
        out_shape=jax.ShapeDtypeStruct((n_seq * seq_len, n_q * hd), BF16),
        grid=(n_seq, n_kv, nqb),
        in_specs=in_specs,
        out_specs=pl.BlockSpec((tq, grp * hd), lambda s, kv, i: (s * nqb + i, kv)),
        compiler_params=_cparams("parallel", "parallel", "arbitrary"),
    )(*args)


def kernel(x_prompt, x_sample, state_hgrn, state_ssd, cache_k, cache_v, c, c_ctx, ada_w, ada_b, norm_w,
           hg_w_in, hg_lb_logits, hg_onorm, hg_w_o, ssd_w_in, ssd_conv_w, ssd_conv_b, ssd_a_log,
           ssd_dt_bias, ssd_d, ssd_norm, ssd_w_o, at_w_qkv, at_qn, at_kn, at_w_o, ff_w13, ff_w2,
           moe_router, moe_w13, moe_w2):
    batch, seq, d = x_prompt.shape
    n_dec, dec_seq, _ = x_sample.shape
    depth = ada_w.shape[0]
    rows = _Rows(batch * seq, dec_seq, n_dec)
    ctx_rows = rows.ctx_rows
    assert seq % CHUNK == 0 and dec_seq % CHUNK == 0 and dec_seq % seq == 0 and ctx_rows % dec_seq == 0

    hg_heads = d // LANES
    ssd_heads = ssd_a_log.shape[2]
    ssd_inner = ssd_w_o.shape[1]
    ssd_hdim = ssd_inner // ssd_heads
    ssd_nstate = state_ssd.shape[-1]
    ssd_groups = (ssd_conv_w.shape[2] - ssd_inner) // (2 * ssd_nstate)
    at_kv = cache_k.shape[3]
    at_heads = at_w_o.shape[1] // LANES
    past = cache_k.shape[2]

    x = jnp.concatenate([x_prompt.reshape(ctx_rows, d), x_sample.reshape(n_dec * dec_seq, d)], axis=0)

    n_sets = 1 + n_dec
    cond = jnp.concatenate([c_ctx[None, :], c], axis=0)
    cond = jnp.pad(cond, ((0, (-n_sets) % SUBLANES), (0, 0)))
    mod = _ada_mod(cond, ada_w, ada_b)[:, :n_sets].reshape(depth, n_sets, 6, d)
    mod = jnp.pad(mod, ((0, 0), (0, 0), (0, SUBLANES - 6), (0, 0)))

    lb_all = None
    cos_t = sin_t = None
    hg_states, ssd_states, k_list, v_list = [], [], [], []
    lat_blk_seq = ctx_rows // dec_seq

    for layer in range(depth):
        j = layer // 3
        kind = layer % 3
        mod_l = mod[layer]
        nw1 = norm_w[layer, 0]
        if kind == 0:
            if lb_all is None:
                pr = jax.nn.softmax(hg_lb_logits.astype(F32), axis=0)
                lb_all = jnp.cumsum(pr, axis=0) - pr[0]
            proj = _matmul(x, nw1, mod_l, 0, hg_w_in, j, F32, rows)
            lb_row = lb_all[j].reshape(1, -1)
            y_c, st_c = _hgrn_core(proj, lb_row, hg_onorm[j], None, batch, seq, 0, hg_heads)
            y_l, _ = _hgrn_core(proj, lb_row, hg_onorm[j], state_hgrn[:, j], n_dec, dec_seq,
                                lat_blk_seq, hg_heads)
            hg_states.append(st_c)
            x = _matmul_residual(y_c, y_l, hg_w_o, j, x, mod_l, 2, rows)
        elif kind == 1:
            proj = _matmul(x, nw1, mod_l, 0, ssd_w_in, j, F32, rows)
            gn2 = 2 * ssd_groups * ssd_nstate
            xbc_blk0 = ssd_inner // LANES
            dt_blk = (2 * ssd_inner + gn2) // LANES
            d_cols = jnp.repeat((ssd_d[j, 0] + ssd_d[j, 1]).astype(F32), ssd_hdim).reshape(1, ssd_inner)
            hpg = ssd_heads // ssd_groups
            ys = []
            for (n_s, s_len, rb0, h0) in ((batch, seq, 0, None),
                                          (n_dec, dec_seq, lat_blk_seq, state_ssd[:, j])):
                xbc = _ssd_conv(proj, ssd_conv_w[j], ssd_conv_b[j], xbc_blk0, n_s, s_len, rb0)
                dta = _ssd_dt(proj, ssd_dt_bias[j], ssd_a_log[j], dt_blk, n_s, s_len, rb0)
                nc = s_len // CHUNK
                dta = dta.reshape(n_s, 2, 2, ssd_groups, hpg, nc, CHUNK).transpose(0, 1, 2, 3, 5, 4, 6)
                y_p, st_p = _ssd_core(xbc, proj, dta, d_cols, ssd_norm[j], h0, n_s, s_len, rb0,
                                      ssd_groups, ssd_heads, ssd_hdim, ssd_nstate)
                ys.append(y_p)
                if h0 is None:
                    ssd_states.append(st_p)
            x = _matmul_residual(ys[0], ys[1], ssd_w_o, j, x, mod_l, 2, rows)
        else:
            proj = _matmul(x, nw1, mod_l, 0, at_w_qkv, j, F32, rows)
            if cos_t is None:
                cos_t, sin_t = _rope_tables(rows, LANES)
            qn, kn = _qk_prep(proj, cos_t, sin_t, at_qn[j], at_kn[j], at_heads, at_kv, rows)
            o_c = _attention(qn, kn, proj, None, None, batch, seq, 0, at_heads, at_kv)
            ck = cache_k[:, j].reshape(n_dec, past, at_kv * LANES)
            cv = cache_v[:, j].reshape(n_dec, past, at_kv * LANES)
            o_l = _attention(qn, kn, proj, ck, cv, n_dec, dec_seq, lat_blk_seq, at_heads, at_kv)
            k_list.append(kn[:ctx_rows].reshape(batch, seq, at_kv, LANES))
            v_list.append(proj[:ctx_rows, (at_heads + at_kv) * LANES:].reshape(batch, seq, at_kv, LANES))
            x = _matmul_residual(o_c, o_l, at_w_o, j, x, mod_l, 2, rows)

        if layer % 2 == 0:
            h2 = _norm_mod(x, norm_w[layer, 1], mod_l, 1, rows)
            act = _matmul_swiglu(h2, ff_w13, layer // 2)
            x = _matmul_residual(act, None, ff_w2, layer // 2, x, mod_l, 5, rows)
        else:
            e = layer // 2
            n_e = moe_router.shape[2]
            tm_e = min(MOE_TILE, rows.total)
            hp, rinfo = _router(x, norm_w[layer, 1], mod_l, moe_router[e], rows)
            src, dest, tile_e, n_used = _route_tables(rinfo, n_e, tm_e)
            ys = _moe_experts(hp, moe_w13[e].astype(BF16), moe_w2[e].astype(BF16), src, tile_e,
                              n_used, tm_e)
            x = _moe_combine(ys, dest, x, rinfo, mod_l, rows)

    y_prompt = x[:ctx_rows].reshape(batch, seq, d)
    y_sample = x[ctx_rows:].reshape(n_dec, dec_seq, d)
    return (y_prompt, y_sample, jnp.stack(hg_states, axis=1), jnp.stack(ssd_states, axis=1),
            jnp.stack(k_list, axis=1), jnp.stack(v_list, axis=1))
```

```python
import functools
import math

import numpy as np
import jax
import jax.numpy as jnp
from jax import lax
from jax.experimental import pallas as pl
from jax.experimental.pallas import tpu as pltpu

F32 = jnp.float32
BF16 = jnp.bfloat16

NORM_EPS = 1e-6
GATE_FLOOR = 1e-30
ROPE_THETA = 10000.0
GRID_W = 64
LANES = 128
SUBLANES = 8
CHUNK = 128
BASE = 16
EXP_CLAMP = 80.0
NEG_BIG = -1e30
VMEM_LIMIT_BYTES = 52 * 1024 * 1024


def _cparams(*sem):
    return pltpu.CompilerParams(dimension_semantics=sem, vmem_limit_bytes=VMEM_LIMIT_BYTES)


def _sigmoid(x):
    return 1.0 / (1.0 + jnp.exp(-x))


def _silu(x):
    return x * _sigmoid(x)


def _pick_tile(n, cap, quantum=LANES):
    best = None
    t = quantum
    while t <= min(n, cap):
        if n % t == 0:
            best = t
        t += quantum
    assert best is not None, (n, cap)
    return best


def _dot(a, b):
    return jnp.dot(a, b, preferred_element_type=F32)


def _dot_nt(a, b):
    return lax.dot_general(a, b, (((1,), (1,)), ((), ())), preferred_element_type=F32)


def _split3(x):
    h1 = x.astype(BF16).astype(F32)
    r1 = x - h1
    h2 = r1.astype(BF16).astype(F32)
    h3 = (r1 - h2).astype(BF16).astype(F32)
    return h1, h2, h3


def _ada_kernel(c_ref, w_ref, b_ref, o_ref):
    s = _silu(c_ref[...]).astype(BF16)
    o_ref[...] = _dot(s, w_ref[...].astype(BF16)) + b_ref[...]


def _ada_mod(cond_rows, ada_w, ada_b):
    nl, d, n6 = ada_w.shape
    tn = _pick_tile(n6, 1024)
    rows = cond_rows.shape[0]
    return pl.pallas_call(
        _ada_kernel,
        out_shape=jax.ShapeDtypeStruct((nl, rows, n6), F32),
        grid=(nl, n6 // tn),
        in_specs=[pl.BlockSpec((rows, d), lambda l, j: (0, 0)),
                  pl.BlockSpec((None, d, tn), lambda l, j: (l, 0, j)),
                  pl.BlockSpec((None, 1, tn), lambda l, j: (l, 0, j))],
        out_specs=pl.BlockSpec((None, rows, tn), lambda l, j: (l, 0, j)),
        compiler_params=_cparams("parallel", "parallel"),
    )(cond_rows, ada_w, ada_b.reshape(nl, 1, n6))


def _modulated_norm(x, nw, mod, which):
    ms = jnp.mean(x * x, axis=-1, keepdims=True)
    y = x * lax.rsqrt(ms + NORM_EPS) * nw
    sh = mod[3 * which:3 * which + 1, :]
    sc = mod[3 * which + 1:3 * which + 2, :]
    return y * (1.0 + sc) + sh


class _Rows:
    def __init__(self, ctx_rows, dec_seq, n_dec):
        self.ctx_rows = ctx_rows
        self.dec_seq = dec_seq
        self.total = ctx_rows + dec_seq * n_dec
        self.tile = math.gcd(ctx_rows, dec_seq)

    def row_tile(self, cap):
        t = self.tile
        while t > cap and t % 2 == 0:
            t //= 2
        return t

    def cond_set(self, row0):
        return jnp.where(row0 < self.ctx_rows, 0, 1 + (row0 - self.ctx_rows) // self.dec_seq)


def _cache_weight(w_ref, wbf_ref):
    @pl.when(pl.program_id(1) == 0)
    def _():
        wbf_ref[...] = w_ref[...].astype(BF16)


def _normed_lhs(x_ref, nw_ref, mod_ref, which):
    return _modulated_norm(x_ref[...], nw_ref[...], mod_ref[...], which).astype(BF16)


def _norm_specs(rows, tm, k):
    return [pl.BlockSpec((tm, k), lambda j, i: (i, 0)),
            pl.BlockSpec((1, k), lambda j, i: (0, 0)),
            pl.BlockSpec((None, SUBLANES, k), lambda j, i: (rows.cond_set(i * tm), 0, 0))]


def _mm_plain_kernel(x_ref, nw_ref, mod_ref, w_ref, o_ref, wbf_ref, *, which):
    _cache_weight(w_ref, wbf_ref)
    h = _normed_lhs(x_ref, nw_ref, mod_ref, which)
    o_ref[...] = _dot(h, wbf_ref[...]).astype(o_ref.dtype)


def _matmul(x, nw, mod_l, which, w, layer, out_dtype, rows, tn_cap=1152):
    m, k = x.shape
    n = w.shape[2]
    tm = rows.row_tile(512)
    tn = _pick_tile(n, tn_cap)
    return pl.pallas_call(
        functools.partial(_mm_plain_kernel, which=which),
        out_shape=jax.ShapeDtypeStruct((m, n), out_dtype),
        grid=(n // tn, m // tm),
        in_specs=_norm_specs(rows, tm, k) + [pl.BlockSpec((None, k, tn), lambda j, i: (layer, 0, j))],
        out_specs=pl.BlockSpec((tm, tn), lambda j, i: (i, j)),
        scratch_shapes=[pltpu.VMEM((k, tn), BF16)],
        compiler_params=_cparams("parallel", "arbitrary"),
    )(x, nw.reshape(1, k), mod_l, w)


def _mm_res_kernel(xa_ref, xb_ref, w_ref, r_ref, mod_ref, o_ref, wbf_ref, *, gate_row, n_a):
    _cache_weight(w_ref, wbf_ref)

    def finish(x_ref):
        acc = _dot(x_ref[...], wbf_ref[...])
        o_ref[...] = r_ref[...] + mod_ref[gate_row:gate_row + 1, :] * acc

    @pl.when(pl.program_id(1) < n_a)
    def _():
        finish(xa_ref)

    @pl.when(pl.program_id(1) >= n_a)
    def _():
        finish(xb_ref)


def _matmul_residual(x_ctx, x_lat, w, layer, res, mod_l, gate_row, rows):
    if x_lat is None:
        x_lat = x_ctx
    k = x_ctx.shape[1]
    m = rows.total
    n = w.shape[2]
    tm = rows.row_tile(512 if k <= 2048 else 256)
    tn = _pick_tile(n, 1024 if k <= 2048 else 512)
    n_a = x_ctx.shape[0] // tm
    return pl.pallas_call(
        functools.partial(_mm_res_kernel, gate_row=gate_row, n_a=n_a),
        out_shape=jax.ShapeDtypeStruct((m, n), F32),
        grid=(n // tn, m // tm),
        in_specs=[pl.BlockSpec((tm, k), lambda j, i: (jnp.minimum(i, n_a - 1), 0)),
                  pl.BlockSpec((tm, k), lambda j, i: (jnp.maximum(i - n_a, 0), 0)),
                  pl.BlockSpec((None, k, tn), lambda j, i: (layer, 0, j)),
                  pl.BlockSpec((tm, tn), lambda j, i: (i, j)),
                  pl.BlockSpec((None, SUBLANES, tn), lambda j, i: (rows.cond_set(i * tm), 0, j))],
        out_specs=pl.BlockSpec((tm, tn), lambda j, i: (i, j)),
        scratch_shapes=[pltpu.VMEM((k, tn), BF16)],
        compiler_params=_cparams("parallel", "arbitrary"),
    )(x_ctx, x_lat, w, res, mod_l)


def _norm_mod_kernel(x_ref, nw_ref, mod_ref, o_ref, *, which):
    o_ref[...] = _normed_lhs(x_ref, nw_ref, mod_ref, which)


def _norm_mod(x, nw, mod_l, which, rows):
    nt, d = x.shape
    tm = rows.row_tile(256)
    return pl.pallas_call(
        functools.partial(_norm_mod_kernel, which=which),
        out_shape=jax.ShapeDtypeStruct((nt, d), BF16),
        grid=(nt // tm,),
        in_specs=[pl.BlockSpec((tm, d), lambda i: (i, 0)),
                  pl.BlockSpec((1, d), lambda i: (0, 0)),
                  pl.BlockSpec((None, SUBLANES, d), lambda i: (rows.cond_set(i * tm), 0, 0))],
        out_specs=pl.BlockSpec((tm, d), lambda i: (i, 0)),
        compiler_params=_cparams("parallel"),
    )(x, nw.reshape(1, d), mod_l)


def _mm_swiglu_kernel(x_ref, wa_ref, wb_ref, o_ref, wabf_ref, wbbf_ref):
    _cache_weight(wa_ref, wabf_ref)
    _cache_weight(wb_ref, wbbf_ref)
    x = x_ref[...]
    a = _dot(x, wabf_ref[...])
    b = _dot(x, wbbf_ref[...])
    o_ref[...] = (_silu(a) * b).astype(o_ref.dtype)


def _matmul_swiglu(x, w13, layer):
    m, k = x.shape
    f = w13.shape[2] // 2
    tm = _pick_tile(m, 512, SUBLANES)
    tn = _pick_tile(f, 512)
    nb = f // tn
    return pl.pallas_call(
        _mm_swiglu_kernel,
        out_shape=jax.ShapeDtypeStruct((m, f), BF16),
        grid=(nb, m // tm),
        in_specs=[pl.BlockSpec((tm, k), lambda j, i: (i, 0)),
                  pl.BlockSpec((None, k, tn), lambda j, i: (layer, 0, j)),
                  pl.BlockSpec((None, k, tn), lambda j, i: (layer, 0, j + nb))],
        out_specs=pl.BlockSpec((tm, tn), lambda j, i: (i, j)),
        scratch_shapes=[pltpu.VMEM((k, tn), BF16), pltpu.VMEM((k, tn), BF16)],
        compiler_params=_cparams("parallel", "arbitrary"),
    )(x, w13, w13)


MOE_TILE = 512
HI16 = 0xFFFF0000


def _router_kernel(x_ref, nw_ref, mod_ref, r_ref, hp_ref, g_ref, *, n_experts):
    h = _modulated_norm(x_ref[...], nw_ref[...], mod_ref[...], 1)
    tm = h.shape[0]
    half = h.shape[1] // 2
    nseg = half // LANES
    bits = lax.bitcast_convert_type(h.astype(BF16).astype(F32), jnp.uint32)
    packed = (bits[:, :half] >> 16) | (bits[:, half:] & jnp.uint32(HI16))
    for j in range(nseg):
        hp_ref[pl.ds(j, tm, stride=nseg), :] = packed[:, j * LANES:(j + 1) * LANES]
    h1, h2, _ = _split3(h)
    r1, r2, _ = _split3(r_ref[...])
    h1, h2, r1, r2 = (t.astype(BF16) for t in (h1, h2, r1, r2))
    logits = _dot(h1, r1) + (_dot(h1, r2) + _dot(h2, r1))
    lane = lax.broadcasted_iota(jnp.int32, logits.shape, 1)
    logits = jnp.where(lane < n_experts, logits, NEG_BIG)
    m1 = jnp.max(logits, axis=-1, keepdims=True)
    i1 = jnp.min(jnp.where(logits == m1, lane, LANES), axis=-1, keepdims=True)
    rest = jnp.where(lane == i1, NEG_BIG, logits)
    m2 = jnp.max(rest, axis=-1, keepdims=True)
    i2 = jnp.min(jnp.where(rest == m2, lane, LANES), axis=-1, keepdims=True)
    e2 = jnp.exp(m2 - m1)
    w1 = 1.0 / (1.0 + e2)
    w2 = e2 * w1
    g_ref[...] = (jnp.where(lane == 0, w1, 0.0) + jnp.where(lane == 1, w2, 0.0)
                  + jnp.where(lane == 2, i1.astype(F32), 0.0) + jnp.where(lane == 3, i2.astype(F32), 0.0))


def _router(x, nw, mod_l, router, rows):
    nt, d = x.shape
    n_experts = router.shape[1]
    tm = rows.row_tile(256)
    nseg = d // 2 // LANES
    rpad = jnp.zeros((d, LANES), F32).at[:, :n_experts].set(router)
    return pl.pallas_call(
        functools.partial(_router_kernel, n_experts=n_experts),
        out_shape=[jax.ShapeDtypeStruct((nt * nseg, LANES), jnp.uint32),
                   jax.ShapeDtypeStruct((nt, LANES), F32)],
        grid=(nt // tm,),
        in_specs=[pl.BlockSpec((tm, d), lambda i: (i, 0)),
                  pl.BlockSpec((1, d), lambda i: (0, 0)),
                  pl.BlockSpec((None, SUBLANES, d), lambda i: (rows.cond_set(i * tm), 0, 0)),
                  pl.BlockSpec((d, LANES), lambda i: (0, 0))],
        out_specs=[pl.BlockSpec((tm * nseg, LANES), lambda i: (i, 0)),
                   pl.BlockSpec((tm, LANES), lambda i: (i, 0))],
        compiler_params=_cparams("parallel"),
    )(x, nw.reshape(1, d), mod_l, rpad)


def _route_tables(rinfo, n_experts, tm):
    nt = rinfo.shape[0]
    e_flat = jnp.concatenate([rinfo[:, 2], rinfo[:, 3]]).astype(jnp.int32)
    onehot = (e_flat[:, None] == jnp.arange(n_experts, dtype=jnp.int32)[None, :]).astype(jnp.int32)
    csum = jnp.cumsum(onehot, axis=0)
    rank = jnp.sum(csum * onehot, axis=1) - 1
    padded = ((csum[-1] + tm - 1) // tm) * tm
    ends = jnp.cumsum(padded)
    dest = jnp.sum(onehot * (ends - padded)[None, :], axis=1) + rank
    p_rows = 2 * nt + n_experts * tm
    tok = jnp.tile(jnp.arange(nt, dtype=jnp.int32), 2)
    src = jnp.zeros((p_rows,), jnp.int32).at[dest].set(tok, unique_indices=True)
    tile_start = jnp.arange(p_rows // tm, dtype=jnp.int32) * tm
    tile_e = jnp.sum((tile_start[:, None] >= ends[None, :]).astype(jnp.int32), axis=1)
    tile_e = jnp.minimum(tile_e, n_experts - 1)
    n_used = (ends[-1] // tm).astype(jnp.int32).reshape(1)
    return src, dest, tile_e, n_used


def _moe_expert_kernel(te_ref, nu_ref, src_ref, hp_ref, w13_ref, w2_ref, ys_ref, xbuf, sem, *, tm):
    del te_ref
    i = pl.program_id(0)
    n_used = nu_ref[0]
    nseg = xbuf.shape[1] // tm
    nout = ys_ref.shape[0] // tm

    def row_copy(tile, slot, r):
        s0 = pl.multiple_of(src_ref[tile * tm + r] * nseg, nseg)
        d0 = pl.multiple_of(r * nseg, nseg)
        return pltpu.make_async_copy(hp_ref.at[pl.ds(s0, nseg)], xbuf.at[slot, pl.ds(d0, nseg)],
                                     sem.at[slot])

    def issue(tile, slot):
        def body(r, carry):
            row_copy(tile, slot, r).start()
            return carry
        lax.fori_loop(0, tm, body, 0, unroll=8)

    @pl.when(i == 0)
    def _():
        issue(0, 0)

    @pl.when(i + 1 < n_used)
    def _():
        issue(i + 1, (i + 1) % 2)

    @pl.when(i < n_used)
    def _():
        slot = i % 2
        pltpu.make_async_copy(hp_ref.at[pl.ds(0, tm * nseg)], xbuf.at[slot], sem.at[slot]).wait()
        xb = xbuf.at[slot]
        segs = [xb[pl.ds(j, tm, stride=nseg), :] for j in range(nseg)]
        lo = [lax.bitcast_convert_type(u << 16, F32) for u in segs]
        hi = [lax.bitcast_convert_type(u & jnp.uint32(HI16), F32) for u in segs]
        x = jnp.concatenate(lo + hi, axis=1).astype(BF16)
        f = w2_ref.shape[0]
        a = _dot(x, w13_ref[:, :f])
        b = _dot(x, w13_ref[:, f:])
        y = _dot((_silu(a) * b).astype(BF16), w2_ref[...])
        for j in range(nout):
            ys_ref[pl.ds(j, tm, stride=nout), :] = y[:, j * LANES:(j + 1) * LANES]

    @pl.when(i >= n_used)
    def _():
        ys_ref[...] = jnp.zeros(ys_ref.shape, ys_ref.dtype)


def _moe_experts(hp, w13, w2, src, tile_e, n_used, tm):
    n_e, d, f2 = w13.shape
    f = f2 // 2
    p_rows = src.shape[0]
    nseg = d // 2 // LANES
    nout = d // LANES
    grid_spec = pltpu.PrefetchScalarGridSpec(
        num_scalar_prefetch=3,
        grid=(p_rows // tm,),
        in_specs=[pl.BlockSpec(memory_space=pl.ANY),
                  pl.BlockSpec((None, d, f2), lambda i, te, nu, sr: (te[i], 0, 0),
                               pipeline_mode=pl.Buffered(1)),
                  pl.BlockSpec((None, f, d), lambda i, te, nu, sr: (te[i], 0, 0),
                               pipeline_mode=pl.Buffered(1))],
        out_specs=pl.BlockSpec((tm * nout, LANES), lambda i, te, nu, sr: (i, 0)),
        scratch_shapes=[pltpu.VMEM((2, tm * nseg, LANES), jnp.uint32), pltpu.SemaphoreType.DMA((2,))])
    return pl.pallas_call(
        functools.partial(_moe_expert_kernel, tm=tm),
        out_shape=jax.ShapeDtypeStruct((p_rows * nout, LANES), F32),
        grid_spec=grid_spec,
        compiler_params=_cparams("arbitrary"),
    )(tile_e, n_used, src, hp, w13, w2)


def _moe_combine_kernel(dest_ref, ys_ref, x_ref, g_ref, mod_ref, o_ref, ybuf, sem, *, tm, nt, n_steps):
    i = pl.program_id(0)
    nout = ybuf.shape[2] // tm

    def row_copy(tile, slot, k, r):
        s0 = pl.multiple_of(dest_ref[k * nt + tile * tm + r] * nout, nout)
        d0 = pl.multiple_of(r * nout, nout)
        return pltpu.make_async_copy(ys_ref.at[pl.ds(s0, nout)], ybuf.at[slot, k, pl.ds(d0, nout)],
                                     sem.at[slot])

    def issue(tile, slot):
        def body(r, carry):
            row_copy(tile, slot, 0, r).start()
            row_copy(tile, slot, 1, r).start()
            return carry
        lax.fori_loop(0, tm, body, 0, unroll=4)

    @pl.when(i == 0)
    def _():
        issue(0, 0)

    @pl.when(i + 1 < n_steps)
    def _():
        issue(i + 1, (i + 1) % 2)

    slot = i % 2
    for k in range(2):
        pltpu.make_async_copy(ys_ref.at[pl.ds(0, tm * nout)], ybuf.at[slot, k], sem.at[slot]).wait()
    g = g_ref[...]
    w0 = g[:, 0:1]
    w1 = g[:, 1:2]
    y0 = ybuf.at[slot, 0]
    y1 = ybuf.at[slot, 1]
    for j in range(nout):
        cols = slice(j * LANES, (j + 1) * LANES)
        mix = w0 * y0[pl.ds(j, tm, stride=nout), :] + w1 * y1[pl.ds(j, tm, stride=nout), :]
        o_ref[:, cols] = x_ref[:, cols] + mod_ref[5:6, cols] * mix


def _moe_combine(ys, dest, x, rinfo, mod_l, rows):
    nt, d = x.shape
    tm = rows.row_tile(256)
    n_steps = nt // tm
    grid_spec = pltpu.PrefetchScalarGridSpec(
        num_scalar_prefetch=1,
        grid=(n_steps,),
        in_specs=[pl.BlockSpec(memory_space=pl.ANY),
                  pl.BlockSpec((tm, d), lambda i, ds: (i, 0)),
                  pl.BlockSpec((tm, LANES), lambda i, ds: (i, 0)),
                  pl.BlockSpec((None, SUBLANES, d), lambda i, ds: (rows.cond_set(i * tm), 0, 0))],
        out_specs=pl.BlockSpec((tm, d), lambda i, ds: (i, 0)),
        scratch_shapes=[pltpu.VMEM((2, 2, tm * (d // LANES), LANES), F32),
                        pltpu.SemaphoreType.DMA((2,))])
    return pl.pallas_call(
        functools.partial(_moe_combine_kernel, tm=tm, nt=nt, n_steps=n_steps),
        out_shape=jax.ShapeDtypeStruct((nt, d), F32),
        grid_spec=grid_spec,
        compiler_params=_cparams("arbitrary"),
    )(dest, ys, x, rinfo, mod_l)


def _tri_pair(n):
    lo = np.tril(np.ones((n, n), np.float32))
    return jnp.asarray(np.stack([lo, lo.T]), BF16)


def _level_halves():
    hs = []
    h = CHUNK // 2
    while h >= BASE:
        hs.append(h)
        h //= 2
    return hs


def _gla_masks():
    t = np.arange(CHUNK)[:, None]
    s = np.arange(CHUNK)[None, :]
    out = []
    for rev in (False, True):
        per = []
        for h in _level_halves():
            same = (t // (2 * h)) == (s // (2 * h))
            t_up = (t % (2 * h)) >= h
            s_up = (s % (2 * h)) >= h
            per.append(same & (~t_up & s_up if rev else t_up & ~s_up))
        same = (t // BASE) == (s // BASE)
        per.append(same & ((s >= t) if rev else (s <= t)))
        out.append(np.stack(per))
    return jnp.asarray(np.stack(out).astype(np.float32))


def _chunk_cumsum(tri, x):
    n = x.shape[1]
    h1 = x.astype(BF16)
    h2 = (x - h1.astype(F32)).astype(BF16)
    p = _dot(tri, jnp.concatenate([h1, h2], axis=1))
    return p[:, n:] + p[:, :n]


def _hgrn_chunks(chains):
    halves = _level_halves()
    c = chains[0]["q"].shape[0]
    for ch in chains:
        z, lb = ch["z"], ch["lb"]
        one_m_lb = 1.0 - lb
        e = jnp.exp(-jnp.abs(z))
        r = 1.0 / (1.0 + e)
        er = e * r
        pos = z >= 0
        f = lb + one_m_lb * jnp.where(pos, r, er)
        ch["logf"] = jnp.log(jnp.maximum(f, GATE_FLOOR))
        ch["key"] = one_m_lb * jnp.where(pos, er, r)
    for ch in chains:
        ch["b"] = _chunk_cumsum(ch["tri"], ch["logf"])
    for ch in chains:
        ch["scores"] = None
    for lvl, h in enumerate(halves):
        for ch in chains:
            b = ch["b"]
            b3 = b.reshape(c // (2 * h), 2 * h, LANES)
            r = h if ch["rev"] else h - 1
            w = jnp.exp(-jnp.abs(b3 - b3[:, r:r + 1, :])).reshape(c, LANES)
            part = _dot_nt((ch["q"] * w).astype(BF16), (ch["key"] * w).astype(BF16)) * ch["masks"][lvl]
            ch["scores"] = part if ch["scores"] is None else ch["scores"] + part
    for ch in chains:
        b3 = ch["b"].reshape(c // BASE, BASE, LANES)
        l3 = ch["logf"].reshape(c // BASE, BASE, LANES)
        r = BASE - 1 if ch["rev"] else 0
        eq = (b3 - (b3[:, r:r + 1, :] - l3[:, r:r + 1, :])).reshape(c, LANES)
        qb = (ch["q"] * jnp.exp(eq)).astype(BF16)
        kb = (ch["key"] * jnp.exp(jnp.minimum(-eq, EXP_CLAMP))).astype(BF16)
        ch["scores"] = ch["scores"] + _dot_nt(qb, kb) * ch["masks"][len(halves)]
    out = []
    for ch in chains:
        b, st, v = ch["b"], ch["st"], ch["v"]
        o = _dot(ch["scores"].astype(BF16), v.astype(BF16)) + \
            _dot_nt((ch["q"] * jnp.exp(b)).astype(BF16), st.astype(BF16))
        r_last = 0 if ch["rev"] else c - 1
        b_last = b[r_last:r_last + 1, :]
        kt = (ch["key"] * jnp.exp(b_last - b)).astype(BF16)
        out.append((o, st * jnp.exp(b_last) + _dot(v.T.astype(BF16), kt)))
    return out


HG_HEADS_PER_STEP = 2


def _hgrn_kernel(*refs, seq_len, use_s0, hps):
    if use_s0:
        (q_ref, zf_ref, zb_ref, v_ref, g_ref, lb_ref, on_ref, tri_ref, msk_ref, s0f_ref, s0b_ref,
         y_ref, st_ref, of_ref, ob_ref) = refs
    else:
        (q_ref, zf_ref, zb_ref, v_ref, g_ref, lb_ref, on_ref, tri_ref, msk_ref,
         y_ref, st_ref, of_ref, ob_ref) = refs
    nc = seq_len // CHUNK
    lanes = [slice(hh * LANES, (hh + 1) * LANES) for hh in range(hps)]

    def rows_of(ci):
        return pl.ds(pl.multiple_of(ci * CHUNK, CHUNK), CHUNK)

    def body(i, sts):
        rf = rows_of(i)
        rb = rows_of(nc - 1 - i)
        chains = []
        for hh, ln in enumerate(lanes):
            chains.append(dict(q=q_ref[rf, ln], z=zf_ref[rf, ln], v=v_ref[rf, ln], lb=lb_ref[:, ln],
                               tri=tri_ref[0], masks=msk_ref.at[0], st=sts[hh], rev=False))
        for hh, ln in enumerate(lanes):
            chains.append(dict(q=q_ref[rb, ln], z=zb_ref[rb, ln], v=v_ref[rb, ln], lb=lb_ref[:, ln],
                               tri=tri_ref[1], masks=msk_ref.at[1], st=sts[hps + hh], rev=True))
        res = _hgrn_chunks(chains)
        for hh, ln in enumerate(lanes):
            of_ref[rf, ln] = res[hh][0]
            ob_ref[rb, ln] = res[hps + hh][0]
        return tuple(st for _, st in res)

    zero = jnp.zeros((LANES, LANES), F32)
    init = tuple((s0f_ref[hh].T if use_s0 else zero) for hh in range(hps)) + \
        tuple((s0b_ref[hh].T if use_s0 else zero) for hh in range(hps))
    sts = lax.fori_loop(0, nc, body, init)
    for hh in range(hps):
        st_ref[0, hh] = sts[hh].T
        st_ref[1, hh] = sts[hps + hh].T

    def finish(ci, carry):
        rows = rows_of(ci)
        for ln in lanes:
            o = of_ref[rows, ln] + ob_ref[rows, ln]
            ms = jnp.mean(o * o, axis=-1, keepdims=True)
            y = o * lax.rsqrt(ms + NORM_EPS) * on_ref[...]
            y_ref[rows, ln] = (y * _silu(g_ref[rows, ln])).astype(y_ref.dtype)
        return carry

    lax.fori_loop(0, nc, finish, 0)


def _hgrn_core(proj, lb_row, onorm, s0, n_seq, seq_len, row_block0, n_heads):
    use_s0 = s0 is not None
    hd = LANES
    hps = HG_HEADS_PER_STEP * (2 if seq_len * 4 <= 1024 else 1)
    hps = math.gcd(hps, n_heads)
    nhb = n_heads // hps
    wd = hps * hd

    def col(off):
        return pl.BlockSpec((seq_len, wd), lambda s, h, off=off: (row_block0 + s, off * nhb + h))

    in_specs = [col(0), col(1), col(2), col(3), col(4),
                pl.BlockSpec((1, wd), lambda s, h: (0, h)),
                pl.BlockSpec((1, hd), lambda s, h: (0, 0)),
                pl.BlockSpec((2, CHUNK, CHUNK), lambda s, h: (0, 0, 0)),
                pl.BlockSpec((2, len(_level_halves()) + 1, CHUNK, CHUNK), lambda s, h: (0, 0, 0, 0))]
    args = [proj] * 5 + [lb_row, onorm.reshape(1, hd), _tri_pair(CHUNK), _gla_masks()]
    if use_s0:
        in_specs += [pl.BlockSpec((None, None, hps, hd, hd), lambda s, h: (s, 0, h, 0, 0)),
                     pl.BlockSpec((None, None, hps, hd, hd), lambda s, h: (s, 1, h, 0, 0))]
        args += [s0, s0]
    y, st = pl.pallas_call(
        functools.partial(_hgrn_kernel, seq_len=seq_len, use_s0=use_s0, hps=hps),
        out_shape=[jax.ShapeDtypeStruct((n_seq * seq_len, n_heads * hd), BF16),
                   jax.ShapeDtypeStruct((n_seq, 2, n_heads, hd, hd), F32)],
        grid=(n_seq, nhb),
        in_specs=in_specs,
        out_specs=[pl.BlockSpec((seq_len, wd), lambda s, h: (s, h)),
                   pl.BlockSpec((None, 2, hps, hd, hd), lambda s, h: (s, 0, h, 0, 0))],
        scratch_shapes=[pltpu.VMEM((seq_len, wd), F32), pltpu.VMEM((seq_len, wd), F32)],
        compiler_params=_cparams("parallel", "parallel"),
    )(*args)
    return y, st


def _ssd_conv_kernel(x_ref, w_ref, b_ref, o_ref, *, seq_len, n_taps):
    x = x_ref[...]
    row = lax.broadcasted_iota(jnp.int32, x.shape, 0)
    half = n_taps // 2
    acc = x * w_ref[half:half + 1, :] + b_ref[...]
    for j in range(n_taps):
        d = j - half
        if d == 0:
            continue
        shifted = pltpu.roll(x, (-d) % seq_len, 0)
        valid = (row + d >= 0) & (row + d < seq_len)
        acc = acc + jnp.where(valid, shifted, 0.0) * w_ref[j:j + 1, :]
    o_ref[...] = _silu(acc)


def _ssd_conv(proj, conv_w, conv_b, col_block0, n_seq, seq_len, row_block0):
    n_taps, c = conv_w.shape
    tc = _pick_tile(math.gcd(c, col_block0 * LANES), max(256, (512 * 1024) // seq_len))
    assert (col_block0 * LANES) % tc == 0
    cb0 = col_block0 * LANES // tc
    return pl.pallas_call(
        functools.partial(_ssd_conv_kernel, seq_len=seq_len, n_taps=n_taps),
        out_shape=jax.ShapeDtypeStruct((n_seq * seq_len, c), F32),
        grid=(n_seq, c // tc),
        in_specs=[pl.BlockSpec((seq_len, tc), lambda s, j: (row_block0 + s, cb0 + j)),
                  pl.BlockSpec((n_taps, tc), lambda s, j: (0, j)),
                  pl.BlockSpec((1, tc), lambda s, j: (0, j))],
        out_specs=pl.BlockSpec((seq_len, tc), lambda s, j: (s, j)),
        compiler_params=_cparams("parallel", "parallel"),
    )(proj, conv_w, conv_b.reshape(1, c))


def _ssd_dt_kernel(d_ref, bias_ref, a_ref, o_ref):
    x = d_ref[...] + bias_ref[...]
    dt = jnp.maximum(x, 0.0) + jnp.log1p(jnp.exp(-jnp.abs(x)))
    o_ref[0] = dt.T
    o_ref[1] = (dt * a_ref[...]).T


def _ssd_dt(proj, dt_bias, a_log, col_block, n_seq, seq_len, row_block0):
    nh2 = dt_bias.size
    assert nh2 == LANES
    a_row = (-jnp.exp(a_log.astype(F32))).reshape(1, nh2)
    return pl.pallas_call(
        _ssd_dt_kernel,
        out_shape=jax.ShapeDtypeStruct((n_seq, 2, nh2, seq_len), F32),
        grid=(n_seq,),
        in_specs=[pl.BlockSpec((seq_len, nh2), lambda s: (row_block0 + s, col_block)),
                  pl.BlockSpec((1, nh2), lambda s: (0, 0)),
                  pl.BlockSpec((1, nh2), lambda s: (0, 0))],
        out_specs=pl.BlockSpec((None, 2, nh2, seq_len), lambda s: (s, 0, 0, 0)),
        compiler_params=_cparams("parallel"),
    )(proj, dt_bias.reshape(1, nh2).astype(F32), a_row)


def _ssd_decays(dta, tri, rev):
    dt_t, a_t = dta[0], dta[1]
    hpg, c = dt_t.shape
    acs_t = _chunk_cumsum_rows(a_t, tri)
    r_last = 0 if rev else c - 1
    a_last = acs_t[:, r_last:r_last + 1]
    pad = jnp.zeros((LANES - 3 * hpg, c), F32)
    cols = jnp.concatenate([acs_t, dt_t * jnp.exp(a_last - acs_t), jnp.exp(acs_t), pad], axis=0).T
    return dict(acs_t=acs_t, dt_t=dt_t, cdec=jnp.exp(a_last), cols=cols)


def _ssd_chunks(chains, hpg, hdim):
    c = chains[0]["xs"].shape[0]
    pair = LANES // hdim
    ti = lax.broadcasted_iota(jnp.int32, (c, c), 0)
    si = lax.broadcasted_iota(jnp.int32, (c, c), 1)
    lane = lax.broadcasted_iota(jnp.int32, (c, LANES), 1)
    sels = [(lane >= u * hdim) & (lane < (u + 1) * hdim) for u in range(pair)]
    for ch in chains:
        ch.update(ch["dec"])
        ch["causal"] = (si >= ti) if ch["rev"] else (si <= ti)
    for ch in chains:
        cmb = ch["cm"].astype(BF16)
        ch["cb"] = _dot_nt(cmb, ch["bm"].astype(BF16))
        ch["y_inter"] = _dot(cmb, ch["h"].astype(BF16))
        ch["bt"] = ch["bm"].T.astype(BF16)
        ch["ys"] = []
        ch["hs"] = []
    for p in range(hpg // pair):
        for ch in chains:
            cols, acs_t, dt_t = ch["cols"], ch["acs_t"], ch["dt_t"]
            ms = []
            dtd_col = e_col = cdec_row = None
            for u in range(pair):
                hh = p * pair + u
                seg = jnp.where(ch["causal"], cols[:, hh:hh + 1] - acs_t[hh:hh + 1, :], NEG_BIG)
                ms.append((ch["cb"] * jnp.exp(seg) * dt_t[hh:hh + 1, :]).astype(BF16))
                dtd = cols[:, hpg + hh:hpg + hh + 1]
                ec = cols[:, 2 * hpg + hh:2 * hpg + hh + 1]
                cd = ch["cdec"][hh:hh + 1, :]
                dtd_col = dtd if dtd_col is None else jnp.where(sels[u], dtd, dtd_col)
                e_col = ec if e_col is None else jnp.where(sels[u], ec, e_col)
                cdec_row = cd if cdec_row is None else jnp.where(sels[u][:1, :], cd, cdec_row)
            x2 = ch["xs"][:, p * LANES:(p + 1) * LANES]
            rhs = [jnp.where(sels[u], x2, 0.0).astype(BF16) for u in range(pair)]
            y_intra = _dot(jnp.concatenate(ms, axis=1), jnp.concatenate(rhs, axis=0))
            ch["ys"].append(y_intra + ch["y_inter"][:, p * LANES:(p + 1) * LANES] * e_col)
            upd = _dot(ch["bt"], (x2 * dtd_col).astype(BF16))
            ch["hs"].append(ch["h"][:, p * LANES:(p + 1) * LANES] * cdec_row + upd)
    return [(jnp.concatenate(ch["ys"], axis=1), jnp.concatenate(ch["hs"], axis=1)) for ch in chains]


def _chunk_cumsum_rows(a_t, tri):
    h = a_t.shape[0]
    parts = jnp.concatenate(_split3(a_t), axis=0).astype(BF16)
    p = _dot(parts, tri)
    return (p[2 * h:] + p[h:2 * h]) + p[:h]


def _ssd_kernel(*refs, seq_len, use_h0, hpg, hdim):
    if use_h0:
        (xs_ref, bm_ref, cm_ref, z_ref, dta_ref, dsk_ref, nw_ref, tri_ref, h0f_ref, h0b_ref,
         y_ref, st_ref, yf_ref, yb_ref) = refs
    else:
        (xs_ref, bm_ref, cm_ref, z_ref, dta_ref, dsk_ref, nw_ref, tri_ref,
         y_ref, st_ref, yf_ref, yb_ref) = refs
    nc = seq_len // CHUNK
    width = hpg * hdim
    n_state = bm_ref.shape[1]

    def rows_of(ci):
        return pl.ds(pl.multiple_of(ci * CHUNK, CHUNK), CHUNK)

    def decays(ci, rev):
        return _ssd_decays(dta_ref[:, 1 if rev else 0, ci], tri_ref[0 if rev else 1], rev)

    def body(i, carry):
        h_f, h_b, dec_f, dec_b = carry
        rf = rows_of(i)
        cb = nc - 1 - i
        rb = rows_of(cb)
        nxt_f = decays(jnp.minimum(i + 1, nc - 1), False)
        nxt_b = decays(jnp.maximum(cb - 1, 0), True)
        (y_f, h_f), (y_b, h_b) = _ssd_chunks(
            [dict(xs=xs_ref[rf, :], bm=bm_ref[rf, :], cm=cm_ref[rf, :], dec=dec_f, h=h_f, rev=False),
             dict(xs=xs_ref[rb, :], bm=bm_ref[rb, :], cm=cm_ref[rb, :], dec=dec_b, h=h_b, rev=True)],
            hpg, hdim)
        yf_ref[rf, :] = y_f
        yb_ref[rb, :] = y_b
        return h_f, h_b, nxt_f, nxt_b

    def load_state(ref):
        return ref[...].reshape(width, n_state).T

    zero = jnp.zeros((n_state, width), F32)
    init = (load_state(h0f_ref), load_state(h0b_ref)) if use_h0 else (zero, zero)
    h_f, h_b, _, _ = lax.fori_loop(0, nc, body, init + (decays(0, False), decays(nc - 1, True)))
    st_ref[0] = h_f.T.reshape(hpg, hdim, n_state)
    st_ref[1] = h_b.T.reshape(hpg, hdim, n_state)

    def finish(ci, carry):
        rows = rows_of(ci)
        y = yf_ref[rows, :] + yb_ref[rows, :] + dsk_ref[...] * xs_ref[rows, :]
        y = y * _silu(z_ref[rows, :])
        ms = jnp.mean(y * y, axis=-1, keepdims=True)
        y_ref[rows, :] = (y * lax.rsqrt(ms + NORM_EPS) * nw_ref[...]).astype(y_ref.dtype)
        return carry

    lax.fori_loop(0, nc, finish, 0)


def _ssd_core(xbc, proj, dta, d_cols, norm_w, h0, n_seq, seq_len, row_block0, n_groups, n_heads,
              hdim, n_state):
    use_h0 = h0 is not None
    hpg = n_heads // n_groups
    width = hpg * hdim
    inner = n_heads * hdim
    assert n_state == LANES and width % LANES == 0 and inner % width == 0
    nc = seq_len // CHUNK
    b_blk0 = inner // n_state
    c_blk0 = b_blk0 + n_groups
    in_specs = [pl.BlockSpec((seq_len, width), lambda s, g: (s, g)),
                pl.BlockSpec((seq_len, n_state), lambda s, g: (s, b_blk0 + g)),
                pl.BlockSpec((seq_len, n_state), lambda s, g: (s, c_blk0 + g)),
                pl.BlockSpec((seq_len, width), lambda s, g: (row_block0 + s, g)),
                pl.BlockSpec((None, 2, 2, None, nc, hpg, CHUNK), lambda s, g: (s, 0, 0, g, 0, 0, 0)),
                pl.BlockSpec((1, width), lambda s, g: (0, g)),
                pl.BlockSpec((1, width), lambda s, g: (0, g)),
                pl.BlockSpec((2, CHUNK, CHUNK), lambda s, g: (0, 0, 0))]
    args = [xbc, xbc, xbc, proj, dta, d_cols, norm_w.reshape(1, inner), _tri_pair(CHUNK)]
    if use_h0:
        st_in = (None, None, hpg, hdim, n_state)
        in_specs += [pl.BlockSpec(st_in, lambda s, g: (s, 0, g, 0, 0)),
                     pl.BlockSpec(st_in, lambda s, g: (s, 1, g, 0, 0))]
        args += [h0, h0]
    y, st = pl.pallas_call(
        functools.partial(_ssd_kernel, seq_len=seq_len, use_h0=use_h0, hpg=hpg, hdim=hdim),
        out_shape=[jax.ShapeDtypeStruct((n_seq * seq_len, inner), BF16),
                   jax.ShapeDtypeStruct((n_seq, 2, n_heads, hdim, n_state), F32)],
        grid=(n_seq, n_groups),
        in_specs=in_specs,
        out_specs=[pl.BlockSpec((seq_len, width), lambda s, g: (s, g)),
                   pl.BlockSpec((None, 2, hpg, hdim, n_state), lambda s, g: (s, 0, g, 0, 0))],
        scratch_shapes=[pltpu.VMEM((seq_len, width), F32), pltpu.VMEM((seq_len, width), F32)],
        compiler_params=_cparams("parallel", "parallel"),
    )(*args)
    return y, st


def _rope_tables(rows, head_dim):
    quarter = head_dim // 4
    inv = ROPE_THETA ** (-np.arange(quarter, dtype=np.float64) / quarter)
    t = np.arange(rows.dec_seq)
    ang_r = (t // GRID_W)[:, None] * inv[None, :]
    ang_c = (t % GRID_W)[:, None] * inv[None, :]
    cos = np.concatenate([np.cos(ang_r)] * 2 + [np.cos(ang_c)] * 2, axis=1)
    sin = np.concatenate([-np.sin(ang_r), np.sin(ang_r), -np.sin(ang_c), np.sin(ang_c)], axis=1)
    n_dec = (rows.total - rows.ctx_rows) // rows.dec_seq
    cos = np.concatenate([np.ones((rows.ctx_rows, head_dim))] + [cos] * n_dec, axis=0)
    sin = np.concatenate([np.zeros((rows.ctx_rows, head_dim))] + [sin] * n_dec, axis=0)
    return jnp.asarray(cos, F32), jnp.asarray(sin, F32)


def _qk_prep_kernel(p_ref, cos_ref, sin_ref, qn_ref, kn_ref, q_ref, k_ref, *, n_q, n_kv, scale):
    cos = cos_ref[...]
    sin = sin_ref[...]
    lane = lax.broadcasted_iota(jnp.int32, cos.shape, 1)
    first = (lane % (LANES // 2)) < (LANES // 4)

    def norm_rope(x, w):
        ms = jnp.mean(x * x, axis=-1, keepdims=True)
        y = x * lax.rsqrt(ms + NORM_EPS) * w
        partner = jnp.where(first, pltpu.roll(y, LANES - LANES // 4, 1), pltpu.roll(y, LANES // 4, 1))
        return y * cos + partner * sin

    for h in range(n_q):
        x = p_ref[:, h * LANES:(h + 1) * LANES]
        q_ref[:, h * LANES:(h + 1) * LANES] = (norm_rope(x, qn_ref[...]) * scale).astype(q_ref.dtype)
    for h in range(n_kv):
        x = p_ref[:, (n_q + h) * LANES:(n_q + h + 1) * LANES]
        k_ref[:, h * LANES:(h + 1) * LANES] = norm_rope(x, kn_ref[...])


def _qk_prep(proj, cos, sin, qn, kn, n_q, n_kv, rows):
    nt = proj.shape[0]
    hd = LANES
    tm = rows.row_tile(256)
    scale = float(hd) ** -0.5
    return pl.pallas_call(
        functools.partial(_qk_prep_kernel, n_q=n_q, n_kv=n_kv, scale=scale),
        out_shape=[jax.ShapeDtypeStruct((nt, n_q * hd), BF16),
                   jax.ShapeDtypeStruct((nt, n_kv * hd), F32)],
        grid=(nt // tm,),
        in_specs=[pl.BlockSpec((tm, proj.shape[1]), lambda i: (i, 0)),
                  pl.BlockSpec((tm, hd), lambda i: (i, 0)),
                  pl.BlockSpec((tm, hd), lambda i: (i, 0)),
                  pl.BlockSpec((1, hd), lambda i: (0, 0)),
                  pl.BlockSpec((1, hd), lambda i: (0, 0))],
        out_specs=[pl.BlockSpec((tm, n_q * hd), lambda i: (i, 0)),
                   pl.BlockSpec((tm, n_kv * hd), lambda i: (i, 0))],
        compiler_params=_cparams("parallel"),
    )(proj, cos, sin, qn.reshape(1, hd), kn.reshape(1, hd))


ATTN_KEY_CHUNK = 512


def _attn_kernel(*refs, grp, use_ctx):
    if use_ctx:
        q_ref, k_ref, v_ref, kc_ref, vc_ref, o_ref = refs
    else:
        q_ref, k_ref, v_ref, o_ref = refs
    tq = q_ref.shape[0]
    q = jnp.concatenate([q_ref[:, g * LANES:(g + 1) * LANES] for g in range(grp)], axis=0)
    chunks = [(kc_ref, vc_ref, 0, kc_ref.shape[0])] if use_ctx else []
    n_keys = k_ref.shape[0]
    kc = min(ATTN_KEY_CHUNK, n_keys)
    chunks += [(k_ref, v_ref, c0, kc) for c0 in range(0, n_keys, kc)]

    def scores(chunk):
        kr, _, c0, n = chunk
        return _dot_nt(q, kr[c0:c0 + n, :].astype(BF16))

    s_next = scores(chunks[0])
    m = l = acc = None
    for idx, (_, vr, c0, n) in enumerate(chunks):
        s = s_next
        if idx + 1 < len(chunks):
            s_next = scores(chunks[idx + 1])
        mj = jnp.max(s, axis=-1, keepdims=True)
        m_new = mj if m is None else jnp.maximum(m, mj)
        p = jnp.exp(s - m_new)
        pv = _dot(p.astype(BF16), vr[c0:c0 + n, :].astype(BF16))
        pl_sum = jnp.sum(p, axis=-1, keepdims=True)
        if m is None:
            l, acc = pl_sum, pv
        else:
            alpha = jnp.exp(m - m_new)
            l = alpha * l + pl_sum
            acc = alpha * acc + pv
        m = m_new
    o = acc / l
    for g in range(grp):
        o_ref[:, g * LANES:(g + 1) * LANES] = o[g * tq:(g + 1) * tq].astype(o_ref.dtype)


def _attention(qn, kn, proj, ctx_k, ctx_v, n_seq, seq_len, row_block0, n_q, n_kv):
    use_ctx = ctx_k is not None
    grp = n_q // n_kv
    hd = LANES
    tq = min(seq_len, 128)
    nqb = seq_len // tq
    v_blk0 = n_q + n_kv
    in_specs = [pl.BlockSpec((tq, grp * hd), lambda s, kv, i: ((row_block0 + s) * nqb + i, kv)),
                pl.BlockSpec((seq_len, hd), lambda s, kv, i: (row_block0 + s, kv)),
                pl.BlockSpec((seq_len, hd), lambda s, kv, i: (row_block0 + s, v_blk0 + kv))]
    args = [qn, kn, proj]
    if use_ctx:
        past = ctx_k.shape[1]
        in_specs += [pl.BlockSpec((None, past, hd), lambda s, kv, i: (s, 0, kv)),
                     pl.BlockSpec((None, past, hd), lambda s, kv, i: (s, 0, kv))]
        args += [ctx_k, ctx_v]
    return pl.pallas_call(
        functools.partial(_attn_kernel, grp=grp, use_ctx=use_ctx),
        out_shape=jax.ShapeDtypeStruct((n_seq * seq_len, n_q * hd), BF16),
        grid=(n_seq, n_kv, nqb),
        in_specs=in_specs,
        out_specs=pl.BlockSpec((tq, grp * hd), lambda s, kv, i: (s * nqb + i, kv)),
        compiler_params=_cparams("parallel", "parallel", "arbitrary"),
    )(*args)


def kernel(x_prompt, x_sample, state_hgrn, state_ssd, cache_k, cache_v, c, c_ctx, ada_w, ada_b, norm_w,
           hg_w_in, hg_lb_logits, hg_onorm, hg_w_o, ssd_w_in, ssd_conv_w, ssd_conv_b, ssd_a_log,
           ssd_dt_bias, ssd_d, ssd_norm, ssd_w_o, at_w_qkv, at_qn, at_kn, at_w_o, ff_w13, ff_w2,
           moe_router, moe_w13, moe_w2):
    batch, seq, d = x_prompt.shape
    n_dec, dec_seq, _ = x_sample.shape
    depth = ada_w.shape[0]
    rows = _Rows(batch * seq, dec_seq, n_dec)
    ctx_rows = rows.ctx_rows
    assert seq % CHUNK == 0 and dec_seq % CHUNK == 0 and dec_seq % seq == 0 and ctx_rows % dec_seq == 0

    hg_heads = d // LANES
    ssd_heads = ssd_a_log.shape[2]
    ssd_inner = ssd_w_o.shape[1]
    ssd_hdim = ssd_inner // ssd_heads
    ssd_nstate = state_ssd.shape[-1]
    ssd_groups = (ssd_conv_w.shape[2] - ssd_inner) // (2 * ssd_nstate)
    at_kv = cache_k.shape[3]
    at_heads = at_w_o.shape[1] // LANES
    past = cache_k.shape[2]

    x = jnp.concatenate([x_prompt.reshape(ctx_rows, d), x_sample.reshape(n_dec * dec_seq, d)], axis=0)

    n_sets = 1 + n_dec
    cond = jnp.concatenate([c_ctx[None, :], c], axis=0)
    cond = jnp.pad(cond, ((0, (-n_sets) % SUBLANES), (0, 0)))
    mod = _ada_mod(cond, ada_w, ada_b)[:, :n_sets].reshape(depth, n_sets, 6, d)
    mod = jnp.pad(mod, ((0, 0), (0, 0), (0, SUBLANES - 6), (0, 0)))

    lb_all = None
    cos_t = sin_t = None
    hg_states, ssd_states, k_list, v_list = [], [], [], []
    lat_blk_seq = ctx_rows // dec_seq

    for layer in range(depth):
        j = layer // 3
        kind = layer % 3
        mod_l = mod[layer]
        nw1 = norm_w[layer, 0]
        if kind == 0:
            if lb_all is None:
                pr = jax.nn.softmax(hg_lb_logits.astype(F32), axis=0)
                lb_all = jnp.cumsum(pr, axis=0) - pr[0]
            proj = _matmul(x, nw1, mod_l, 0, hg_w_in, j, F32, rows)
            lb_row = lb_all[j].reshape(1, -1)
            y_c, st_c = _hgrn_core(proj, lb_row, hg_onorm[j], None, batch, seq, 0, hg_heads)
            y_l, _ = _hgrn_core(proj, lb_row, hg_onorm[j], state_hgrn[:, j], n_dec, dec_seq,
                                lat_blk_seq, hg_heads)
            hg_states.append(st_c)
            x = _matmul_residual(y_c, y_l, hg_w_o, j, x, mod_l, 2, rows)
        elif kind == 1:
            proj = _matmul(x, nw1, mod_l, 0, ssd_w_in, j, F32, rows)
            gn2 = 2 * ssd_groups * ssd_nstate
            xbc_blk0 = ssd_inner // LANES
            dt_blk = (2 * ssd_inner + gn2) // LANES
            d_cols = jnp.repeat((ssd_d[j, 0] + ssd_d[j, 1]).astype(F32), ssd_hdim).reshape(1, ssd_inner)
            hpg = ssd_heads // ssd_groups
            ys = []
            for (n_s, s_len, rb0, h0) in ((batch, seq, 0, None),
                                          (n_dec, dec_seq, lat_blk_seq, state_ssd[:, j])):
                xbc = _ssd_conv(proj, ssd_conv_w[j], ssd_conv_b[j], xbc_blk0, n_s, s_len, rb0)
                dta = _ssd_dt(proj, ssd_dt_bias[j], ssd_a_log[j], dt_blk, n_s, s_len, rb0)
                nc = s_len // CHUNK
                dta = dta.reshape(n_s, 2, 2, ssd_groups, hpg, nc, CHUNK).transpose(0, 1, 2, 3, 5, 4, 6)
                y_p, st_p = _ssd_core(xbc, proj, dta, d_cols, ssd_norm[j], h0, n_s, s_len, rb0,
                                      ssd_groups, ssd_heads, ssd_hdim, ssd_nstate)
                ys.append(y_p)
                if h0 is None:
                    ssd_states.append(st_p)
            x = _matmul_residual(ys[0], ys[1], ssd_w_o, j, x, mod_l, 2, rows)
        else:
            proj = _matmul(x, nw1, mod_l, 0, at_w_qkv, j, F32, rows)
            if cos_t is None:
                cos_t, sin_t = _rope_tables(rows, LANES)
            qn, kn = _qk_prep(proj, cos_t, sin_t, at_qn[j], at_kn[j], at_heads, at_kv, rows)
            o_c = _attention(qn, kn, proj, None, None, batch, seq, 0, at_heads, at_kv)
            ck = cache_k[:, j].reshape(n_dec, past, at_kv * LANES)
            cv = cache_v[:, j].reshape(n_dec, past, at_kv * LANES)
            o_l = _attention(qn, kn, proj, ck, cv, n_dec, dec_seq, lat_blk_seq, at_heads, at_kv)
            k_list.append(kn[:ctx_rows].reshape(batch, seq, at_kv, LANES))
            v_list.append(proj[:ctx_rows, (at_heads + at_kv) * LANES:].reshape(batch, seq, at_kv, LANES))
            x = _matmul_residual(o_c, o_l, at_w_o, j, x, mod_l, 2, rows)

        if layer % 2 == 0:
            h2 = _norm_mod(x, norm_w[layer, 1], mod_l, 1, rows)
            act = _matmul_swiglu(h2, ff_w13, layer // 2)
            x = _matmul_residual(act, None, ff_w2, layer // 2, x, mod_l, 5, rows)
        else:
            e = layer // 2
            n_e = moe_router.shape[2]
            tm_e = min(MOE_TILE, rows.total)
            hp, rinfo = _router(x, norm_w[layer, 1], mod_l, moe_router[e], rows)
            src, dest, tile_e, n_used = _route_tables(rinfo, n_e, tm_e)
            ys = _moe_experts(hp, moe_w13[e].astype(BF16), moe_w2[e].astype(BF16), src, tile_e,
                              n_used, tm_e)
            x = _moe_combine(ys, dest, x, rinfo, mod_l, rows)

    y_prompt = x[:ctx_rows].reshape(batch, seq, d)
    y_sample = x[ctx_rows:].reshape(n_dec, dec_seq, d)
    return (y_prompt, y_sample, jnp.stack(hg_states, axis=1), jnp.stack(ssd_states, axis=1),
            jnp.stack(k_list, axis=1), jnp.stack(v_list, axis=1))
```

```python
import functools
import math

import numpy as np
import jax
import jax.numpy as jnp
from jax import lax
from jax.experimental import pallas as pl
from jax.experimental.pallas import tpu as pltpu

F32 = jnp.float32
BF16 = jnp.bfloat16

NORM_EPS = 1e-6
GATE_FLOOR = 1e-30
ROPE_THETA = 10000.0
GRID_W = 64
LANES = 128
SUBLANES = 8
CHUNK = 128
BASE = 16
EXP_CLAMP = 80.0
NEG_BIG = -1e30
VMEM_LIMIT_BYTES = 52 * 1024 * 1024


def _cparams(*sem):
    return pltpu.CompilerParams(dimension_semantics=sem, vmem_limit_bytes=VMEM_LIMIT_BYTES)


def _sigmoid(x):
    return 1.0 / (1.0 + jnp.exp(-x))


def _silu(x):
    return x * _sigmoid(x)


def _pick_tile(n, cap, quantum=LANES):
    best = None
    t = quantum
    while t <= min(n, cap):
        if n % t == 0:
            best = t
        t += quantum
    assert best is not None, (n, cap)
    return best


def _dot(a, b):
    return jnp.dot(a, b, preferred_element_type=F32)


def _dot_nt(a, b):
    return lax.dot_general(a, b, (((1,), (1,)), ((), ())), preferred_element_type=F32)


def _split3(x):
    h1 = x.astype(BF16).astype(F32)
    r1 = x - h1
    h2 = r1.astype(BF16).astype(F32)
    h3 = (r1 - h2).astype(BF16).astype(F32)
    return h1, h2, h3


def _ada_kernel(c_ref, w_ref, b_ref, o_ref):
    s = _silu(c_ref[...]).astype(BF16)
    o_ref[...] = _dot(s, w_ref[...].astype(BF16)) + b_ref[...]


def _ada_mod(cond_rows, ada_w, ada_b):
    nl, d, n6 = ada_w.shape
    tn = _pick_tile(n6, 1024)
    rows = cond_rows.shape[0]
    return pl.pallas_call(
        _ada_kernel,
        out_shape=jax.ShapeDtypeStruct((nl, rows, n6), F32),
        grid=(nl, n6 // tn),
        in_specs=[pl.BlockSpec((rows, d), lambda l, j: (0, 0)),
                  pl.BlockSpec((None, d, tn), lambda l, j: (l, 0, j)),
                  pl.BlockSpec((None, 1, tn), lambda l, j: (l, 0, j))],
        out_specs=pl.BlockSpec((None, rows, tn), lambda l, j: (l, 0, j)),
        compiler_params=_cparams("parallel", "parallel"),
    )(cond_rows, ada_w, ada_b.reshape(nl, 1, n6))


def _modulated_norm(x, nw, mod, which):
    ms = jnp.mean(x * x, axis=-1, keepdims=True)
    y = x * lax.rsqrt(ms + NORM_EPS) * nw
    sh = mod[3 * which:3 * which + 1, :]
    sc = mod[3 * which + 1:3 * which + 2, :]
    return y * (1.0 + sc) + sh


class _Rows:
    def __init__(self, ctx_rows, dec_seq, n_dec):
        self.ctx_rows = ctx_rows
        self.dec_seq = dec_seq
        self.total = ctx_rows + dec_seq * n_dec
        self.tile = math.gcd(ctx_rows, dec_seq)

    def row_tile(self, cap):
        t = self.tile
        while t > cap and t % 2 == 0:
            t //= 2
        return t

    def cond_set(self, row0):
        return jnp.where(row0 < self.ctx_rows, 0, 1 + (row0 - self.ctx_rows) // self.dec_seq)


def _cache_weight(w_ref, wbf_ref):
    @pl.when(pl.program_id(1) == 0)
    def _():
        wbf_ref[...] = w_ref[...].astype(BF16)


def _normed_lhs(x_ref, nw_ref, mod_ref, which):
    return _modulated_norm(x_ref[...], nw_ref[...], mod_ref[...], which).astype(BF16)


def _norm_specs(rows, tm, k):
    return [pl.BlockSpec((tm, k), lambda j, i: (i, 0)),
            pl.BlockSpec((1, k), lambda j, i: (0, 0)),
            pl.BlockSpec((None, SUBLANES, k), lambda j, i: (rows.cond_set(i * tm), 0, 0))]


def _mm_plain_kernel(x_ref, nw_ref, mod_ref, w_ref, o_ref, wbf_ref, *, which):
    _cache_weight(w_ref, wbf_ref)
    h = _normed_lhs(x_ref, nw_ref, mod_ref, which)
    o_ref[...] = _dot(h, wbf_ref[...]).astype(o_ref.dtype)


def _matmul(x, nw, mod_l, which, w, layer, out_dtype, rows, tn_cap=1152):
    m, k = x.shape
    n = w.shape[2]
    tm = rows.row_tile(512)
    tn = _pick_tile(n, tn_cap)
    return pl.pallas_call(
        functools.partial(_mm_plain_kernel, which=which),
        out_shape=jax.ShapeDtypeStruct((m, n), out_dtype),
        grid=(n // tn, m // tm),
        in_specs=_norm_specs(rows, tm, k) + [pl.BlockSpec((None, k, tn), lambda j, i: (layer, 0, j))],
        out_specs=pl.BlockSpec((tm, tn), lambda j, i: (i, j)),
        scratch_shapes=[pltpu.VMEM((k, tn), BF16)],
        compiler_params=_cparams("parallel", "arbitrary"),
    )(x, nw.reshape(1, k), mod_l, w)


def _mm_res_kernel(*refs, gate_row, n_a):
    x_refs = refs[:-5]
    w_ref, r_ref, mod_ref, o_ref, wbf_ref = refs[-5:]
    _cache_weight(w_ref, wbf_ref)

    def finish(x_ref):
        acc = _dot(x_ref[...], wbf_ref[...])
        o_ref[...] = r_ref[...] + mod_ref[gate_row:gate_row + 1, :] * acc

    if len(x_refs) == 1:
        finish(x_refs[0])
    else:
        @pl.when(pl.program_id(1) < n_a)
        def _():
            finish(x_refs[0])

        @pl.when(pl.program_id(1) >= n_a)
        def _():
            finish(x_refs[1])


def _matmul_residual(x_ctx, x_lat, w, layer, res, mod_l, gate_row, rows):
    k = x_ctx.shape[1]
    m = rows.total
    n = w.shape[2]
    tm = rows.row_tile(512)
    tn = _pick_tile(n, 1024 if k <= 2048 else 512)
    n_a = x_ctx.shape[0] // tm
    if x_lat is None:
        xs = [x_ctx]
        x_specs = [pl.BlockSpec((tm, k), lambda j, i: (i, 0))]
    else:
        xs = [x_ctx, x_lat]
        x_specs = [pl.BlockSpec((tm, k), lambda j, i: (jnp.minimum(i, n_a - 1), 0)),
                   pl.BlockSpec((tm, k), lambda j, i: (jnp.maximum(i - n_a, 0), 0))]
    return pl.pallas_call(
        functools.partial(_mm_res_kernel, gate_row=gate_row, n_a=n_a),
        out_shape=jax.ShapeDtypeStruct((m, n), F32),
        grid=(n // tn, m // tm),
        in_specs=x_specs + [
            pl.BlockSpec((None, k, tn), lambda j, i: (layer, 0, j)),
            pl.BlockSpec((tm, tn), lambda j, i: (i, j)),
            pl.BlockSpec((None, SUBLANES, tn), lambda j, i: (rows.cond_set(i * tm), 0, j))],
        out_specs=pl.BlockSpec((tm, tn), lambda j, i: (i, j)),
        scratch_shapes=[pltpu.VMEM((k, tn), BF16)],
        compiler_params=_cparams("parallel", "arbitrary"),
    )(*xs, w, res, mod_l)


def _norm_mod_kernel(x_ref, nw_ref, mod_ref, o_ref, *, which):
    o_ref[...] = _normed_lhs(x_ref, nw_ref, mod_ref, which)


def _norm_mod(x, nw, mod_l, which, rows):
    nt, d = x.shape
    tm = rows.row_tile(256)
    return pl.pallas_call(
        functools.partial(_norm_mod_kernel, which=which),
        out_shape=jax.ShapeDtypeStruct((nt, d), BF16),
        grid=(nt // tm,),
        in_specs=[pl.BlockSpec((tm, d), lambda i: (i, 0)),
                  pl.BlockSpec((1, d), lambda i: (0, 0)),
                  pl.BlockSpec((None, SUBLANES, d), lambda i: (rows.cond_set(i * tm), 0, 0))],
        out_specs=pl.BlockSpec((tm, d), lambda i: (i, 0)),
        compiler_params=_cparams("parallel"),
    )(x, nw.reshape(1, d), mod_l)


def _mm_swiglu_kernel(x_ref, wa_ref, wb_ref, o_ref, wabf_ref, wbbf_ref):
    _cache_weight(wa_ref, wabf_ref)
    _cache_weight(wb_ref, wbbf_ref)
    x = x_ref[...]
    a = _dot(x, wabf_ref[...])
    b = _dot(x, wbbf_ref[...])
    o_ref[...] = (_silu(a) * b).astype(o_ref.dtype)


def _matmul_swiglu(x, w13, layer):
    m, k = x.shape
    f = w13.shape[2] // 2
    tm = _pick_tile(m, 512, SUBLANES)
    tn = _pick_tile(f, 512)
    nb = f // tn
    return pl.pallas_call(
        _mm_swiglu_kernel,
        out_shape=jax.ShapeDtypeStruct((m, f), BF16),
        grid=(nb, m // tm),
        in_specs=[pl.BlockSpec((tm, k), lambda j, i: (i, 0)),
                  pl.BlockSpec((None, k, tn), lambda j, i: (layer, 0, j)),
                  pl.BlockSpec((None, k, tn), lambda j, i: (layer, 0, j + nb))],
        out_specs=pl.BlockSpec((tm, tn), lambda j, i: (i, j)),
        scratch_shapes=[pltpu.VMEM((k, tn), BF16), pltpu.VMEM((k, tn), BF16)],
        compiler_params=_cparams("parallel", "arbitrary"),
    )(x, w13, w13)


MOE_TILE = 512
HI16 = 0xFFFF0000


def _router_kernel(x_ref, nw_ref, mod_ref, r_ref, hp_ref, g_ref, *, n_experts):
    h = _modulated_norm(x_ref[...], nw_ref[...], mod_ref[...], 1)
    tm = h.shape[0]
    half = h.shape[1] // 2
    nseg = half // LANES
    bits = lax.bitcast_convert_type(h.astype(BF16).astype(F32), jnp.uint32)
    packed = (bits[:, :half] >> 16) | (bits[:, half:] & jnp.uint32(HI16))
    for j in range(nseg):
        hp_ref[pl.ds(j, tm, stride=nseg), :] = packed[:, j * LANES:(j + 1) * LANES]
    h1, h2, _ = _split3(h)
    r1, r2, _ = _split3(r_ref[...])
    h1, h2, r1, r2 = (t.astype(BF16) for t in (h1, h2, r1, r2))
    logits = _dot(h1, r1) + (_dot(h1, r2) + _dot(h2, r1))
    lane = lax.broadcasted_iota(jnp.int32, logits.shape, 1)
    logits = jnp.where(lane < n_experts, logits, NEG_BIG)
    m1 = jnp.max(logits, axis=-1, keepdims=True)
    i1 = jnp.min(jnp.where(logits == m1, lane, LANES), axis=-1, keepdims=True)
    rest = jnp.where(lane == i1, NEG_BIG, logits)
    m2 = jnp.max(rest, axis=-1, keepdims=True)
    i2 = jnp.min(jnp.where(rest == m2, lane, LANES), axis=-1, keepdims=True)
    e2 = jnp.exp(m2 - m1)
    w1 = 1.0 / (1.0 + e2)
    w2 = e2 * w1
    g_ref[...] = (jnp.where(lane == 0, w1, 0.0) + jnp.where(lane == 1, w2, 0.0)
                  + jnp.where(lane == 2, i1.astype(F32), 0.0) + jnp.where(lane == 3, i2.astype(F32), 0.0))


def _router(x, nw, mod_l, router, rows):
    nt, d = x.shape
    n_experts = router.shape[1]
    tm = rows.row_tile(256)
    nseg = d // 2 // LANES
    rpad = jnp.zeros((d, LANES), F32).at[:, :n_experts].set(router)
    return pl.pallas_call(
        functools.partial(_router_kernel, n_experts=n_experts),
        out_shape=[jax.ShapeDtypeStruct((nt * nseg, LANES), jnp.uint32),
                   jax.ShapeDtypeStruct((nt, LANES), F32)],
        grid=(nt // tm,),
        in_specs=[pl.BlockSpec((tm, d), lambda i: (i, 0)),
                  pl.BlockSpec((1, d), lambda i: (0, 0)),
                  pl.BlockSpec((None, SUBLANES, d), lambda i: (rows.cond_set(i * tm), 0, 0)),
                  pl.BlockSpec((d, LANES), lambda i: (0, 0))],
        out_specs=[pl.BlockSpec((tm * nseg, LANES), lambda i: (i, 0)),
                   pl.BlockSpec((tm, LANES), lambda i: (i, 0))],
        compiler_params=_cparams("parallel"),
    )(x, nw.reshape(1, d), mod_l, rpad)


def _route_tables(rinfo, n_experts, tm):
    nt = rinfo.shape[0]
    e_flat = jnp.concatenate([rinfo[:, 2], rinfo[:, 3]]).astype(jnp.int32)
    onehot = (e_flat[:, None] == jnp.arange(n_experts, dtype=jnp.int32)[None, :]).astype(jnp.int32)
    csum = jnp.cumsum(onehot, axis=0)
    rank = jnp.sum(csum * onehot, axis=1) - 1
    padded = ((csum[-1] + tm - 1) // tm) * tm
    ends = jnp.cumsum(padded)
    dest = jnp.sum(onehot * (ends - padded)[None, :], axis=1) + rank
    p_rows = 2 * nt + n_experts * tm
    tok = jnp.tile(jnp.arange(nt, dtype=jnp.int32), 2)
    src = jnp.zeros((p_rows,), jnp.int32).at[dest].set(tok, unique_indices=True)
    tile_start = jnp.arange(p_rows // tm, dtype=jnp.int32) * tm
    tile_e = jnp.sum((tile_start[:, None] >= ends[None, :]).astype(jnp.int32), axis=1)
    tile_e = jnp.minimum(tile_e, n_experts - 1)
    n_used = (ends[-1] // tm).astype(jnp.int32).reshape(1)
    return src, dest, tile_e, n_used


def _moe_expert_kernel(te_ref, nu_ref, src_ref, hp_ref, w13_ref, w2_ref, ys_ref, xbuf, sem, *, tm):
    del te_ref
    i = pl.program_id(0)
    n_used = nu_ref[0]
    nseg = xbuf.shape[1] // tm
    nout = ys_ref.shape[0] // tm

    def row_copy(tile, slot, r):
        s0 = pl.multiple_of(src_ref[tile * tm + r] * nseg, nseg)
        d0 = pl.multiple_of(r * nseg, nseg)
        return pltpu.make_async_copy(hp_ref.at[pl.ds(s0, nseg)], xbuf.at[slot, pl.ds(d0, nseg)],
                                     sem.at[slot])

    def issue(tile, slot):
        def body(r, carry):
            row_copy(tile, slot, r).start()
            return carry
        lax.fori_loop(0, tm, body, 0, unroll=8)

    @pl.when(i == 0)
    def _():
        issue(0, 0)

    @pl.when(i + 1 < n_used)
    def _():
        issue(i + 1, (i + 1) % 2)

    @pl.when(i < n_used)
    def _():
        slot = i % 2
        pltpu.make_async_copy(hp_ref.at[pl.ds(0, tm * nseg)], xbuf.at[slot], sem.at[slot]).wait()
        xb = xbuf.at[slot]
        segs = [xb[pl.ds(j, tm, stride=nseg), :] for j in range(nseg)]
        lo = [lax.bitcast_convert_type(u << 16, F32) for u in segs]
        hi = [lax.bitcast_convert_type(u & jnp.uint32(HI16), F32) for u in segs]
        x = jnp.concatenate(lo + hi, axis=1).astype(BF16)
        f = w2_ref.shape[0]
        a = _dot(x, w13_ref[:, :f])
        b = _dot(x, w13_ref[:, f:])
        y = _dot((_silu(a) * b).astype(BF16), w2_ref[...])
        for j in range(nout):
            ys_ref[pl.ds(j, tm, stride=nout), :] = y[:, j * LANES:(j + 1) * LANES]

    @pl.when(i >= n_used)
    def _():
        ys_ref[...] = jnp.zeros(ys_ref.shape, ys_ref.dtype)


def _moe_experts(hp, w13, w2, src, tile_e, n_used, tm):
    n_e, d, f2 = w13.shape
    f = f2 // 2
    p_rows = src.shape[0]
    nseg = d // 2 // LANES
    nout = d // LANES
    grid_spec = pltpu.PrefetchScalarGridSpec(
        num_scalar_prefetch=3,
        grid=(p_rows // tm,),
        in_specs=[pl.BlockSpec(memory_space=pl.ANY),
                  pl.BlockSpec((None, d, f2), lambda i, te, nu, sr: (te[i], 0, 0),
                               pipeline_mode=pl.Buffered(1)),
                  pl.BlockSpec((None, f, d), lambda i, te, nu, sr: (te[i], 0, 0),
                               pipeline_mode=pl.Buffered(1))],
        out_specs=pl.BlockSpec((tm * nout, LANES), lambda i, te, nu, sr: (i, 0)),
        scratch_shapes=[pltpu.VMEM((2, tm * nseg, LANES), jnp.uint32), pltpu.SemaphoreType.DMA((2,))])
    return pl.pallas_call(
        functools.partial(_moe_expert_kernel, tm=tm),
        out_shape=jax.ShapeDtypeStruct((p_rows * nout, LANES), F32),
        grid_spec=grid_spec,
        compiler_params=_cparams("arbitrary"),
    )(tile_e, n_used, src, hp, w13, w2)


def _moe_combine_kernel(dest_ref, ys_ref, x_ref, g_ref, mod_ref, o_ref, ybuf, sem, *, tm, nt, n_steps):
    i = pl.program_id(0)
    nout = ybuf.shape[2] // tm

    def row_copy(tile, slot, k, r):
        s0 = pl.multiple_of(dest_ref[k * nt + tile * tm + r] * nout, nout)
        d0 = pl.multiple_of(r * nout, nout)
        return pltpu.make_async_copy(ys_ref.at[pl.ds(s0, nout)], ybuf.at[slot, k, pl.ds(d0, nout)],
                                     sem.at[slot])

    def issue(tile, slot):
        def body(r, carry):
            row_copy(tile, slot, 0, r).start()
            row_copy(tile, slot, 1, r).start()
            return carry
        lax.fori_loop(0, tm, body, 0, unroll=4)

    @pl.when(i == 0)
    def _():
        issue(0, 0)

    @pl.when(i + 1 < n_steps)
    def _():
        issue(i + 1, (i + 1) % 2)

    slot = i % 2
    for k in range(2):
        pltpu.make_async_copy(ys_ref.at[pl.ds(0, tm * nout)], ybuf.at[slot, k], sem.at[slot]).wait()
    g = g_ref[...]
    w0 = g[:, 0:1]
    w1 = g[:, 1:2]
    y0 = ybuf.at[slot, 0]
    y1 = ybuf.at[slot, 1]
    for j in range(nout):
        cols = slice(j * LANES, (j + 1) * LANES)
        mix = w0 * y0[pl.ds(j, tm, stride=nout), :] + w1 * y1[pl.ds(j, tm, stride=nout), :]
        o_ref[:, cols] = x_ref[:, cols] + mod_ref[5:6, cols] * mix


def _moe_combine(ys, dest, x, rinfo, mod_l, rows):
    nt, d = x.shape
    tm = rows.row_tile(256)
    n_steps = nt // tm
    grid_spec = pltpu.PrefetchScalarGridSpec(
        num_scalar_prefetch=1,
        grid=(n_steps,),
        in_specs=[pl.BlockSpec(memory_space=pl.ANY),
                  pl.BlockSpec((tm, d), lambda i, ds: (i, 0)),
                  pl.BlockSpec((tm, LANES), lambda i, ds: (i, 0)),
                  pl.BlockSpec((None, SUBLANES, d), lambda i, ds: (rows.cond_set(i * tm), 0, 0))],
        out_specs=pl.BlockSpec((tm, d), lambda i, ds: (i, 0)),
        scratch_shapes=[pltpu.VMEM((2, 2, tm * (d // LANES), LANES), F32),
                        pltpu.SemaphoreType.DMA((2,))])
    return pl.pallas_call(
        functools.partial(_moe_combine_kernel, tm=tm, nt=nt, n_steps=n_steps),
        out_shape=jax.ShapeDtypeStruct((nt, d), F32),
        grid_spec=grid_spec,
        compiler_params=_cparams("arbitrary"),
    )(dest, ys, x, rinfo, mod_l)


def _tri_pair(n):
    lo = np.tril(np.ones((n, n), np.float32))
    return jnp.asarray(np.stack([lo, lo.T]), BF16)


def _level_halves():
    hs = []
    h = CHUNK // 2
    while h >= BASE:
        hs.append(h)
        h //= 2
    return hs


def _gla_masks():
    t = np.arange(CHUNK)[:, None]
    s = np.arange(CHUNK)[None, :]
    out = []
    for rev in (False, True):
        per = []
        for h in _level_halves():
            same = (t // (2 * h)) == (s // (2 * h))
            t_up = (t % (2 * h)) >= h
            s_up = (s % (2 * h)) >= h
            per.append(same & (~t_up & s_up if rev else t_up & ~s_up))
        same = (t // BASE) == (s // BASE)
        per.append(same & ((s >= t) if rev else (s <= t)))
        out.append(np.stack(per))
    return jnp.asarray(np.stack(out).astype(np.float32))


def _chunk_cumsum(tri, x):
    n = x.shape[1]
    h1 = x.astype(BF16)
    h2 = (x - h1.astype(F32)).astype(BF16)
    p = _dot(tri, jnp.concatenate([h1, h2], axis=1))
    return p[:, n:] + p[:, :n]


def _hgrn_chunks(chains):
    halves = _level_halves()
    c = chains[0]["q"].shape[0]
    for ch in chains:
        z, lb = ch["z"], ch["lb"]
        one_m_lb = 1.0 - lb
        e = jnp.exp(-jnp.abs(z))
        r = 1.0 / (1.0 + e)
        er = e * r
        pos = z >= 0
        f = lb + one_m_lb * jnp.where(pos, r, er)
        ch["logf"] = jnp.log(jnp.maximum(f, GATE_FLOOR))
        ch["key"] = one_m_lb * jnp.where(pos, er, r)
    for ch in chains:
        ch["b"] = _chunk_cumsum(ch["tri"], ch["logf"])
    for ch in chains:
        ch["scores"] = None
    for lvl, h in enumerate(halves):
        for ch in chains:
            b = ch["b"]
            b3 = b.reshape(c // (2 * h), 2 * h, LANES)
            r = h if ch["rev"] else h - 1
            w = jnp.exp(-jnp.abs(b3 - b3[:, r:r + 1, :])).reshape(c, LANES)
            part = _dot_nt((ch["q"] * w).astype(BF16), (ch["key"] * w).astype(BF16)) * ch["masks"][lvl]
            ch["scores"] = part if ch["scores"] is None else ch["scores"] + part
    for ch in chains:
        b3 = ch["b"].reshape(c // BASE, BASE, LANES)
        l3 = ch["logf"].reshape(c // BASE, BASE, LANES)
        r = BASE - 1 if ch["rev"] else 0
        eq = (b3 - (b3[:, r:r + 1, :] - l3[:, r:r + 1, :])).reshape(c, LANES)
        qb = (ch["q"] * jnp.exp(eq)).astype(BF16)
        kb = (ch["key"] * jnp.exp(jnp.minimum(-eq, EXP_CLAMP))).astype(BF16)
        ch["scores"] = ch["scores"] + _dot_nt(qb, kb) * ch["masks"][len(halves)]
    out = []
    for ch in chains:
        b, st, v = ch["b"], ch["st"], ch["v"]
        o = _dot(ch["scores"].astype(BF16), v.astype(BF16)) + \
            _dot_nt((ch["q"] * jnp.exp(b)).astype(BF16), st.astype(BF16))
        r_last = 0 if ch["rev"] else c - 1
        b_last = b[r_last:r_last + 1, :]
        kt = (ch["key"] * jnp.exp(b_last - b)).astype(BF16)
        out.append((o, st * jnp.exp(b_last) + _dot(v.T.astype(BF16), kt)))
    return out


HG_HEADS_PER_STEP = 2


def _hgrn_kernel(*refs, seq_len, use_s0, hps):
    if use_s0:
        (q_ref, zf_ref, zb_ref, v_ref, g_ref, lb_ref, on_ref, tri_ref, msk_ref, s0f_ref, s0b_ref,
         y_ref, st_ref, of_ref, ob_ref) = refs
    else:
        (q_ref, zf_ref, zb_ref, v_ref, g_ref, lb_ref, on_ref, tri_ref, msk_ref,
         y_ref, st_ref, of_ref, ob_ref) = refs
    nc = seq_len // CHUNK
    lanes = [slice(hh * LANES, (hh + 1) * LANES) for hh in range(hps)]

    def rows_of(ci):
        return pl.ds(pl.multiple_of(ci * CHUNK, CHUNK), CHUNK)

    def body(i, sts):
        rf = rows_of(i)
        rb = rows_of(nc - 1 - i)
        chains = []
        for hh, ln in enumerate(lanes):
            chains.append(dict(q=q_ref[rf, ln], z=zf_ref[rf, ln], v=v_ref[rf, ln], lb=lb_ref[:, ln],
                               tri=tri_ref[0], masks=msk_ref.at[0], st=sts[hh], rev=False))
        for hh, ln in enumerate(lanes):
            chains.append(dict(q=q_ref[rb, ln], z=zb_ref[rb, ln], v=v_ref[rb, ln], lb=lb_ref[:, ln],
                               tri=tri_ref[1], masks=msk_ref.at[1], st=sts[hps + hh], rev=True))
        res = _hgrn_chunks(chains)
        for hh, ln in enumerate(lanes):
            of_ref[rf, ln] = res[hh][0]
            ob_ref[rb, ln] = res[hps + hh][0]
        return tuple(st for _, st in res)

    zero = jnp.zeros((LANES, LANES), F32)
    init = tuple((s0f_ref[hh].T if use_s0 else zero) for hh in range(hps)) + \
        tuple((s0b_ref[hh].T if use_s0 else zero) for hh in range(hps))
    sts = lax.fori_loop(0, nc, body, init)
    for hh in range(hps):
        st_ref[0, hh] = sts[hh].T
        st_ref[1, hh] = sts[hps + hh].T

    def finish(ci, carry):
        rows = rows_of(ci)
        for ln in lanes:
            o = of_ref[rows, ln] + ob_ref[rows, ln]
            ms = jnp.mean(o * o, axis=-1, keepdims=True)
            y = o * lax.rsqrt(ms + NORM_EPS) * on_ref[...]
            y_ref[rows, ln] = (y * _silu(g_ref[rows, ln])).astype(y_ref.dtype)
        return carry

    lax.fori_loop(0, nc, finish, 0)


def _hgrn_core(proj, lb_row, onorm, s0, n_seq, seq_len, row_block0, n_heads):
    use_s0 = s0 is not None
    hd = LANES
    hps = HG_HEADS_PER_STEP * (2 if seq_len * 4 <= 1024 else 1)
    hps = math.gcd(hps, n_heads)
    nhb = n_heads // hps
    wd = hps * hd

    def col(off):
        return pl.BlockSpec((seq_len, wd), lambda s, h, off=off: (row_block0 + s, off * nhb + h))

    in_specs = [col(0), col(1), col(2), col(3), col(4),
                pl.BlockSpec((1, wd), lambda s, h: (0, h)),
                pl.BlockSpec((1, hd), lambda s, h: (0, 0)),
                pl.BlockSpec((2, CHUNK, CHUNK), lambda s, h: (0, 0, 0)),
                pl.BlockSpec((2, len(_level_halves()) + 1, CHUNK, CHUNK), lambda s, h: (0, 0, 0, 0))]
    args = [proj] * 5 + [lb_row, onorm.reshape(1, hd), _tri_pair(CHUNK), _gla_masks()]
    if use_s0:
        in_specs += [pl.BlockSpec((None, None, hps, hd, hd), lambda s, h: (s, 0, h, 0, 0)),
                     pl.BlockSpec((None, None, hps, hd, hd), lambda s, h: (s, 1, h, 0, 0))]
        args += [s0, s0]
    y, st = pl.pallas_call(
        functools.partial(_hgrn_kernel, seq_len=seq_len, use_s0=use_s0, hps=hps),
        out_shape=[jax.ShapeDtypeStruct((n_seq * seq_len, n_heads * hd), BF16),
                   jax.ShapeDtypeStruct((n_seq, 2, n_heads, hd, hd), F32)],
        grid=(n_seq, nhb),
        in_specs=in_specs,
        out_specs=[pl.BlockSpec((seq_len, wd), lambda s, h: (s, h)),
                   pl.BlockSpec((None, 2, hps, hd, hd), lambda s, h: (s, 0, h, 0, 0))],
        scratch_shapes=[pltpu.VMEM((seq_len, wd), F32), pltpu.VMEM((seq_len, wd), F32)],
        compiler_params=_cparams("parallel", "parallel"),
    )(*args)
    return y, st


def _ssd_conv_kernel(x_ref, w_ref, b_ref, o_ref, *, seq_len, n_taps):
    x = x_ref[...]
    row = lax.broadcasted_iota(jnp.int32, x.shape, 0)
    half = n_taps // 2
    acc = x * w_ref[half:half + 1, :] + b_ref[...]
    for j in range(n_taps):
        d = j - half
        if d == 0:
            continue
        shifted = pltpu.roll(x, (-d) % seq_len, 0)
        valid = (row + d >= 0) & (row + d < seq_len)
        acc = acc + jnp.where(valid, shifted, 0.0) * w_ref[j:j + 1, :]
    o_ref[...] = _silu(acc)


def _ssd_conv(proj, conv_w, conv_b, col_block0, n_seq, seq_len, row_block0):
    n_taps, c = conv_w.shape
    tc = _pick_tile(math.gcd(c, col_block0 * LANES), max(256, (512 * 1024) // seq_len))
    assert (col_block0 * LANES) % tc == 0
    cb0 = col_block0 * LANES // tc
    return pl.pallas_call(
        functools.partial(_ssd_conv_kernel, seq_len=seq_len, n_taps=n_taps),
        out_shape=jax.ShapeDtypeStruct((n_seq * seq_len, c), F32),
        grid=(n_seq, c // tc),
        in_specs=[pl.BlockSpec((seq_len, tc), lambda s, j: (row_block0 + s, cb0 + j)),
                  pl.BlockSpec((n_taps, tc), lambda s, j: (0, j)),
                  pl.BlockSpec((1, tc), lambda s, j: (0, j))],
        out_specs=pl.BlockSpec((seq_len, tc), lambda s, j: (s, j)),
        compiler_params=_cparams("parallel", "parallel"),
    )(proj, conv_w, conv_b.reshape(1, c))


def _ssd_dt_kernel(d_ref, bias_ref, a_ref, o_ref):
    x = d_ref[...] + bias_ref[...]
    dt = jnp.maximum(x, 0.0) + jnp.log1p(jnp.exp(-jnp.abs(x)))
    o_ref[0] = dt.T
    o_ref[1] = (dt * a_ref[...]).T


def _ssd_dt(proj, dt_bias, a_log, col_block, n_seq, seq_len, row_block0):
    nh2 = dt_bias.size
    assert nh2 == LANES
    a_row = (-jnp.exp(a_log.astype(F32))).reshape(1, nh2)
    return pl.pallas_call(
        _ssd_dt_kernel,
        out_shape=jax.ShapeDtypeStruct((n_seq, 2, nh2, seq_len), F32),
        grid=(n_seq,),
        in_specs=[pl.BlockSpec((seq_len, nh2), lambda s: (row_block0 + s, col_block)),
                  pl.BlockSpec((1, nh2), lambda s: (0, 0)),
                  pl.BlockSpec((1, nh2), lambda s: (0, 0))],
        out_specs=pl.BlockSpec((None, 2, nh2, seq_len), lambda s: (s, 0, 0, 0)),
        compiler_params=_cparams("parallel"),
    )(proj, dt_bias.reshape(1, nh2).astype(F32), a_row)


def _ssd_expanders(hpg, hdim):
    r = np.arange(2 * LANES)[:, None] % LANES
    head = np.arange(hpg * hdim)[None, :] // hdim
    return jnp.asarray(np.stack([r == hpg + head, r == 2 * hpg + head]).astype(np.float32), BF16)


def _ssd_decays(dta, tri, rev):
    dt_t, a_t = dta[0], dta[1]
    hpg, c = dt_t.shape
    acs_t = _chunk_cumsum_rows(a_t, tri)
    r_last = 0 if rev else c - 1
    a_last = acs_t[:, r_last:r_last + 1]
    pad = jnp.zeros((LANES - 3 * hpg, c), F32)
    cols = jnp.concatenate([acs_t, dt_t * jnp.exp(a_last - acs_t), jnp.exp(acs_t), pad], axis=0).T
    hi = cols.astype(BF16)
    lo = (cols - hi.astype(F32)).astype(BF16)
    pieces = jnp.concatenate([hi, lo], axis=1)
    return dict(acs_t=acs_t, dt_t=dt_t, cdec=jnp.exp(a_last), cols=cols, pieces=pieces)


def _ssd_chunks(chains, hpg, hdim):
    c = chains[0]["xs"].shape[0]
    pair = LANES // hdim
    ti = lax.broadcasted_iota(jnp.int32, (c, c), 0)
    si = lax.broadcasted_iota(jnp.int32, (c, c), 1)
    lane = lax.broadcasted_iota(jnp.int32, (c, LANES), 1)
    sels = [(lane >= u * hdim) & (lane < (u + 1) * hdim) for u in range(pair)]
    for ch in chains:
        ch.update(ch["dec"])
        ch["causal"] = (si >= ti) if ch["rev"] else (si <= ti)
    for ch in chains:
        cmb = ch["cm"].astype(BF16)
        ch["cb"] = _dot_nt(cmb, ch["bm"].astype(BF16))
        ch["y_inter"] = _dot(cmb, ch["h"].astype(BF16))
        ch["bt"] = ch["bm"].T.astype(BF16)
        ch["dtd"] = _dot(ch["pieces"], ch["sel"][0])
        ch["dfs"] = _dot(ch["pieces"], ch["sel"][1])
        ch["ys"] = []
        ch["hs"] = []
    for p in range(hpg // pair):
        for ch in chains:
            cols, acs_t, dt_t = ch["cols"], ch["acs_t"], ch["dt_t"]
            lanes_p = slice(p * LANES, (p + 1) * LANES)
            ms = []
            cdec_row = None
            for u in range(pair):
                hh = p * pair + u
                seg = jnp.where(ch["causal"], cols[:, hh:hh + 1] - acs_t[hh:hh + 1, :], NEG_BIG)
                ms.append((ch["cb"] * jnp.exp(seg) * dt_t[hh:hh + 1, :]).astype(BF16))
                cd = ch["cdec"][hh:hh + 1, :]
                cdec_row = cd if cdec_row is None else jnp.where(sels[u][:1, :], cd, cdec_row)
            x2 = ch["xs"][:, lanes_p]
            rhs = [jnp.where(sels[u], x2, 0.0).astype(BF16) for u in range(pair)]
            y_intra = _dot(jnp.concatenate(ms, axis=1), jnp.concatenate(rhs, axis=0))
            ch["ys"].append(y_intra + ch["y_inter"][:, lanes_p] * ch["dfs"][:, lanes_p])
            upd = _dot(ch["bt"], (x2 * ch["dtd"][:, lanes_p]).astype(BF16))
            ch["hs"].append(ch["h"][:, lanes_p] * cdec_row + upd)
    return [(jnp.concatenate(ch["ys"], axis=1), jnp.concatenate(ch["hs"], axis=1)) for ch in chains]


def _chunk_cumsum_rows(a_t, tri):
    h = a_t.shape[0]
    parts = jnp.concatenate(_split3(a_t), axis=0).astype(BF16)
    p = _dot(parts, tri)
    return (p[2 * h:] + p[h:2 * h]) + p[:h]


def _ssd_kernel(*refs, seq_len, use_h0, hpg, hdim):
    if use_h0:
        (xs_ref, bm_ref, cm_ref, z_ref, dta_ref, dsk_ref, nw_ref, tri_ref, sel_ref, h0f_ref, h0b_ref,
         y_ref, st_ref, yf_ref, yb_ref) = refs
    else:
        (xs_ref, bm_ref, cm_ref, z_ref, dta_ref, dsk_ref, nw_ref, tri_ref, sel_ref,
         y_ref, st_ref, yf_ref, yb_ref) = refs
    nc = seq_len // CHUNK
    width = hpg * hdim
    n_state = bm_ref.shape[1]

    def rows_of(ci):
        return pl.ds(pl.multiple_of(ci * CHUNK, CHUNK), CHUNK)

    def decays(ci, rev):
        return _ssd_decays(dta_ref[:, 1 if rev else 0, ci], tri_ref[0 if rev else 1], rev)

    def body(i, carry):
        h_f, h_b, dec_f, dec_b = carry
        rf = rows_of(i)
        cb = nc - 1 - i
        rb = rows_of(cb)
        nxt_f = decays(jnp.minimum(i + 1, nc - 1), False)
        nxt_b = decays(jnp.maximum(cb - 1, 0), True)
        (y_f, h_f), (y_b, h_b) = _ssd_chunks(
            [dict(xs=xs_ref[rf, :], bm=bm_ref[rf, :], cm=cm_ref[rf, :], dec=dec_f, h=h_f, sel=sel_ref,
                  rev=False),
             dict(xs=xs_ref[rb, :], bm=bm_ref[rb, :], cm=cm_ref[rb, :], dec=dec_b, h=h_b, sel=sel_ref,
                  rev=True)],
            hpg, hdim)
        yf_ref[rf, :] = y_f
        yb_ref[rb, :] = y_b
        return h_f, h_b, nxt_f, nxt_b

    def load_state(ref):
        return ref[...].reshape(width, n_state).T

    zero = jnp.zeros((n_state, width), F32)
    init = (load_state(h0f_ref), load_state(h0b_ref)) if use_h0 else (zero, zero)
    h_f, h_b, _, _ = lax.fori_loop(0, nc, body, init + (decays(0, False), decays(nc - 1, True)))
    st_ref[0] = h_f.T.reshape(hpg, hdim, n_state)
    st_ref[1] = h_b.T.reshape(hpg, hdim, n_state)

    def finish(ci, carry):
        rows = rows_of(ci)
        y = yf_ref[rows, :] + yb_ref[rows, :] + dsk_ref[...] * xs_ref[rows, :]
        y = y * _silu(z_ref[rows, :])
        ms = jnp.mean(y * y, axis=-1, keepdims=True)
        y_ref[rows, :] = (y * lax.rsqrt(ms + NORM_EPS) * nw_ref[...]).astype(y_ref.dtype)
        return carry

    lax.fori_loop(0, nc, finish, 0)


def _ssd_core(xbc, proj, dta, d_cols, norm_w, h0, n_seq, seq_len, row_block0, n_groups, n_heads,
              hdim, n_state):
    use_h0 = h0 is not None
    hpg = n_heads // n_groups
    width = hpg * hdim
    inner = n_heads * hdim
    assert n_state == LANES and width % LANES == 0 and inner % width == 0
    nc = seq_len // CHUNK
    b_blk0 = inner // n_state
    c_blk0 = b_blk0 + n_groups
    in_specs = [pl.BlockSpec((seq_len, width), lambda s, g: (s, g)),
                pl.BlockSpec((seq_len, n_state), lambda s, g: (s, b_blk0 + g)),
                pl.BlockSpec((seq_len, n_state), lambda s, g: (s, c_blk0 + g)),
                pl.BlockSpec((seq_len, width), lambda s, g: (row_block0 + s, g)),
                pl.BlockSpec((None, 2, 2, None, nc, hpg, CHUNK), lambda s, g: (s, 0, 0, g, 0, 0, 0)),
                pl.BlockSpec((1, width), lambda s, g: (0, g)),
                pl.BlockSpec((1, width), lambda s, g: (0, g)),
                pl.BlockSpec((2, CHUNK, CHUNK), lambda s, g: (0, 0, 0)),
                pl.BlockSpec((2, 2 * LANES, width), lambda s, g: (0, 0, 0))]
    args = [xbc, xbc, xbc, proj, dta, d_cols, norm_w.reshape(1, inner), _tri_pair(CHUNK),
            _ssd_expanders(hpg, hdim)]
    if use_h0:
        st_in = (None, None, hpg, hdim, n_state)
        in_specs += [pl.BlockSpec(st_in, lambda s, g: (s, 0, g, 0, 0)),
                     pl.BlockSpec(st_in, lambda s, g: (s, 1, g, 0, 0))]
        args += [h0, h0]
    y, st = pl.pallas_call(
        functools.partial(_ssd_kernel, seq_len=seq_len, use_h0=use_h0, hpg=hpg, hdim=hdim),
        out_shape=[jax.ShapeDtypeStruct((n_seq * seq_len, inner), BF16),
                   jax.ShapeDtypeStruct((n_seq, 2, n_heads, hdim, n_state), F32)],
        grid=(n_seq, n_groups),
        in_specs=in_specs,
        out_specs=[pl.BlockSpec((seq_len, width), lambda s, g: (s, g)),
                   pl.BlockSpec((None, 2, hpg, hdim, n_state), lambda s, g: (s, 0, g, 0, 0))],
        scratch_shapes=[pltpu.VMEM((seq_len, width), F32), pltpu.VMEM((seq_len, width), F32)],
        compiler_params=_cparams("parallel", "parallel"),
    )(*args)
    return y, st


def _rope_tables(rows, head_dim):
    quarter = head_dim // 4
    inv = ROPE_THETA ** (-np.arange(quarter, dtype=np.float64) / quarter)
    t = np.arange(rows.dec_seq)
    ang_r = (t // GRID_W)[:, None] * inv[None, :]
    ang_c = (t % GRID_W)[:, None] * inv[None, :]
    cos = np.concatenate([np.cos(ang_r)] * 2 + [np.cos(ang_c)] * 2, axis=1)
    sin = np.concatenate([-np.sin(ang_r), np.sin(ang_r), -np.sin(ang_c), np.sin(ang_c)], axis=1)
    n_dec = (rows.total - rows.ctx_rows) // rows.dec_seq
    cos = np.concatenate([np.ones((rows.ctx_rows, head_dim))] + [cos] * n_dec, axis=0)
    sin = np.concatenate([np.zeros((rows.ctx_rows, head_dim))] + [sin] * n_dec, axis=0)
    return jnp.asarray(cos, F32), jnp.asarray(sin, F32)


def _qk_prep_kernel(p_ref, cos_ref, sin_ref, qn_ref, kn_ref, q_ref, k_ref, *, n_q, n_kv, scale):
    cos = cos_ref[...]
    sin = sin_ref[...]
    lane = lax.broadcasted_iota(jnp.int32, cos.shape, 1)
    first = (lane % (LANES // 2)) < (LANES // 4)

    def norm_rope(x, w):
        ms = jnp.mean(x * x, axis=-1, keepdims=True)
        y = x * lax.rsqrt(ms + NORM_EPS) * w
        partner = jnp.where(first, pltpu.roll(y, LANES - LANES // 4, 1), pltpu.roll(y, LANES // 4, 1))
        return y * cos + partner * sin

    for h in range(n_q):
        x = p_ref[:, h * LANES:(h + 1) * LANES]
        q_ref[:, h * LANES:(h + 1) * LANES] = (norm_rope(x, qn_ref[...]) * scale).astype(q_ref.dtype)
    for h in range(n_kv):
        x = p_ref[:, (n_q + h) * LANES:(n_q + h + 1) * LANES]
        k_ref[:, h * LANES:(h + 1) * LANES] = norm_rope(x, kn_ref[...])


def _qk_prep(proj, cos, sin, qn, kn, n_q, n_kv, rows):
    nt = proj.shape[0]
    hd = LANES
    tm = rows.row_tile(256)
    scale = float(hd) ** -0.5
    return pl.pallas_call(
        functools.partial(_qk_prep_kernel, n_q=n_q, n_kv=n_kv, scale=scale),
        out_shape=[jax.ShapeDtypeStruct((nt, n_q * hd), BF16),
                   jax.ShapeDtypeStruct((nt, n_kv * hd), F32)],
        grid=(nt // tm,),
        in_specs=[pl.BlockSpec((tm, proj.shape[1]), lambda i: (i, 0)),
                  pl.BlockSpec((tm, hd), lambda i: (i, 0)),
                  pl.BlockSpec((tm, hd), lambda i: (i, 0)),
                  pl.BlockSpec((1, hd), lambda i: (0, 0)),
                  pl.BlockSpec((1, hd), lambda i: (0, 0))],
        out_specs=[pl.BlockSpec((tm, n_q * hd), lambda i: (i, 0)),
                   pl.BlockSpec((tm, n_kv * hd), lambda i: (i, 0))],
        compiler_params=_cparams("parallel"),
    )(proj, cos, sin, qn.reshape(1, hd), kn.reshape(1, hd))


ATTN_KEY_CHUNK = 512


def _attn_kernel(*refs, grp, use_ctx):
    if use_ctx:
        q_ref, k_ref, v_ref, kc_ref, vc_ref, o_ref = refs
    else:
        q_ref, k_ref, v_ref, o_ref = refs
    tq = q_ref.shape[0]
    q = jnp.concatenate([q_ref[:, g * LANES:(g + 1) * LANES] for g in range(grp)], axis=0)
    chunks = [(kc_ref, vc_ref, 0, kc_ref.shape[0])] if use_ctx else []
    n_keys = k_ref.shape[0]
    kc = min(ATTN_KEY_CHUNK, n_keys)
    chunks += [(k_ref, v_ref, c0, kc) for c0 in range(0, n_keys, kc)]

    def scores(chunk):
        kr, _, c0, n = chunk
        return _dot_nt(q, kr[c0:c0 + n, :].astype(BF16))

    s_next = scores(chunks[0])
    m = l = acc = None
    for idx, (_, vr, c0, n) in enumerate(chunks):
        s = s_next
        if idx + 1 < len(chunks):
            s_next = scores(chunks[idx + 1])
        mj = jnp.max(s, axis=-1, keepdims=True)
        m_new = mj if m is None else jnp.maximum(m, mj)
        p = jnp.exp(s - m_new)
        pv = _dot(p.astype(BF16), vr[c0:c0 + n, :].astype(BF16))
        pl_sum = jnp.sum(p, axis=-1, keepdims=True)
        if m is None:
            l, acc = pl_sum, pv
        else:
            alpha = jnp.exp(m - m_new)
            l = alpha * l + pl_sum
            acc = alpha * acc + pv
        m = m_new
    o = acc / l
    for g in range(grp):
        o_ref[:, g * LANES:(g + 1) * LANES] = o[g * tq:(g + 1) * tq].astype(o_ref.dtype)


def _attention(qn, kn, proj, ctx_k, ctx_v, n_seq, seq_len, row_block0, n_q, n_kv):
    use_ctx = ctx_k is not None
    grp = n_q // n_kv
    hd = LANES
    tq = min(seq_len, 128)
    nqb = seq_len // tq
    v_blk0 = n_q + n_kv
    in_specs = [pl.BlockSpec((tq, grp * hd), lambda s, kv, i: ((row_block0 + s) * nqb + i, kv)),
                pl.BlockSpec((seq_len, hd), lambda s, kv, i: (row_block0 + s, kv)),
                pl.BlockSpec((seq_len, hd), lambda s, kv, i: (row_block0 + s, v_blk0 + kv))]
    args = [qn, kn, proj]
    if use_ctx:
        past = ctx_k.shape[1]
        in_specs += [pl.BlockSpec((None, past, hd), lambda s, kv, i: (s, 0, kv)),
                     pl.BlockSpec((None, past, hd), lambda s, kv, i: (s, 0, kv))]
        args += [ctx_k, ctx_v]
    return pl.pallas_call(
        functools.partial(_attn_kernel, grp=grp, use_ctx=use_ctx),
        out_shape=jax.ShapeDtypeStruct((n_seq * seq_len, n_q * hd), BF16),
        grid=(n_seq, n_kv, nqb),
        in_specs=in_specs,
        out_specs=pl.BlockSpec((tq, grp * hd), lambda s, kv, i: (s * nqb + i, kv)),
        compiler_params=_cparams("parallel", "parallel", "arbitrary"),
    )(*args)


def kernel(x_prompt, x_sample, state_hgrn, state_ssd, cache_k, cache_v, c, c_ctx, ada_w, ada_b, norm_w,
           hg_w_in, hg_lb_logits, hg_onorm, hg_w_o, ssd_w_in, ssd_conv_w, ssd_conv_b, ssd_a_log,
           ssd_dt_bias, ssd_d, ssd_norm, ssd_w_o, at_w_qkv, at_qn, at_kn, at_w_o, ff_w13, ff_w2,
           moe_router, moe_w13, moe_w2):
    batch, seq, d = x_prompt.shape
    n_dec, dec_seq, _ = x_sample.shape
    depth = ada_w.shape[0]
    rows = _Rows(batch * seq, dec_seq, n_dec)
    ctx_rows = rows.ctx_rows
    assert seq % CHUNK == 0 and dec_seq % CHUNK == 0 and dec_seq % seq == 0 and ctx_rows % dec_seq == 0

    hg_heads = d // LANES
    ssd_heads = ssd_a_log.shape[2]
    ssd_inner = ssd_w_o.shape[1]
    ssd_hdim = ssd_inner // ssd_heads
    ssd_nstate = state_ssd.shape[-1]
    ssd_groups = (ssd_conv_w.shape[2] - ssd_inner) // (2 * ssd_nstate)
    at_kv = cache_k.shape[3]
    at_heads = at_w_o.shape[1] // LANES
    past = cache_k.shape[2]

    x = jnp.concatenate([x_prompt.reshape(ctx_rows, d), x_sample.reshape(n_dec * dec_seq, d)], axis=0)

    n_sets = 1 + n_dec
    cond = jnp.concatenate([c_ctx[None, :], c], axis=0)
    cond = jnp.pad(cond, ((0, (-n_sets) % SUBLANES), (0, 0)))
    mod = _ada_mod(cond, ada_w, ada_b)[:, :n_sets].reshape(depth, n_sets, 6, d)
    mod = jnp.pad(mod, ((0, 0), (0, 0), (0, SUBLANES - 6), (0, 0)))

    lb_all = None
    cos_t = sin_t = None
    hg_states, ssd_states, k_list, v_list = [], [], [], []
    lat_blk_seq = ctx_rows // dec_seq

    for layer in range(depth):
        j = layer // 3
        kind = layer % 3
        mod_l = mod[layer]
        nw1 = norm_w[layer, 0]
        if kind == 0:
            if lb_all is None:
                pr = jax.nn.softmax(hg_lb_logits.astype(F32), axis=0)
                lb_all = jnp.cumsum(pr, axis=0) - pr[0]
            proj = _matmul(x, nw1, mod_l, 0, hg_w_in, j, F32, rows)
            lb_row = lb_all[j].reshape(1, -1)
            y_c, st_c = _hgrn_core(proj, lb_row, hg_onorm[j], None, batch, seq, 0, hg_heads)
            y_l, _ = _hgrn_core(proj, lb_row, hg_onorm[j], state_hgrn[:, j], n_dec, dec_seq,
                                lat_blk_seq, hg_heads)
            hg_states.append(st_c)
            x = _matmul_residual(y_c, y_l, hg_w_o, j, x, mod_l, 2, rows)
        elif kind == 1:
            proj = _matmul(x, nw1, mod_l, 0, ssd_w_in, j, F32, rows)
            gn2 = 2 * ssd_groups * ssd_nstate
            xbc_blk0 = ssd_inner // LANES
            dt_blk = (2 * ssd_inner + gn2) // LANES
            d_cols = jnp.repeat((ssd_d[j, 0] + ssd_d[j, 1]).astype(F32), ssd_hdim).reshape(1, ssd_inner)
            hpg = ssd_heads // ssd_groups
            ys = []
            for (n_s, s_len, rb0, h0) in ((batch, seq, 0, None),
                                          (n_dec, dec_seq, lat_blk_seq, state_ssd[:, j])):
                xbc = _ssd_conv(proj, ssd_conv_w[j], ssd_conv_b[j], xbc_blk0, n_s, s_len, rb0)
                dta = _ssd_dt(proj, ssd_dt_bias[j], ssd_a_log[j], dt_blk, n_s, s_len, rb0)
                nc = s_len // CHUNK
                dta = dta.reshape(n_s, 2, 2, ssd_groups, hpg, nc, CHUNK).transpose(0, 1, 2, 3, 5, 4, 6)
                y_p, st_p = _ssd_core(xbc, proj, dta, d_cols, ssd_norm[j], h0, n_s, s_len, rb0,
                                      ssd_groups, ssd_heads, ssd_hdim, ssd_nstate)
                ys.append(y_p)
                if h0 is None:
                    ssd_states.append(st_p)
            x = _matmul_residual(ys[0], ys[1], ssd_w_o, j, x, mod_l, 2, rows)
        else:
            proj = _matmul(x, nw1, mod_l, 0, at_w_qkv, j, F32, rows)
            if cos_t is None:
                cos_t, sin_t = _rope_tables(rows, LANES)
            qn, kn = _qk_prep(proj, cos_t, sin_t, at_qn[j], at_kn[j], at_heads, at_kv, rows)
            o_c = _attention(qn, kn, proj, None, None, batch, seq, 0, at_heads, at_kv)
            ck = cache_k[:, j].reshape(n_dec, past, at_kv * LANES)
            cv = cache_v[:, j].reshape(n_dec, past, at_kv * LANES)
            o_l = _attention(qn, kn, proj, ck, cv, n_dec, dec_seq, lat_blk_seq, at_heads, at_kv)
            k_list.append(kn[:ctx_rows].reshape(batch, seq, at_kv, LANES))
            v_list.append(proj[:ctx_rows, (at_heads + at_kv) * LANES:].reshape(batch, seq, at_kv, LANES))
            x = _matmul_residual(o_c, o_l, at_w_o, j, x, mod_l, 2, rows)

        if layer % 2 == 0:
            h2 = _norm_mod(x, norm_w[layer, 1], mod_l, 1, rows)
            act = _matmul_swiglu(h2, ff_w13, layer // 2)
            x = _matmul_residual(act, None, ff_w2, layer // 2, x, mod_l, 5, rows)
        else:
            e = layer // 2
            n_e = moe_router.shape[2]
            tm_e = min(MOE_TILE, rows.total)
            hp, rinfo = _router(x, norm_w[layer, 1], mod_l, moe_router[e], rows)
            src, dest, tile_e, n_used = _route_tables(rinfo, n_e, tm_e)
            ys = _moe_experts(hp, moe_w13[e].astype(BF16), moe_w2[e].astype(BF16), src, tile_e,
                              n_used, tm_e)
            x = _moe_combine(ys, dest, x, rinfo, mod_l, rows)

    y_prompt = x[:ctx_rows].reshape(batch, seq, d)
    y_sample = x[ctx_rows:].reshape(n_dec, dec_seq, d)
    return (y_prompt, y_sample, jnp.stack(hg_states, axis=1), jnp.stack(ssd_states, axis=1),
            jnp.stack(k_list, axis=1), jnp.stack(v_list, axis=1))
```

```python
import functools
import math

import numpy as np
import jax
import jax.numpy as jnp
from jax import lax
from jax.experimental import pallas as pl
from jax.experimental.pallas import tpu as pltpu

F32 = jnp.float32
BF16 = jnp.bfloat16

NORM_EPS = 1e-6
GATE_FLOOR = 1e-30
ROPE_THETA = 10000.0
GRID_W = 64
LANES = 128
SUBLANES = 8
CHUNK = 128
BASE = 16
EXP_CLAMP = 80.0
NEG_BIG = -1e30
VMEM_LIMIT_BYTES = 52 * 1024 * 1024


def _cparams(*sem, vmem_limit_bytes=VMEM_LIMIT_BYTES):
    return pltpu.CompilerParams(dimension_semantics=sem, vmem_limit_bytes=vmem_limit_bytes)


def _sigmoid(x):
    return 1.0 / (1.0 + jnp.exp(-x))


def _silu(x):
    return x * _sigmoid(x)


def _pick_tile(n, cap, quantum=LANES):
    best = None
    t = quantum
    while t <= min(n, cap):
        if n % t == 0:
            best = t
        t += quantum
    assert best is not None, (n, cap)
    return best


def _dot(a, b):
    return jnp.dot(a, b, preferred_element_type=F32)


def _dot_nt(a, b):
    return lax.dot_general(a, b, (((1,), (1,)), ((), ())), preferred_element_type=F32)


def _split3(x):
    h1 = x.astype(BF16).astype(F32)
    r1 = x - h1
    h2 = r1.astype(BF16).astype(F32)
    h3 = (r1 - h2).astype(BF16).astype(F32)
    return h1, h2, h3


def _ada_kernel(c_ref, w_ref, b_ref, o_ref):
    s = _silu(c_ref[...]).astype(BF16)
    o_ref[...] = _dot(s, w_ref[...].astype(BF16)) + b_ref[...]


def _ada_mod(cond_rows, ada_w, ada_b):
    nl, d, n6 = ada_w.shape
    tn = _pick_tile(n6, 1024)
    rows = cond_rows.shape[0]
    return pl.pallas_call(
        _ada_kernel,
        out_shape=jax.ShapeDtypeStruct((nl, rows, n6), F32),
        grid=(nl, n6 // tn),
        in_specs=[pl.BlockSpec((rows, d), lambda l, j: (0, 0)),
                  pl.BlockSpec((None, d, tn), lambda l, j: (l, 0, j)),
                  pl.BlockSpec((None, 1, tn), lambda l, j: (l, 0, j))],
        out_specs=pl.BlockSpec((None, rows, tn), lambda l, j: (l, 0, j)),
        compiler_params=_cparams("parallel", "parallel"),
    )(cond_rows, ada_w, ada_b.reshape(nl, 1, n6))


def _modulated_norm(x, nw, mod, which):
    ms = jnp.mean(x * x, axis=-1, keepdims=True)
    y = x * lax.rsqrt(ms + NORM_EPS) * nw
    sh = mod[3 * which:3 * which + 1, :]
    sc = mod[3 * which + 1:3 * which + 2, :]
    return y * (1.0 + sc) + sh


class _Rows:
    def __init__(self, ctx_rows, dec_seq, n_dec):
        self.ctx_rows = ctx_rows
        self.dec_seq = dec_seq
        self.total = ctx_rows + dec_seq * n_dec
        self.tile = math.gcd(ctx_rows, dec_seq)

    def row_tile(self, cap):
        t = self.tile
        while t > cap and t % 2 == 0:
            t //= 2
        return t

    def cond_set(self, row0):
        return jnp.where(row0 < self.ctx_rows, 0, 1 + (row0 - self.ctx_rows) // self.dec_seq)


def _cache_weight(w_ref, wbf_ref):
    @pl.when(pl.program_id(1) == 0)
    def _():
        wbf_ref[...] = w_ref[...].astype(BF16)


def _normed_lhs(x_ref, nw_ref, mod_ref, which):
    return _modulated_norm(x_ref[...], nw_ref[...], mod_ref[...], which).astype(BF16)


def _norm_specs(rows, tm, k):
    return [pl.BlockSpec((tm, k), lambda j, i: (i, 0)),
            pl.BlockSpec((1, k), lambda j, i: (0, 0)),
            pl.BlockSpec((None, SUBLANES, k), lambda j, i: (rows.cond_set(i * tm), 0, 0))]


def _mm_plain_kernel(x_ref, nw_ref, mod_ref, w_ref, o_ref, wbf_ref, *, which):
    _cache_weight(w_ref, wbf_ref)
    h = _normed_lhs(x_ref, nw_ref, mod_ref, which)
    o_ref[...] = _dot(h, wbf_ref[...]).astype(o_ref.dtype)


def _matmul(x, nw, mod_l, which, w, layer, out_dtype, rows, tn_cap=1152):
    m, k = x.shape
    n = w.shape[2]
    tm = rows.row_tile(512)
    tn = _pick_tile(n, tn_cap)
    return pl.pallas_call(
        functools.partial(_mm_plain_kernel, which=which),
        out_shape=jax.ShapeDtypeStruct((m, n), out_dtype),
        grid=(n // tn, m // tm),
        in_specs=_norm_specs(rows, tm, k) + [pl.BlockSpec((None, k, tn), lambda j, i: (layer, 0, j))],
        out_specs=pl.BlockSpec((tm, tn), lambda j, i: (i, j)),
        scratch_shapes=[pltpu.VMEM((k, tn), BF16)],
        compiler_params=_cparams("parallel", "arbitrary"),
    )(x, nw.reshape(1, k), mod_l, w)


def _mm_res_kernel(*refs, gate_row, n_a):
    x_refs = refs[:-5]
    w_ref, r_ref, mod_ref, o_ref, wbf_ref = refs[-5:]
    _cache_weight(w_ref, wbf_ref)

    def finish(x_ref):
        acc = _dot(x_ref[...], wbf_ref[...])
        o_ref[...] = r_ref[...] + mod_ref[gate_row:gate_row + 1, :] * acc

    if len(x_refs) == 1:
        finish(x_refs[0])
    else:
        @pl.when(pl.program_id(1) < n_a)
        def _():
            finish(x_refs[0])

        @pl.when(pl.program_id(1) >= n_a)
        def _():
            finish(x_refs[1])


def _matmul_residual(x_ctx, x_lat, w, layer, res, mod_l, gate_row, rows):
    k = x_ctx.shape[1]
    m = rows.total
    n = w.shape[2]
    tm = rows.row_tile(512)
    tn = _pick_tile(n, 1024 if k <= 2048 else 512)
    n_a = x_ctx.shape[0] // tm
    if x_lat is None:
        xs = [x_ctx]
        x_specs = [pl.BlockSpec((tm, k), lambda j, i: (i, 0))]
    else:
        xs = [x_ctx, x_lat]
        x_specs = [pl.BlockSpec((tm, k), lambda j, i: (jnp.minimum(i, n_a - 1), 0)),
                   pl.BlockSpec((tm, k), lambda j, i: (jnp.maximum(i - n_a, 0), 0))]
    return pl.pallas_call(
        functools.partial(_mm_res_kernel, gate_row=gate_row, n_a=n_a),
        out_shape=jax.ShapeDtypeStruct((m, n), F32),
        grid=(n // tn, m // tm),
        in_specs=x_specs + [
            pl.BlockSpec((None, k, tn), lambda j, i: (layer, 0, j)),
            pl.BlockSpec((tm, tn), lambda j, i: (i, j)),
            pl.BlockSpec((None, SUBLANES, tn), lambda j, i: (rows.cond_set(i * tm), 0, j))],
        out_specs=pl.BlockSpec((tm, tn), lambda j, i: (i, j)),
        scratch_shapes=[pltpu.VMEM((k, tn), BF16)],
        compiler_params=_cparams("parallel", "arbitrary"),
    )(*xs, w, res, mod_l)


def _norm_mod_kernel(x_ref, nw_ref, mod_ref, o_ref, *, which):
    o_ref[...] = _normed_lhs(x_ref, nw_ref, mod_ref, which)


def _norm_mod(x, nw, mod_l, which, rows):
    nt, d = x.shape
    tm = rows.row_tile(256)
    return pl.pallas_call(
        functools.partial(_norm_mod_kernel, which=which),
        out_shape=jax.ShapeDtypeStruct((nt, d), BF16),
        grid=(nt // tm,),
        in_specs=[pl.BlockSpec((tm, d), lambda i: (i, 0)),
                  pl.BlockSpec((1, d), lambda i: (0, 0)),
                  pl.BlockSpec((None, SUBLANES, d), lambda i: (rows.cond_set(i * tm), 0, 0))],
        out_specs=pl.BlockSpec((tm, d), lambda i: (i, 0)),
        compiler_params=_cparams("parallel"),
    )(x, nw.reshape(1, d), mod_l)


def _mm_swiglu_kernel(x_ref, wa_ref, wb_ref, o_ref, wabf_ref, wbbf_ref):
    _cache_weight(wa_ref, wabf_ref)
    _cache_weight(wb_ref, wbbf_ref)
    x = x_ref[...]
    a = _dot(x, wabf_ref[...])
    b = _dot(x, wbbf_ref[...])
    o_ref[...] = (_silu(a) * b).astype(o_ref.dtype)


def _matmul_swiglu(x, w13, layer):
    m, k = x.shape
    f = w13.shape[2] // 2
    tm = _pick_tile(m, 512, SUBLANES)
    tn = _pick_tile(f, 512)
    nb = f // tn
    return pl.pallas_call(
        _mm_swiglu_kernel,
        out_shape=jax.ShapeDtypeStruct((m, f), BF16),
        grid=(nb, m // tm),
        in_specs=[pl.BlockSpec((tm, k), lambda j, i: (i, 0)),
                  pl.BlockSpec((None, k, tn), lambda j, i: (layer, 0, j)),
                  pl.BlockSpec((None, k, tn), lambda j, i: (layer, 0, j + nb))],
        out_specs=pl.BlockSpec((tm, tn), lambda j, i: (i, j)),
        scratch_shapes=[pltpu.VMEM((k, tn), BF16), pltpu.VMEM((k, tn), BF16)],
        compiler_params=_cparams("parallel", "arbitrary"),
    )(x, w13, w13)


MOE_TILE = 512
MOE_VMEM_LIMIT_BYTES = 60 * 1024 * 1024
HI16 = 0xFFFF0000


def _router_kernel(x_ref, nw_ref, mod_ref, r_ref, hp_ref, g_ref, *, n_experts):
    h = _modulated_norm(x_ref[...], nw_ref[...], mod_ref[...], 1)
    tm = h.shape[0]
    half = h.shape[1] // 2
    nseg = half // LANES
    bits = lax.bitcast_convert_type(h.astype(BF16).astype(F32), jnp.uint32)
    packed = (bits[:, :half] >> 16) | (bits[:, half:] & jnp.uint32(HI16))
    for j in range(nseg):
        hp_ref[pl.ds(j, tm, stride=nseg), :] = packed[:, j * LANES:(j + 1) * LANES]
    h1, h2, _ = _split3(h)
    r1, r2, _ = _split3(r_ref[...])
    h1, h2, r1, r2 = (t.astype(BF16) for t in (h1, h2, r1, r2))
    logits = _dot(h1, r1) + (_dot(h1, r2) + _dot(h2, r1))
    lane = lax.broadcasted_iota(jnp.int32, logits.shape, 1)
    logits = jnp.where(lane < n_experts, logits, NEG_BIG)
    m1 = jnp.max(logits, axis=-1, keepdims=True)
    i1 = jnp.min(jnp.where(logits == m1, lane, LANES), axis=-1, keepdims=True)
    rest = jnp.where(lane == i1, NEG_BIG, logits)
    m2 = jnp.max(rest, axis=-1, keepdims=True)
    i2 = jnp.min(jnp.where(rest == m2, lane, LANES), axis=-1, keepdims=True)
    e2 = jnp.exp(m2 - m1)
    w1 = 1.0 / (1.0 + e2)
    w2 = e2 * w1
    g_ref[...] = (jnp.where(lane == 0, w1, 0.0) + jnp.where(lane == 1, w2, 0.0)
                  + jnp.where(lane == 2, i1.astype(F32), 0.0) + jnp.where(lane == 3, i2.astype(F32), 0.0))


def _router(x, nw, mod_l, router, rows):
    nt, d = x.shape
    n_experts = router.shape[1]
    tm = rows.row_tile(256)
    nseg = d // 2 // LANES
    rpad = jnp.zeros((d, LANES), F32).at[:, :n_experts].set(router)
    return pl.pallas_call(
        functools.partial(_router_kernel, n_experts=n_experts),
        out_shape=[jax.ShapeDtypeStruct((nt * nseg, LANES), jnp.uint32),
                   jax.ShapeDtypeStruct((nt, LANES), F32)],
        grid=(nt // tm,),
        in_specs=[pl.BlockSpec((tm, d), lambda i: (i, 0)),
                  pl.BlockSpec((1, d), lambda i: (0, 0)),
                  pl.BlockSpec((None, SUBLANES, d), lambda i: (rows.cond_set(i * tm), 0, 0)),
                  pl.BlockSpec((d, LANES), lambda i: (0, 0))],
        out_specs=[pl.BlockSpec((tm * nseg, LANES), lambda i: (i, 0)),
                   pl.BlockSpec((tm, LANES), lambda i: (i, 0))],
        compiler_params=_cparams("parallel"),
    )(x, nw.reshape(1, d), mod_l, rpad)


def _route_tables(rinfo, n_experts, tm):
    nt = rinfo.shape[0]
    e_flat = jnp.concatenate([rinfo[:, 2], rinfo[:, 3]]).astype(jnp.int32)
    onehot = (e_flat[:, None] == jnp.arange(n_experts, dtype=jnp.int32)[None, :]).astype(jnp.int32)
    csum = jnp.cumsum(onehot, axis=0)
    rank = jnp.sum(csum * onehot, axis=1) - 1
    padded = ((csum[-1] + tm - 1) // tm) * tm
    ends = jnp.cumsum(padded)
    dest = jnp.sum(onehot * (ends - padded)[None, :], axis=1) + rank
    p_rows = 2 * nt + n_experts * tm
    tok = jnp.tile(jnp.arange(nt, dtype=jnp.int32), 2)
    src = jnp.zeros((p_rows,), jnp.int32).at[dest].set(tok, unique_indices=True)
    tile_start = jnp.arange(p_rows // tm, dtype=jnp.int32) * tm
    tile_e = jnp.sum((tile_start[:, None] >= ends[None, :]).astype(jnp.int32), axis=1)
    tile_e = jnp.minimum(tile_e, n_experts - 1)
    n_used = (ends[-1] // tm).astype(jnp.int32).reshape(1)
    return src, dest, tile_e, n_used


def _moe_expert_kernel(te_ref, nu_ref, src_ref, hp_ref, w13_ref, w2_ref, ys_ref, xbuf, sem, *, tm):
    del te_ref
    i = pl.program_id(0)
    n_used = nu_ref[0]
    nseg = xbuf.shape[1] // tm
    nout = ys_ref.shape[0] // tm

    def row_copy(tile, slot, r):
        s0 = pl.multiple_of(src_ref[tile * tm + r] * nseg, nseg)
        d0 = pl.multiple_of(r * nseg, nseg)
        return pltpu.make_async_copy(hp_ref.at[pl.ds(s0, nseg)], xbuf.at[slot, pl.ds(d0, nseg)],
                                     sem.at[slot])

    def issue(tile, slot):
        def body(r, carry):
            row_copy(tile, slot, r).start()
            return carry
        lax.fori_loop(0, tm, body, 0, unroll=8)

    @pl.when(i == 0)
    def _():
        issue(0, 0)

    @pl.when(i + 1 < n_used)
    def _():
        issue(i + 1, (i + 1) % 2)

    @pl.when(i < n_used)
    def _():
        slot = i % 2
        pltpu.make_async_copy(hp_ref.at[pl.ds(0, tm * nseg)], xbuf.at[slot], sem.at[slot]).wait()
        xb = xbuf.at[slot]
        segs = [xb[pl.ds(j, tm, stride=nseg), :] for j in range(nseg)]
        lo = [lax.bitcast_convert_type(u << 16, F32) for u in segs]
        hi = [lax.bitcast_convert_type(u & jnp.uint32(HI16), F32) for u in segs]
        x = jnp.concatenate(lo + hi, axis=1).astype(BF16)
        f = w2_ref.shape[0]
        a = _dot(x, w13_ref[:, :f])
        b = _dot(x, w13_ref[:, f:])
        y = _dot((_silu(a) * b).astype(BF16), w2_ref[...])
        for j in range(nout):
            ys_ref[pl.ds(j, tm, stride=nout), :] = y[:, j * LANES:(j + 1) * LANES]

    @pl.when(i >= n_used)
    def _():
        ys_ref[...] = jnp.zeros(ys_ref.shape, ys_ref.dtype)


def _moe_experts(hp, w13, w2, src, tile_e, n_used, tm):
    n_e, d, f2 = w13.shape
    f = f2 // 2
    p_rows = src.shape[0]
    nseg = d // 2 // LANES
    nout = d // LANES
    grid_spec = pltpu.PrefetchScalarGridSpec(
        num_scalar_prefetch=3,
        grid=(p_rows // tm,),
        in_specs=[pl.BlockSpec(memory_space=pl.ANY),
                  pl.BlockSpec((None, d, f2), lambda i, te, nu, sr: (te[i], 0, 0)),
                  pl.BlockSpec((None, f, d), lambda i, te, nu, sr: (te[i], 0, 0))],
        out_specs=pl.BlockSpec((tm * nout, LANES), lambda i, te, nu, sr: (i, 0)),
        scratch_shapes=[pltpu.VMEM((2, tm * nseg, LANES), jnp.uint32), pltpu.SemaphoreType.DMA((2,))])
    return pl.pallas_call(
        functools.partial(_moe_expert_kernel, tm=tm),
        out_shape=jax.ShapeDtypeStruct((p_rows * nout, LANES), F32),
        grid_spec=grid_spec,
        compiler_params=_cparams("arbitrary", vmem_limit_bytes=MOE_VMEM_LIMIT_BYTES),
    )(tile_e, n_used, src, hp, w13, w2)


def _moe_combine_kernel(dest_ref, ys_ref, x_ref, g_ref, mod_ref, o_ref, ybuf, sem, *, tm, nt, n_steps):
    i = pl.program_id(0)
    nout = ybuf.shape[2] // tm

    def row_copy(tile, slot, k, r):
        s0 = pl.multiple_of(dest_ref[k * nt + tile * tm + r] * nout, nout)
        d0 = pl.multiple_of(r * nout, nout)
        return pltpu.make_async_copy(ys_ref.at[pl.ds(s0, nout)], ybuf.at[slot, k, pl.ds(d0, nout)],
                                     sem.at[slot])

    def issue(tile, slot):
        def body(r, carry):
            row_copy(tile, slot, 0, r).start()
            row_copy(tile, slot, 1, r).start()
            return carry
        lax.fori_loop(0, tm, body, 0, unroll=4)

    @pl.when(i == 0)
    def _():
        issue(0, 0)

    @pl.when(i + 1 < n_steps)
    def _():
        issue(i + 1, (i + 1) % 2)

    slot = i % 2
    for k in range(2):
        pltpu.make_async_copy(ys_ref.at[pl.ds(0, tm * nout)], ybuf.at[slot, k], sem.at[slot]).wait()
    g = g_ref[...]
    w0 = g[:, 0:1]
    w1 = g[:, 1:2]
    y0 = ybuf.at[slot, 0]
    y1 = ybuf.at[slot, 1]
    for j in range(nout):
        cols = slice(j * LANES, (j + 1) * LANES)
        mix = w0 * y0[pl.ds(j, tm, stride=nout), :] + w1 * y1[pl.ds(j, tm, stride=nout), :]
        o_ref[:, cols] = x_ref[:, cols] + mod_ref[5:6, cols] * mix


def _moe_combine(ys, dest, x, rinfo, mod_l, rows):
    nt, d = x.shape
    tm = rows.row_tile(256)
    n_steps = nt // tm
    grid_spec = pltpu.PrefetchScalarGridSpec(
        num_scalar_prefetch=1,
        grid=(n_steps,),
        in_specs=[pl.BlockSpec(memory_space=pl.ANY),
                  pl.BlockSpec((tm, d), lambda i, ds: (i, 0)),
                  pl.BlockSpec((tm, LANES), lambda i, ds: (i, 0)),
                  pl.BlockSpec((None, SUBLANES, d), lambda i, ds: (rows.cond_set(i * tm), 0, 0))],
        out_specs=pl.BlockSpec((tm, d), lambda i, ds: (i, 0)),
        scratch_shapes=[pltpu.VMEM((2, 2, tm * (d // LANES), LANES), F32),
                        pltpu.SemaphoreType.DMA((2,))])
    return pl.pallas_call(
        functools.partial(_moe_combine_kernel, tm=tm, nt=nt, n_steps=n_steps),
        out_shape=jax.ShapeDtypeStruct((nt, d), F32),
        grid_spec=grid_spec,
        compiler_params=_cparams("arbitrary"),
    )(dest, ys, x, rinfo, mod_l)


def _tri_pair(n):
    lo = np.tril(np.ones((n, n), np.float32))
    return jnp.asarray(np.stack([lo, lo.T]), BF16)


def _level_halves():
    hs = []
    h = CHUNK // 2
    while h >= BASE:
        hs.append(h)
        h //= 2
    return hs


def _gla_masks():
    t = np.arange(CHUNK)[:, None]
    s = np.arange(CHUNK)[None, :]
    out = []
    for rev in (False, True):
        per = []
        for h in _level_halves():
            same = (t // (2 * h)) == (s // (2 * h))
            t_up = (t % (2 * h)) >= h
            s_up = (s % (2 * h)) >= h
            per.append(same & (~t_up & s_up if rev else t_up & ~s_up))
        same = (t // BASE) == (s // BASE)
        per.append(same & ((s >= t) if rev else (s <= t)))
        out.append(np.stack(per))
    return jnp.asarray(np.stack(out).astype(np.float32))


def _chunk_cumsum(tri, x):
    n = x.shape[1]
    h1 = x.astype(BF16)
    h2 = (x - h1.astype(F32)).astype(BF16)
    p = _dot(tri, jnp.concatenate([h1, h2], axis=1))
    return p[:, n:] + p[:, :n]


def _hgrn_chunks(chains):
    halves = _level_halves()
    c = chains[0]["q"].shape[0]
    for ch in chains:
        z, lb = ch["z"], ch["lb"]
        one_m_lb = 1.0 - lb
        e = jnp.exp(-jnp.abs(z))
        r = 1.0 / (1.0 + e)
        er = e * r
        pos = z >= 0
        f = lb + one_m_lb * jnp.where(pos, r, er)
        ch["logf"] = jnp.log(jnp.maximum(f, GATE_FLOOR))
        ch["key"] = one_m_lb * jnp.where(pos, er, r)
    for ch in chains:
        ch["b"] = _chunk_cumsum(ch["tri"], ch["logf"])
    for ch in chains:
        ch["scores"] = None
    for lvl, h in enumerate(halves):
        for ch in chains:
            b = ch["b"]
            b3 = b.reshape(c // (2 * h), 2 * h, LANES)
            r = h if ch["rev"] else h - 1
            w = jnp.exp(-jnp.abs(b3 - b3[:, r:r + 1, :])).reshape(c, LANES)
            part = _dot_nt((ch["q"] * w).astype(BF16), (ch["key"] * w).astype(BF16)) * ch["masks"][lvl]
            ch["scores"] = part if ch["scores"] is None else ch["scores"] + part
    for ch in chains:
        b3 = ch["b"].reshape(c // BASE, BASE, LANES)
        l3 = ch["logf"].reshape(c // BASE, BASE, LANES)
        r = BASE - 1 if ch["rev"] else 0
        eq = (b3 - (b3[:, r:r + 1, :] - l3[:, r:r + 1, :])).reshape(c, LANES)
        qb = (ch["q"] * jnp.exp(eq)).astype(BF16)
        kb = (ch["key"] * jnp.exp(jnp.minimum(-eq, EXP_CLAMP))).astype(BF16)
        ch["scores"] = ch["scores"] + _dot_nt(qb, kb) * ch["masks"][len(halves)]
    out = []
    for ch in chains:
        b, st, v = ch["b"], ch["st"], ch["v"]
        o = _dot(ch["scores"].astype(BF16), v.astype(BF16)) + \
            _dot_nt((ch["q"] * jnp.exp(b)).astype(BF16), st.astype(BF16))
        r_last = 0 if ch["rev"] else c - 1
        b_last = b[r_last:r_last + 1, :]
        kt = (ch["key"] * jnp.exp(b_last - b)).astype(BF16)
        out.append((o, st * jnp.exp(b_last) + _dot(v.T.astype(BF16), kt)))
    return out


HG_HEADS_PER_STEP = 2


def _hgrn_kernel(*refs, seq_len, use_s0, hps):
    if use_s0:
        (q_ref, zf_ref, zb_ref, v_ref, g_ref, lb_ref, on_ref, tri_ref, msk_ref, s0f_ref, s0b_ref,
         y_ref, st_ref, of_ref, ob_ref) = refs
    else:
        (q_ref, zf_ref, zb_ref, v_ref, g_ref, lb_ref, on_ref, tri_ref, msk_ref,
         y_ref, st_ref, of_ref, ob_ref) = refs
    nc = seq_len // CHUNK
    lanes = [slice(hh * LANES, (hh + 1) * LANES) for hh in range(hps)]

    def rows_of(ci):
        return pl.ds(pl.multiple_of(ci * CHUNK, CHUNK), CHUNK)

    def body(i, sts):
        rf = rows_of(i)
        rb = rows_of(nc - 1 - i)
        chains = []
        for hh, ln in enumerate(lanes):
            chains.append(dict(q=q_ref[rf, ln], z=zf_ref[rf, ln], v=v_ref[rf, ln], lb=lb_ref[:, ln],
                               tri=tri_ref[0], masks=msk_ref.at[0], st=sts[hh], rev=False))
        for hh, ln in enumerate(lanes):
            chains.append(dict(q=q_ref[rb, ln], z=zb_ref[rb, ln], v=v_ref[rb, ln], lb=lb_ref[:, ln],
                               tri=tri_ref[1], masks=msk_ref.at[1], st=sts[hps + hh], rev=True))
        res = _hgrn_chunks(chains)
        for hh, ln in enumerate(lanes):
            of_ref[rf, ln] = res[hh][0]
            ob_ref[rb, ln] = res[hps + hh][0]
        return tuple(st for _, st in res)

    zero = jnp.zeros((LANES, LANES), F32)
    init = tuple((s0f_ref[hh].T if use_s0 else zero) for hh in range(hps)) + \
        tuple((s0b_ref[hh].T if use_s0 else zero) for hh in range(hps))
    sts = lax.fori_loop(0, nc, body, init)
    for hh in range(hps):
        st_ref[0, hh] = sts[hh].T
        st_ref[1, hh] = sts[hps + hh].T

    def finish(ci, carry):
        rows = rows_of(ci)
        for ln in lanes:
            o = of_ref[rows, ln] + ob_ref[rows, ln]
            ms = jnp.mean(o * o, axis=-1, keepdims=True)
            y = o * lax.rsqrt(ms + NORM_EPS) * on_ref[...]
            y_ref[rows, ln] = (y * _silu(g_ref[rows, ln])).astype(y_ref.dtype)
        return carry

    lax.fori_loop(0, nc, finish, 0)


def _hgrn_core(proj, lb_row, onorm, s0, n_seq, seq_len, row_block0, n_heads):
    use_s0 = s0 is not None
    hd = LANES
    hps = HG_HEADS_PER_STEP * (2 if seq_len * 4 <= 1024 else 1)
    hps = math.gcd(hps, n_heads)
    nhb = n_heads // hps
    wd = hps * hd

    def col(off):
        return pl.BlockSpec((seq_len, wd), lambda s, h, off=off: (row_block0 + s, off * nhb + h))

    in_specs = [col(0), col(1), col(2), col(3), col(4),
                pl.BlockSpec((1, wd), lambda s, h: (0, h)),
                pl.BlockSpec((1, hd), lambda s, h: (0, 0)),
                pl.BlockSpec((2, CHUNK, CHUNK), lambda s, h: (0, 0, 0)),
                pl.BlockSpec((2, len(_level_halves()) + 1, CHUNK, CHUNK), lambda s, h: (0, 0, 0, 0))]
    args = [proj] * 5 + [lb_row, onorm.reshape(1, hd), _tri_pair(CHUNK), _gla_masks()]
    if use_s0:
        in_specs += [pl.BlockSpec((None, None, hps, hd, hd), lambda s, h: (s, 0, h, 0, 0)),
                     pl.BlockSpec((None, None, hps, hd, hd), lambda s, h: (s, 1, h, 0, 0))]
        args += [s0, s0]
    y, st = pl.pallas_call(
        functools.partial(_hgrn_kernel, seq_len=seq_len, use_s0=use_s0, hps=hps),
        out_shape=[jax.ShapeDtypeStruct((n_seq * seq_len, n_heads * hd), BF16),
                   jax.ShapeDtypeStruct((n_seq, 2, n_heads, hd, hd), F32)],
        grid=(n_seq, nhb),
        in_specs=in_specs,
        out_specs=[pl.BlockSpec((seq_len, wd), lambda s, h: (s, h)),
                   pl.BlockSpec((None, 2, hps, hd, hd), lambda s, h: (s, 0, h, 0, 0))],
        scratch_shapes=[pltpu.VMEM((seq_len, wd), F32), pltpu.VMEM((seq_len, wd), F32)],
        compiler_params=_cparams("parallel", "parallel"),
    )(*args)
    return y, st


def _ssd_conv_kernel(x_ref, w_ref, b_ref, o_ref, *, seq_len, n_taps):
    x = x_ref[...]
    row = lax.broadcasted_iota(jnp.int32, x.shape, 0)
    half = n_taps // 2
    acc = x * w_ref[half:half + 1, :] + b_ref[...]
    for j in range(n_taps):
        d = j - half
        if d == 0:
            continue
        shifted = pltpu.roll(x, (-d) % seq_len, 0)
        valid = (row + d >= 0) & (row + d < seq_len)
        acc = acc + jnp.where(valid, shifted, 0.0) * w_ref[j:j + 1, :]
    o_ref[...] = _silu(acc)


def _ssd_conv(proj, conv_w, conv_b, col_block0, n_seq, seq_len, row_block0):
    n_taps, c = conv_w.shape
    tc = _pick_tile(math.gcd(c, col_block0 * LANES), max(256, (512 * 1024) // seq_len))
    assert (col_block0 * LANES) % tc == 0
    cb0 = col_block0 * LANES // tc
    return pl.pallas_call(
        functools.partial(_ssd_conv_kernel, seq_len=seq_len, n_taps=n_taps),
        out_shape=jax.ShapeDtypeStruct((n_seq * seq_len, c), F32),
        grid=(n_seq, c // tc),
        in_specs=[pl.BlockSpec((seq_len, tc), lambda s, j: (row_block0 + s, cb0 + j)),
                  pl.BlockSpec((n_taps, tc), lambda s, j: (0, j)),
                  pl.BlockSpec((1, tc), lambda s, j: (0, j))],
        out_specs=pl.BlockSpec((seq_len, tc), lambda s, j: (s, j)),
        compiler_params=_cparams("parallel", "parallel"),
    )(proj, conv_w, conv_b.reshape(1, c))


def _ssd_dt_kernel(d_ref, bias_ref, a_ref, o_ref):
    x = d_ref[...] + bias_ref[...]
    dt = jnp.maximum(x, 0.0) + jnp.log1p(jnp.exp(-jnp.abs(x)))
    o_ref[0] = dt.T
    o_ref[1] = (dt * a_ref[...]).T


def _ssd_dt(proj, dt_bias, a_log, col_block, n_seq, seq_len, row_block0):
    nh2 = dt_bias.size
    assert nh2 == LANES
    a_row = (-jnp.exp(a_log.astype(F32))).reshape(1, nh2)
    return pl.pallas_call(
        _ssd_dt_kernel,
        out_shape=jax.ShapeDtypeStruct((n_seq, 2, nh2, seq_len), F32),
        grid=(n_seq,),
        in_specs=[pl.BlockSpec((seq_len, nh2), lambda s: (row_block0 + s, col_block)),
                  pl.BlockSpec((1, nh2), lambda s: (0, 0)),
                  pl.BlockSpec((1, nh2), lambda s: (0, 0))],
        out_specs=pl.BlockSpec((None, 2, nh2, seq_len), lambda s: (s, 0, 0, 0)),
        compiler_params=_cparams("parallel"),
    )(proj, dt_bias.reshape(1, nh2).astype(F32), a_row)


def _ssd_expanders(hpg, hdim):
    r = np.arange(2 * LANES)[:, None] % LANES
    head = np.arange(hpg * hdim)[None, :] // hdim
    return jnp.asarray(np.stack([r == hpg + head, r == 2 * hpg + head]).astype(np.float32), BF16)


def _ssd_decays(dta, tri, rev):
    dt_t, a_t = dta[0], dta[1]
    hpg, c = dt_t.shape
    acs_t = _chunk_cumsum_rows(a_t, tri)
    r_last = 0 if rev else c - 1
    a_last = acs_t[:, r_last:r_last + 1]
    pad = jnp.zeros((LANES - 3 * hpg, c), F32)
    cols = jnp.concatenate([acs_t, dt_t * jnp.exp(a_last - acs_t), jnp.exp(acs_t), pad], axis=0).T
    hi = cols.astype(BF16)
    lo = (cols - hi.astype(F32)).astype(BF16)
    pieces = jnp.concatenate([hi, lo], axis=1)
    return dict(acs_t=acs_t, dt_t=dt_t, cdec=jnp.exp(a_last), cols=cols, pieces=pieces)


def _ssd_chunks(chains, hpg, hdim):
    c = chains[0]["xs"].shape[0]
    pair = LANES // hdim
    ti = lax.broadcasted_iota(jnp.int32, (c, c), 0)
    si = lax.broadcasted_iota(jnp.int32, (c, c), 1)
    lane = lax.broadcasted_iota(jnp.int32, (c, LANES), 1)
    sels = [(lane >= u * hdim) & (lane < (u + 1) * hdim) for u in range(pair)]
    for ch in chains:
        ch.update(ch["dec"])
        ch["causal"] = (si >= ti) if ch["rev"] else (si <= ti)
    for ch in chains:
        cmb = ch["cm"].astype(BF16)
        ch["cb"] = _dot_nt(cmb, ch["bm"].astype(BF16))
        ch["y_inter"] = _dot(cmb, ch["h"].astype(BF16))
        ch["bt"] = ch["bm"].T.astype(BF16)
        ch["dtd"] = _dot(ch["pieces"], ch["sel"][0])
        ch["dfs"] = _dot(ch["pieces"], ch["sel"][1])
        ch["ys"] = []
        ch["hs"] = []
    for p in range(hpg // pair):
        for ch in chains:
            cols, acs_t, dt_t = ch["cols"], ch["acs_t"], ch["dt_t"]
            lanes_p = slice(p * LANES, (p + 1) * LANES)
            ms = []
            cdec_row = None
            for u in range(pair):
                hh = p * pair + u
                seg = jnp.where(ch["causal"], cols[:, hh:hh + 1] - acs_t[hh:hh + 1, :], NEG_BIG)
                ms.append((ch["cb"] * jnp.exp(seg) * dt_t[hh:hh + 1, :]).astype(BF16))
                cd = ch["cdec"][hh:hh + 1, :]
                cdec_row = cd if cdec_row is None else jnp.where(sels[u][:1, :], cd, cdec_row)
            x2 = ch["xs"][:, lanes_p]
            rhs = [jnp.where(sels[u], x2, 0.0).astype(BF16) for u in range(pair)]
            y_intra = _dot(jnp.concatenate(ms, axis=1), jnp.concatenate(rhs, axis=0))
            ch["ys"].append(y_intra + ch["y_inter"][:, lanes_p] * ch["dfs"][:, lanes_p])
            upd = _dot(ch["bt"], (x2 * ch["dtd"][:, lanes_p]).astype(BF16))
            ch["hs"].append(ch["h"][:, lanes_p] * cdec_row + upd)
    return [(jnp.concatenate(ch["ys"], axis=1), jnp.concatenate(ch["hs"], axis=1)) for ch in chains]


def _chunk_cumsum_rows(a_t, tri):
    h = a_t.shape[0]
    parts = jnp.concatenate(_split3(a_t), axis=0).astype(BF16)
    p = _dot(parts, tri)
    return (p[2 * h:] + p[h:2 * h]) + p[:h]


def _ssd_kernel(*refs, seq_len, use_h0, hpg, hdim):
    if use_h0:
        (xs_ref, bm_ref, cm_ref, z_ref, dta_ref, dsk_ref, nw_ref, tri_ref, sel_ref, h0f_ref, h0b_ref,
         y_ref, st_ref, yf_ref, yb_ref) = refs
    else:
        (xs_ref, bm_ref, cm_ref, z_ref, dta_ref, dsk_ref, nw_ref, tri_ref, sel_ref,
         y_ref, st_ref, yf_ref, yb_ref) = refs
    nc = seq_len // CHUNK
    width = hpg * hdim
    n_state = bm_ref.shape[1]

    def rows_of(ci):
        return pl.ds(pl.multiple_of(ci * CHUNK, CHUNK), CHUNK)

    def decays(ci, rev):
        return _ssd_decays(dta_ref[:, 1 if rev else 0, ci], tri_ref[0 if rev else 1], rev)

    def body(i, carry):
        h_f, h_b, dec_f, dec_b = carry
        rf = rows_of(i)
        cb = nc - 1 - i
        rb = rows_of(cb)
        nxt_f = decays(jnp.minimum(i + 1, nc - 1), False)
        nxt_b = decays(jnp.maximum(cb - 1, 0), True)
        (y_f, h_f), (y_b, h_b) = _ssd_chunks(
            [dict(xs=xs_ref[rf, :], bm=bm_ref[rf, :], cm=cm_ref[rf, :], dec=dec_f, h=h_f, sel=sel_ref,
                  rev=False),
             dict(xs=xs_ref[rb, :], bm=bm_ref[rb, :], cm=cm_ref[rb, :], dec=dec_b, h=h_b, sel=sel_ref,
                  rev=True)],
            hpg, hdim)
        yf_ref[rf, :] = y_f
        yb_ref[rb, :] = y_b
        return h_f, h_b, nxt_f, nxt_b

    def load_state(ref):
        return ref[...].reshape(width, n_state).T

    zero = jnp.zeros((n_state, width), F32)
    init = (load_state(h0f_ref), load_state(h0b_ref)) if use_h0 else (zero, zero)
    h_f, h_b, _, _ = lax.fori_loop(0, nc, body, init + (decays(0, False), decays(nc - 1, True)))
    st_ref[0] = h_f.T.reshape(hpg, hdim, n_state)
    st_ref[1] = h_b.T.reshape(hpg, hdim, n_state)

    def finish(ci, carry):
        rows = rows_of(ci)
        y = yf_ref[rows, :] + yb_ref[rows, :] + dsk_ref[...] * xs_ref[rows, :]
        y = y * _silu(z_ref[rows, :])
        ms = jnp.mean(y * y, axis=-1, keepdims=True)
        y_ref[rows, :] = (y * lax.rsqrt(ms + NORM_EPS) * nw_ref[...]).astype(y_ref.dtype)
        return carry

    lax.fori_loop(0, nc, finish, 0)


def _ssd_core(xbc, proj, dta, d_cols, norm_w, h0, n_seq, seq_len, row_block0, n_groups, n_heads,
              hdim, n_state):
    use_h0 = h0 is not None
    hpg = n_heads // n_groups
    width = hpg * hdim
    inner = n_heads * hdim
    assert n_state == LANES and width % LANES == 0 and inner % width == 0
    nc = seq_len // CHUNK
    b_blk0 = inner // n_state
    c_blk0 = b_blk0 + n_groups
    in_specs = [pl.BlockSpec((seq_len, width), lambda s, g: (s, g)),
                pl.BlockSpec((seq_len, n_state), lambda s, g: (s, b_blk0 + g)),
                pl.BlockSpec((seq_len, n_state), lambda s, g: (s, c_blk0 + g)),
                pl.BlockSpec((seq_len, width), lambda s, g: (row_block0 + s, g)),
                pl.BlockSpec((None, 2, 2, None, nc, hpg, CHUNK), lambda s, g: (s, 0, 0, g, 0, 0, 0)),
                pl.BlockSpec((1, width), lambda s, g: (0, g)),
                pl.BlockSpec((1, width), lambda s, g: (0, g)),
                pl.BlockSpec((2, CHUNK, CHUNK), lambda s, g: (0, 0, 0)),
                pl.BlockSpec((2, 2 * LANES, width), lambda s, g: (0, 0, 0))]
    args = [xbc, xbc, xbc, proj, dta, d_cols, norm_w.reshape(1, inner), _tri_pair(CHUNK),
            _ssd_expanders(hpg, hdim)]
    if use_h0:
        st_in = (None, None, hpg, hdim, n_state)
        in_specs += [pl.BlockSpec(st_in, lambda s, g: (s, 0, g, 0, 0)),
                     pl.BlockSpec(st_in, lambda s, g: (s, 1, g, 0, 0))]
        args += [h0, h0]
    y, st = pl.pallas_call(
        functools.partial(_ssd_kernel, seq_len=seq_len, use_h0=use_h0, hpg=hpg, hdim=hdim),
        out_shape=[jax.ShapeDtypeStruct((n_seq * seq_len, inner), BF16),
                   jax.ShapeDtypeStruct((n_seq, 2, n_heads, hdim, n_state), F32)],
        grid=(n_seq, n_groups),
        in_specs=in_specs,
        out_specs=[pl.BlockSpec((seq_len, width), lambda s, g: (s, g)),
                   pl.BlockSpec((None, 2, hpg, hdim, n_state), lambda s, g: (s, 0, g, 0, 0))],
        scratch_shapes=[pltpu.VMEM((seq_len, width), F32), pltpu.VMEM((seq_len, width), F32)],
        compiler_params=_cparams("parallel", "parallel"),
    )(*args)
    return y, st


def _rope_tables(rows, head_dim):
    quarter = head_dim // 4
    inv = ROPE_THETA ** (-np.arange(quarter, dtype=np.float64) / quarter)
    t = np.arange(rows.dec_seq)
    ang_r = (t // GRID_W)[:, None] * inv[None, :]
    ang_c = (t % GRID_W)[:, None] * inv[None, :]
    cos = np.concatenate([np.cos(ang_r)] * 2 + [np.cos(ang_c)] * 2, axis=1)
    sin = np.concatenate([-np.sin(ang_r), np.sin(ang_r), -np.sin(ang_c), np.sin(ang_c)], axis=1)
    n_dec = (rows.total - rows.ctx_rows) // rows.dec_seq
    cos = np.concatenate([np.ones((rows.ctx_rows, head_dim))] + [cos] * n_dec, axis=0)
    sin = np.concatenate([np.zeros((rows.ctx_rows, head_dim))] + [sin] * n_dec, axis=0)
    return jnp.asarray(cos, F32), jnp.asarray(sin, F32)


def _qk_prep_kernel(p_ref, cos_ref, sin_ref, qn_ref, kn_ref, q_ref, k_ref, *, n_q, n_kv, scale):
    cos = cos_ref[...]
    sin = sin_ref[...]
    lane = lax.broadcasted_iota(jnp.int32, cos.shape, 1)
    first = (lane % (LANES // 2)) < (LANES // 4)

    def norm_rope(x, w):
        ms = jnp.mean(x * x, axis=-1, keepdims=True)
        y = x * lax.rsqrt(ms + NORM_EPS) * w
        partner = jnp.where(first, pltpu.roll(y, LANES - LANES // 4, 1), pltpu.roll(y, LANES // 4, 1))
        return y * cos + partner * sin

    for h in range(n_q):
        x = p_ref[:, h * LANES:(h + 1) * LANES]
        q_ref[:, h * LANES:(h + 1) * LANES] = (norm_rope(x, qn_ref[...]) * scale).astype(q_ref.dtype)
    for h in range(n_kv):
        x = p_ref[:, (n_q + h) * LANES:(n_q + h + 1) * LANES]
        k_ref[:, h * LANES:(h + 1) * LANES] = norm_rope(x, kn_ref[...])


def _qk_prep(proj, cos, sin, qn, kn, n_q, n_kv, rows):
    nt = proj.shape[0]
    hd = LANES
    tm = rows.row_tile(256)
    scale = float(hd) ** -0.5
    return pl.pallas_call(
        functools.partial(_qk_prep_kernel, n_q=n_q, n_kv=n_kv, scale=scale),
        out_shape=[jax.ShapeDtypeStruct((nt, n_q * hd), BF16),
                   jax.ShapeDtypeStruct((nt, n_kv * hd), F32)],
        grid=(nt // tm,),
        in_specs=[pl.BlockSpec((tm, proj.shape[1]), lambda i: (i, 0)),
                  pl.BlockSpec((tm, hd), lambda i: (i, 0)),
                  pl.BlockSpec((tm, hd), lambda i: (i, 0)),
                  pl.BlockSpec((1, hd), lambda i: (0, 0)),
                  pl.BlockSpec((1, hd), lambda i: (0, 0))],
        out_specs=[pl.BlockSpec((tm, n_q * hd), lambda i: (i, 0)),
                   pl.BlockSpec((tm, n_kv * hd), lambda i: (i, 0))],
        compiler_params=_cparams("parallel"),
    )(proj, cos, sin, qn.reshape(1, hd), kn.reshape(1, hd))


ATTN_KEY_CHUNK = 512


def _attn_kernel(*refs, grp, use_ctx):
    if use_ctx:
        q_ref, k_ref, v_ref, kc_ref, vc_ref, o_ref = refs
    else:
        q_ref, k_ref, v_ref, o_ref = refs
    tq = q_ref.shape[0]
    q = jnp.concatenate([q_ref[:, g * LANES:(g + 1) * LANES] for g in range(grp)], axis=0)
    chunks = [(kc_ref, vc_ref, 0, kc_ref.shape[0])] if use_ctx else []
    n_keys = k_ref.shape[0]
    kc = min(ATTN_KEY_CHUNK, n_keys)
    chunks += [(k_ref, v_ref, c0, kc) for c0 in range(0, n_keys, kc)]

    def scores(chunk):
        kr, _, c0, n = chunk
        return _dot_nt(q, kr[c0:c0 + n, :].astype(BF16))

    s_next = scores(chunks[0])
    m = l = acc = None
    for idx, (_, vr, c0, n) in enumerate(chunks):
        s = s_next
        if idx + 1 < len(chunks):
            s_next = scores(chunks[idx + 1])
        mj = jnp.max(s, axis=-1, keepdims=True)
        m_new = mj if m is None else jnp.maximum(m, mj)
        p = jnp.exp(s - m_new)
        pv = _dot(p.astype(BF16), vr[c0:c0 + n, :].astype(BF16))
        pl_sum = jnp.sum(p, axis=-1, keepdims=True)
        if m is None:
            l, acc = pl_sum, pv
        else:
            alpha = jnp.exp(m - m_new)
            l = alpha * l + pl_sum
            acc = alpha * acc + pv
        m = m_new
    o = acc / l
    for g in range(grp):
        o_ref[:, g * LANES:(g + 1) * LANES] = o[g * tq:(g + 1) * tq].astype(o_ref.dtype)


def _attention(qn, kn, proj, ctx_k, ctx_v, n_seq, seq_len, row_block0, n_q, n_kv):
    use_ctx = ctx_k is not None
    grp = n_q // n_kv
    hd = LANES
    tq = seq_len if seq_len <= 256 else 128
    nqb = seq_len // tq
    v_blk0 = n_q + n_kv
    in_specs = [pl.BlockSpec((tq, grp * hd), lambda s, kv, i: ((row_block0 + s) * nqb + i, kv)),
                pl.BlockSpec((seq_len, hd), lambda s, kv, i: (row_block0 + s, kv)),
                pl.BlockSpec((seq_len, hd), lambda s, kv, i: (row_block0 + s, v_blk0 + kv))]
    args = [qn, kn, proj]
    if use_ctx:
        past = ctx_k.shape[1]
        in_specs += [pl.BlockSpec((None, past, hd), lambda s, kv, i: (s, 0, kv)),
                     pl.BlockSpec((None, past, hd), lambda s, kv, i: (s, 0, kv))]
        args += [ctx_k, ctx_v]
    return pl.pallas_call(
        functools.partial(_attn_kernel, grp=grp, use_ctx=use_ctx),
        out_shape=jax.ShapeDtypeStruct((n_seq * seq_len, n_q * hd), BF16),
        grid=(n_seq, n_kv, nqb),
        in_specs=in_specs,
        out_specs=pl.BlockSpec((tq, grp * hd), lambda s, kv, i: (s * nqb + i, kv)),
        compiler_params=_cparams("parallel", "parallel", "arbitrary"),
    )(*args)


def kernel(x_prompt, x_sample, state_hgrn, state_ssd, cache_k, cache_v, c, c_ctx, ada_w, ada_b, norm_w,
           hg_w_in, hg_lb_logits, hg_onorm, hg_w_o, ssd_w_in, ssd_conv_w, ssd_conv_b, ssd_a_log,
           ssd_dt_bias, ssd_d, ssd_norm, ssd_w_o, at_w_qkv, at_qn, at_kn, at_w_o, ff_w13, ff_w2,
           moe_router, moe_w13, moe_w2):
    batch, seq, d = x_prompt.shape
    n_dec, dec_seq, _ = x_sample.shape
    depth = ada_w.shape[0]
    rows = _Rows(batch * seq, dec_seq, n_dec)
    ctx_rows = rows.ctx_rows
    assert seq % CHUNK == 0 and dec_seq % CHUNK == 0 and dec_seq % seq == 0 and ctx_rows % dec_seq == 0

    hg_heads = d // LANES
    ssd_heads = ssd_a_log.shape[2]
    ssd_inner = ssd_w_o.shape[1]
    ssd_hdim = ssd_inner // ssd_heads
    ssd_nstate = state_ssd.shape[-1]
    ssd_groups = (ssd_conv_w.shape[2] - ssd_inner) // (2 * ssd_nstate)
    at_kv = cache_k.shape[3]
    at_heads = at_w_o.shape[1] // LANES
    past = cache_k.shape[2]

    x = jnp.concatenate([x_prompt.reshape(ctx_rows, d), x_sample.reshape(n_dec * dec_seq, d)], axis=0)

    n_sets = 1 + n_dec
    cond = jnp.concatenate([c_ctx[None, :], c], axis=0)
    cond = jnp.pad(cond, ((0, (-n_sets) % SUBLANES), (0, 0)))
    mod = _ada_mod(cond, ada_w, ada_b)[:, :n_sets].reshape(depth, n_sets, 6, d)
    mod = jnp.pad(mod, ((0, 0), (0, 0), (0, SUBLANES - 6), (0, 0)))

    lb_all = None
    cos_t = sin_t = None
    hg_states, ssd_states, k_list, v_list = [], [], [], []
    lat_blk_seq = ctx_rows // dec_seq

    for layer in range(depth):
        j = layer // 3
        kind = layer % 3
        mod_l = mod[layer]
        nw1 = norm_w[layer, 0]
        if kind == 0:
            if lb_all is None:
                pr = jax.nn.softmax(hg_lb_logits.astype(F32), axis=0)
                lb_all = jnp.cumsum(pr, axis=0) - pr[0]
            proj = _matmul(x, nw1, mod_l, 0, hg_w_in, j, F32, rows)
            lb_row = lb_all[j].reshape(1, -1)
            y_c, st_c = _hgrn_core(proj, lb_row, hg_onorm[j], None, batch, seq, 0, hg_heads)
            y_l, _ = _hgrn_core(proj, lb_row, hg_onorm[j], state_hgrn[:, j], n_dec, dec_seq,
                                lat_blk_seq, hg_heads)
            hg_states.append(st_c)
            x = _matmul_residual(y_c, y_l, hg_w_o, j, x, mod_l, 2, rows)
        elif kind == 1:
            proj = _matmul(x, nw1, mod_l, 0, ssd_w_in, j, F32, rows)
            gn2 = 2 * ssd_groups * ssd_nstate
            xbc_blk0 = ssd_inner // LANES
            dt_blk = (2 * ssd_inner + gn2) // LANES
            d_cols = jnp.repeat((ssd_d[j, 0] + ssd_d[j, 1]).astype(F32), ssd_hdim).reshape(1, ssd_inner)
            hpg = ssd_heads // ssd_groups
            ys = []
            for (n_s, s_len, rb0, h0) in ((batch, seq, 0, None),
                                          (n_dec, dec_seq, lat_blk_seq, state_ssd[:, j])):
                xbc = _ssd_conv(proj, ssd_conv_w[j], ssd_conv_b[j], xbc_blk0, n_s, s_len, rb0)
                dta = _ssd_dt(proj, ssd_dt_bias[j], ssd_a_log[j], dt_blk, n_s, s_len, rb0)
                nc = s_len // CHUNK
                dta = dta.reshape(n_s, 2, 2, ssd_groups, hpg, nc, CHUNK).transpose(0, 1, 2, 3, 5, 4, 6)
                y_p, st_p = _ssd_core(xbc, proj, dta, d_cols, ssd_norm[j], h0, n_s, s_len, rb0,
                                      ssd_groups, ssd_heads, ssd_hdim, ssd_nstate)
                ys.append(y_p)
                if h0 is None:
                    ssd_states.append(st_p)
            x = _matmul_residual(ys[0], ys[1], ssd_w_o, j, x, mod_l, 2, rows)
        else:
            proj = _matmul(x, nw1, mod_l, 0, at_w_qkv, j, F32, rows)
            if cos_t is None:
                cos_t, sin_t = _rope_tables(rows, LANES)
            qn, kn = _qk_prep(proj, cos_t, sin_t, at_qn[j], at_kn[j], at_heads, at_kv, rows)
            o_c = _attention(qn, kn, proj, None, None, batch, seq, 0, at_heads, at_kv)
            ck = cache_k[:, j].reshape(n_dec, past, at_kv * LANES)
            cv = cache_v[:, j].reshape(n_dec, past, at_kv * LANES)
            o_l = _attention(qn, kn, proj, ck, cv, n_dec, dec_seq, lat_blk_seq, at_heads, at_kv)
            k_list.append(kn[:ctx_rows].reshape(batch, seq, at_kv, LANES))
            v_list.append(proj[:ctx_rows, (at_heads + at_kv) * LANES:].reshape(batch, seq, at_kv, LANES))
            x = _matmul_residual(o_c, o_l, at_w_o, j, x, mod_l, 2, rows)

        if layer % 2 == 0:
            h2 = _norm_mod(x, norm_w[layer, 1], mod_l, 1, rows)
            act = _matmul_swiglu(h2, ff_w13, layer // 2)
            x = _matmul_residual(act, None, ff_w2, layer // 2, x, mod_l, 5, rows)
        else:
            e = layer // 2
            n_e = moe_router.shape[2]
            tm_e = min(MOE_TILE, rows.total)
            hp, rinfo = _router(x, norm_w[layer, 1], mod_l, moe_router[e], rows)
            src, dest, tile_e, n_used = _route_tables(rinfo, n_e, tm_e)
            ys = _moe_experts(hp, moe_w13[e].astype(BF16), moe_w2[e].astype(BF16), src, tile_e,
                              n_used, tm_e)
            x = _moe_combine(ys, dest, x, rinfo, mod_l, rows)

    y_prompt = x[:ctx_rows].reshape(batch, seq, d)
    y_sample = x[ctx_rows:].reshape(n_dec, dec_seq, d)
    return (y_prompt, y_sample, jnp.stack(hg_states, axis=1), jnp.stack(ssd_states, axis=1),
            jnp.stack(k_list, axis=1), jnp.stack(v_list, axis=1))
```

```python
import functools
import math

import numpy as np
import jax
import jax.numpy as jnp
from jax import lax
from jax.experimental import pallas as pl
from jax.experimental.pallas import tpu as pltpu

F32 = jnp.float32
BF16 = jnp.bfloat16

NORM_EPS = 1e-6
GATE_FLOOR = 1e-30
ROPE_THETA = 10000.0
GRID_W = 64
LANES = 128
SUBLANES = 8
CHUNK = 128
BASE = 16
EXP_CLAMP = 80.0
NEG_BIG = -1e30
VMEM_LIMIT_BYTES = 52 * 1024 * 1024


def _cparams(*sem, vmem_limit_bytes=VMEM_LIMIT_BYTES):
    return pltpu.CompilerParams(dimension_semantics=sem, vmem_limit_bytes=vmem_limit_bytes)


def _sigmoid(x):
    return 1.0 / (1.0 + jnp.exp(-x))


def _silu(x):
    return x * _sigmoid(x)


def _pick_tile(n, cap, quantum=LANES):
    best = None
    t = quantum
    while t <= min(n, cap):
        if n % t == 0:
            best = t
        t += quantum
    assert best is not None, (n, cap)
    return best


def _dot(a, b):
    return jnp.dot(a, b, preferred_element_type=F32)


def _dot_nt(a, b):
    return lax.dot_general(a, b, (((1,), (1,)), ((), ())), preferred_element_type=F32)


def _split3(x):
    h1 = x.astype(BF16).astype(F32)
    r1 = x - h1
    h2 = r1.astype(BF16).astype(F32)
    h3 = (r1 - h2).astype(BF16).astype(F32)
    return h1, h2, h3


def _ada_kernel(c_ref, w_ref, b_ref, o_ref):
    s = _silu(c_ref[...]).astype(BF16)
    o_ref[...] = _dot(s, w_ref[...].astype(BF16)) + b_ref[...]


def _ada_mod(cond_rows, ada_w, ada_b):
    nl, d, n6 = ada_w.shape
    tn = _pick_tile(n6, 1024)
    rows = cond_rows.shape[0]
    return pl.pallas_call(
        _ada_kernel,
        out_shape=jax.ShapeDtypeStruct((nl, rows, n6), F32),
        grid=(nl, n6 // tn),
        in_specs=[pl.BlockSpec((rows, d), lambda l, j: (0, 0)),
                  pl.BlockSpec((None, d, tn), lambda l, j: (l, 0, j)),
                  pl.BlockSpec((None, 1, tn), lambda l, j: (l, 0, j))],
        out_specs=pl.BlockSpec((None, rows, tn), lambda l, j: (l, 0, j)),
        compiler_params=_cparams("parallel", "parallel"),
    )(cond_rows, ada_w, ada_b.reshape(nl, 1, n6))


def _modulated_norm(x, nw, mod, which):
    ms = jnp.mean(x * x, axis=-1, keepdims=True)
    y = x * lax.rsqrt(ms + NORM_EPS) * nw
    sh = mod[3 * which:3 * which + 1, :]
    sc = mod[3 * which + 1:3 * which + 2, :]
    return y * (1.0 + sc) + sh


class _Rows:
    def __init__(self, ctx_rows, dec_seq, n_dec):
        self.ctx_rows = ctx_rows
        self.dec_seq = dec_seq
        self.total = ctx_rows + dec_seq * n_dec
        self.tile = math.gcd(ctx_rows, dec_seq)

    def row_tile(self, cap):
        t = self.tile
        while t > cap and t % 2 == 0:
            t //= 2
        return t

    def cond_set(self, row0):
        return jnp.where(row0 < self.ctx_rows, 0, 1 + (row0 - self.ctx_rows) // self.dec_seq)


def _cache_weight(w_ref, wbf_ref):
    @pl.when(pl.program_id(1) == 0)
    def _():
        wbf_ref[...] = w_ref[...].astype(BF16)


def _normed_lhs(x_ref, nw_ref, mod_ref, which):
    return _modulated_norm(x_ref[...], nw_ref[...], mod_ref[...], which).astype(BF16)


def _norm_specs(rows, tm, k):
    return [pl.BlockSpec((tm, k), lambda j, i: (i, 0)),
            pl.BlockSpec((1, k), lambda j, i: (0, 0)),
            pl.BlockSpec((None, SUBLANES, k), lambda j, i: (rows.cond_set(i * tm), 0, 0))]


def _mm_plain_kernel(x_ref, nw_ref, mod_ref, w_ref, o_ref, wbf_ref, *, which):
    _cache_weight(w_ref, wbf_ref)
    h = _normed_lhs(x_ref, nw_ref, mod_ref, which)
    o_ref[...] = _dot(h, wbf_ref[...]).astype(o_ref.dtype)


def _matmul(x, nw, mod_l, which, w, layer, out_dtype, rows, tn_cap=1152):
    m, k = x.shape
    n = w.shape[2]
    tm = rows.row_tile(512)
    tn = _pick_tile(n, tn_cap)
    return pl.pallas_call(
        functools.partial(_mm_plain_kernel, which=which),
        out_shape=jax.ShapeDtypeStruct((m, n), out_dtype),
        grid=(n // tn, m // tm),
        in_specs=_norm_specs(rows, tm, k) + [pl.BlockSpec((None, k, tn), lambda j, i: (layer, 0, j))],
        out_specs=pl.BlockSpec((tm, tn), lambda j, i: (i, j)),
        scratch_shapes=[pltpu.VMEM((k, tn), BF16)],
        compiler_params=_cparams("parallel", "arbitrary"),
    )(x, nw.reshape(1, k), mod_l, w)


def _mm_res_kernel(*refs, gate_row, n_a):
    x_refs = refs[:-5]
    w_ref, r_ref, mod_ref, o_ref, wbf_ref = refs[-5:]
    _cache_weight(w_ref, wbf_ref)

    def finish(x_ref):
        acc = _dot(x_ref[...], wbf_ref[...])
        o_ref[...] = r_ref[...] + mod_ref[gate_row:gate_row + 1, :] * acc

    if len(x_refs) == 1:
        finish(x_refs[0])
    else:
        @pl.when(pl.program_id(1) < n_a)
        def _():
            finish(x_refs[0])

        @pl.when(pl.program_id(1) >= n_a)
        def _():
            finish(x_refs[1])


def _matmul_residual(x_ctx, x_lat, w, layer, res, mod_l, gate_row, rows):
    k = x_ctx.shape[1]
    m = rows.total
    n = w.shape[2]
    tm = rows.row_tile(512)
    tn = _pick_tile(n, 1024 if k <= 2048 else 512)
    n_a = x_ctx.shape[0] // tm
    if x_lat is None:
        xs = [x_ctx]
        x_specs = [pl.BlockSpec((tm, k), lambda j, i: (i, 0))]
    else:
        xs = [x_ctx, x_lat]
        x_specs = [pl.BlockSpec((tm, k), lambda j, i: (jnp.minimum(i, n_a - 1), 0)),
                   pl.BlockSpec((tm, k), lambda j, i: (jnp.maximum(i - n_a, 0), 0))]
    return pl.pallas_call(
        functools.partial(_mm_res_kernel, gate_row=gate_row, n_a=n_a),
        out_shape=jax.ShapeDtypeStruct((m, n), F32),
        grid=(n // tn, m // tm),
        in_specs=x_specs + [
            pl.BlockSpec((None, k, tn), lambda j, i: (layer, 0, j)),
            pl.BlockSpec((tm, tn), lambda j, i: (i, j)),
            pl.BlockSpec((None, SUBLANES, tn), lambda j, i: (rows.cond_set(i * tm), 0, j))],
        out_specs=pl.BlockSpec((tm, tn), lambda j, i: (i, j)),
        scratch_shapes=[pltpu.VMEM((k, tn), BF16)],
        compiler_params=_cparams("parallel", "arbitrary"),
    )(*xs, w, res, mod_l)


def _norm_mod_kernel(x_ref, nw_ref, mod_ref, o_ref, *, which):
    o_ref[...] = _normed_lhs(x_ref, nw_ref, mod_ref, which)


def _norm_mod(x, nw, mod_l, which, rows):
    nt, d = x.shape
    tm = rows.row_tile(256)
    return pl.pallas_call(
        functools.partial(_norm_mod_kernel, which=which),
        out_shape=jax.ShapeDtypeStruct((nt, d), BF16),
        grid=(nt // tm,),
        in_specs=[pl.BlockSpec((tm, d), lambda i: (i, 0)),
                  pl.BlockSpec((1, d), lambda i: (0, 0)),
                  pl.BlockSpec((None, SUBLANES, d), lambda i: (rows.cond_set(i * tm), 0, 0))],
        out_specs=pl.BlockSpec((tm, d), lambda i: (i, 0)),
        compiler_params=_cparams("parallel"),
    )(x, nw.reshape(1, d), mod_l)


def _mm_swiglu_kernel(x_ref, wa_ref, wb_ref, o_ref, wabf_ref, wbbf_ref):
    _cache_weight(wa_ref, wabf_ref)
    _cache_weight(wb_ref, wbbf_ref)
    x = x_ref[...]
    a = _dot(x, wabf_ref[...])
    b = _dot(x, wbbf_ref[...])
    o_ref[...] = (_silu(a) * b).astype(o_ref.dtype)


def _matmul_swiglu(x, w13, layer):
    m, k = x.shape
    f = w13.shape[2] // 2
    tm = _pick_tile(m, 512, SUBLANES)
    tn = _pick_tile(f, 512)
    nb = f // tn
    return pl.pallas_call(
        _mm_swiglu_kernel,
        out_shape=jax.ShapeDtypeStruct((m, f), BF16),
        grid=(nb, m // tm),
        in_specs=[pl.BlockSpec((tm, k), lambda j, i: (i, 0)),
                  pl.BlockSpec((None, k, tn), lambda j, i: (layer, 0, j)),
                  pl.BlockSpec((None, k, tn), lambda j, i: (layer, 0, j + nb))],
        out_specs=pl.BlockSpec((tm, tn), lambda j, i: (i, j)),
        scratch_shapes=[pltpu.VMEM((k, tn), BF16), pltpu.VMEM((k, tn), BF16)],
        compiler_params=_cparams("parallel", "arbitrary"),
    )(x, w13, w13)


MOE_TILE = 512
MOE_VMEM_LIMIT_BYTES = 60 * 1024 * 1024
HI16 = 0xFFFF0000


def _router_kernel(x_ref, nw_ref, mod_ref, r_ref, hp_ref, g_ref, *, n_experts):
    h = _modulated_norm(x_ref[...], nw_ref[...], mod_ref[...], 1)
    tm = h.shape[0]
    half = h.shape[1] // 2
    nseg = half // LANES
    bits = lax.bitcast_convert_type(h.astype(BF16).astype(F32), jnp.uint32)
    packed = (bits[:, :half] >> 16) | (bits[:, half:] & jnp.uint32(HI16))
    for j in range(nseg):
        hp_ref[pl.ds(j, tm, stride=nseg), :] = packed[:, j * LANES:(j + 1) * LANES]
    h1, h2, _ = _split3(h)
    r1, r2, _ = _split3(r_ref[...])
    h1, h2, r1, r2 = (t.astype(BF16) for t in (h1, h2, r1, r2))
    logits = _dot(h1, r1) + (_dot(h1, r2) + _dot(h2, r1))
    lane = lax.broadcasted_iota(jnp.int32, logits.shape, 1)
    logits = jnp.where(lane < n_experts, logits, NEG_BIG)
    m1 = jnp.max(logits, axis=-1, keepdims=True)
    i1 = jnp.min(jnp.where(logits == m1, lane, LANES), axis=-1, keepdims=True)
    rest = jnp.where(lane == i1, NEG_BIG, logits)
    m2 = jnp.max(rest, axis=-1, keepdims=True)
    i2 = jnp.min(jnp.where(rest == m2, lane, LANES), axis=-1, keepdims=True)
    e2 = jnp.exp(m2 - m1)
    w1 = 1.0 / (1.0 + e2)
    w2 = e2 * w1
    g_ref[...] = (jnp.where(lane == 0, w1, 0.0) + jnp.where(lane == 1, w2, 0.0)
                  + jnp.where(lane == 2, i1.astype(F32), 0.0) + jnp.where(lane == 3, i2.astype(F32), 0.0))


def _router(x, nw, mod_l, router, rows):
    nt, d = x.shape
    n_experts = router.shape[1]
    tm = rows.row_tile(256)
    nseg = d // 2 // LANES
    rpad = jnp.zeros((d, LANES), F32).at[:, :n_experts].set(router)
    return pl.pallas_call(
        functools.partial(_router_kernel, n_experts=n_experts),
        out_shape=[jax.ShapeDtypeStruct((nt * nseg, LANES), jnp.uint32),
                   jax.ShapeDtypeStruct((nt, LANES), F32)],
        grid=(nt // tm,),
        in_specs=[pl.BlockSpec((tm, d), lambda i: (i, 0)),
                  pl.BlockSpec((1, d), lambda i: (0, 0)),
                  pl.BlockSpec((None, SUBLANES, d), lambda i: (rows.cond_set(i * tm), 0, 0)),
                  pl.BlockSpec((d, LANES), lambda i: (0, 0))],
        out_specs=[pl.BlockSpec((tm * nseg, LANES), lambda i: (i, 0)),
                   pl.BlockSpec((tm, LANES), lambda i: (i, 0))],
        compiler_params=_cparams("parallel"),
    )(x, nw.reshape(1, d), mod_l, rpad)


def _route_tables(rinfo, n_experts, tm):
    nt = rinfo.shape[0]
    e_flat = jnp.concatenate([rinfo[:, 2], rinfo[:, 3]]).astype(jnp.int32)
    onehot = (e_flat[:, None] == jnp.arange(n_experts, dtype=jnp.int32)[None, :]).astype(jnp.int32)
    csum = jnp.cumsum(onehot, axis=0)
    rank = jnp.sum(csum * onehot, axis=1) - 1
    padded = ((csum[-1] + tm - 1) // tm) * tm
    ends = jnp.cumsum(padded)
    dest = jnp.sum(onehot * (ends - padded)[None, :], axis=1) + rank
    p_rows = 2 * nt + n_experts * tm
    tok = jnp.tile(jnp.arange(nt, dtype=jnp.int32), 2)
    src = jnp.zeros((p_rows,), jnp.int32).at[dest].set(tok, unique_indices=True)
    tile_start = jnp.arange(p_rows // tm, dtype=jnp.int32) * tm
    tile_e = jnp.sum((tile_start[:, None] >= ends[None, :]).astype(jnp.int32), axis=1)
    tile_e = jnp.minimum(tile_e, n_experts - 1)
    n_used = (ends[-1] // tm).astype(jnp.int32).reshape(1)
    return src, dest, tile_e, n_used


def _moe_expert_kernel(te_ref, nu_ref, src_ref, hp_ref, w13_ref, w2_ref, ys_ref, xbuf, sem, *, tm):
    del te_ref
    i = pl.program_id(0)
    n_used = nu_ref[0]
    nseg = xbuf.shape[1] // tm
    nout = ys_ref.shape[0] // tm

    def row_copy(tile, slot, r):
        s0 = pl.multiple_of(src_ref[tile * tm + r] * nseg, nseg)
        d0 = pl.multiple_of(r * nseg, nseg)
        return pltpu.make_async_copy(hp_ref.at[pl.ds(s0, nseg)], xbuf.at[slot, pl.ds(d0, nseg)],
                                     sem.at[slot])

    def issue(tile, slot):
        def body(r, carry):
            row_copy(tile, slot, r).start()
            return carry
        lax.fori_loop(0, tm, body, 0, unroll=8)

    @pl.when(i == 0)
    def _():
        issue(0, 0)

    @pl.when(i + 1 < n_used)
    def _():
        issue(i + 1, (i + 1) % 2)

    @pl.when(i < n_used)
    def _():
        slot = i % 2
        pltpu.make_async_copy(hp_ref.at[pl.ds(0, tm * nseg)], xbuf.at[slot], sem.at[slot]).wait()
        xb = xbuf.at[slot]
        segs = [xb[pl.ds(j, tm, stride=nseg), :] for j in range(nseg)]
        lo = [lax.bitcast_convert_type(u << 16, F32) for u in segs]
        hi = [lax.bitcast_convert_type(u & jnp.uint32(HI16), F32) for u in segs]
        x = jnp.concatenate(lo + hi, axis=1).astype(BF16)
        f = w2_ref.shape[0]
        a = _dot(x, w13_ref[:, :f])
        b = _dot(x, w13_ref[:, f:])
        y = _dot((_silu(a) * b).astype(BF16), w2_ref[...])
        for j in range(nout):
            ys_ref[pl.ds(j, tm, stride=nout), :] = y[:, j * LANES:(j + 1) * LANES]

    @pl.when(i >= n_used)
    def _():
        ys_ref[...] = jnp.zeros(ys_ref.shape, ys_ref.dtype)


def _moe_experts(hp, w13, w2, src, tile_e, n_used, tm):
    n_e, d, f2 = w13.shape
    f = f2 // 2
    p_rows = src.shape[0]
    nseg = d // 2 // LANES
    nout = d // LANES
    grid_spec = pltpu.PrefetchScalarGridSpec(
        num_scalar_prefetch=3,
        grid=(p_rows // tm,),
        in_specs=[pl.BlockSpec(memory_space=pl.ANY),
                  pl.BlockSpec((None, d, f2), lambda i, te, nu, sr: (te[i], 0, 0)),
                  pl.BlockSpec((None, f, d), lambda i, te, nu, sr: (te[i], 0, 0))],
        out_specs=pl.BlockSpec((tm * nout, LANES), lambda i, te, nu, sr: (i, 0)),
        scratch_shapes=[pltpu.VMEM((2, tm * nseg, LANES), jnp.uint32), pltpu.SemaphoreType.DMA((2,))])
    return pl.pallas_call(
        functools.partial(_moe_expert_kernel, tm=tm),
        out_shape=jax.ShapeDtypeStruct((p_rows * nout, LANES), F32),
        grid_spec=grid_spec,
        compiler_params=_cparams("arbitrary", vmem_limit_bytes=MOE_VMEM_LIMIT_BYTES),
    )(tile_e, n_used, src, hp, w13, w2)


def _moe_combine_kernel(dest_ref, ys_ref, x_ref, g_ref, mod_ref, o_ref, ybuf, sem, *, tm, nt, n_steps):
    i = pl.program_id(0)
    nout = ybuf.shape[2] // tm

    def row_copy(tile, slot, k, r):
        s0 = pl.multiple_of(dest_ref[k * nt + tile * tm + r] * nout, nout)
        d0 = pl.multiple_of(r * nout, nout)
        return pltpu.make_async_copy(ys_ref.at[pl.ds(s0, nout)], ybuf.at[slot, k, pl.ds(d0, nout)],
                                     sem.at[slot])

    def issue(tile, slot):
        def body(r, carry):
            row_copy(tile, slot, 0, r).start()
            row_copy(tile, slot, 1, r).start()
            return carry
        lax.fori_loop(0, tm, body, 0, unroll=4)

    @pl.when(i == 0)
    def _():
        issue(0, 0)

    @pl.when(i + 1 < n_steps)
    def _():
        issue(i + 1, (i + 1) % 2)

    slot = i % 2
    for k in range(2):
        pltpu.make_async_copy(ys_ref.at[pl.ds(0, tm * nout)], ybuf.at[slot, k], sem.at[slot]).wait()
    g = g_ref[...]
    w0 = g[:, 0:1]
    w1 = g[:, 1:2]
    y0 = ybuf.at[slot, 0]
    y1 = ybuf.at[slot, 1]
    for j in range(nout):
        cols = slice(j * LANES, (j + 1) * LANES)
        mix = w0 * y0[pl.ds(j, tm, stride=nout), :] + w1 * y1[pl.ds(j, tm, stride=nout), :]
        o_ref[:, cols] = x_ref[:, cols] + mod_ref[5:6, cols] * mix


def _moe_combine(ys, dest, x, rinfo, mod_l, rows):
    nt, d = x.shape
    tm = rows.row_tile(256)
    n_steps = nt // tm
    grid_spec = pltpu.PrefetchScalarGridSpec(
        num_scalar_prefetch=1,
        grid=(n_steps,),
        in_specs=[pl.BlockSpec(memory_space=pl.ANY),
                  pl.BlockSpec((tm, d), lambda i, ds: (i, 0)),
                  pl.BlockSpec((tm, LANES), lambda i, ds: (i, 0)),
                  pl.BlockSpec((None, SUBLANES, d), lambda i, ds: (rows.cond_set(i * tm), 0, 0))],
        out_specs=pl.BlockSpec((tm, d), lambda i, ds: (i, 0)),
        scratch_shapes=[pltpu.VMEM((2, 2, tm * (d // LANES), LANES), F32),
                        pltpu.SemaphoreType.DMA((2,))])
    return pl.pallas_call(
        functools.partial(_moe_combine_kernel, tm=tm, nt=nt, n_steps=n_steps),
        out_shape=jax.ShapeDtypeStruct((nt, d), F32),
        grid_spec=grid_spec,
        compiler_params=_cparams("arbitrary"),
    )(dest, ys, x, rinfo, mod_l)


def _tri_pair(n):
    lo = np.tril(np.ones((n, n), np.float32))
    return jnp.asarray(np.stack([lo, lo.T]), BF16)


def _level_halves():
    hs = []
    h = CHUNK // 2
    while h >= BASE:
        hs.append(h)
        h //= 2
    return hs


def _gla_masks():
    t = np.arange(CHUNK)[:, None]
    s = np.arange(CHUNK)[None, :]
    out = []
    for rev in (False, True):
        per = []
        for h in _level_halves():
            same = (t // (2 * h)) == (s // (2 * h))
            t_up = (t % (2 * h)) >= h
            s_up = (s % (2 * h)) >= h
            per.append(same & (~t_up & s_up if rev else t_up & ~s_up))
        same = (t // BASE) == (s // BASE)
        per.append(same & ((s >= t) if rev else (s <= t)))
        out.append(np.stack(per))
    return jnp.asarray(np.stack(out).astype(np.float32))


def _chunk_cumsum(tri, x):
    n = x.shape[1]
    h1 = x.astype(BF16)
    h2 = (x - h1.astype(F32)).astype(BF16)
    p = _dot(tri, jnp.concatenate([h1, h2], axis=1))
    return p[:, n:] + p[:, :n]


def _hgrn_chunks(chains):
    halves = _level_halves()
    c = chains[0]["q"].shape[0]
    for ch in chains:
        z, lb = ch["z"], ch["lb"]
        one_m_lb = 1.0 - lb
        e = jnp.exp(-jnp.abs(z))
        r = 1.0 / (1.0 + e)
        er = e * r
        pos = z >= 0
        f = lb + one_m_lb * jnp.where(pos, r, er)
        ch["logf"] = jnp.log(jnp.maximum(f, GATE_FLOOR))
        ch["key"] = one_m_lb * jnp.where(pos, er, r)
    for ch in chains:
        ch["b"] = _chunk_cumsum(ch["tri"], ch["logf"])
    for ch in chains:
        ch["scores"] = None
    for lvl, h in enumerate(halves):
        for ch in chains:
            b = ch["b"]
            b3 = b.reshape(c // (2 * h), 2 * h, LANES)
            r = h if ch["rev"] else h - 1
            w = jnp.exp(-jnp.abs(b3 - b3[:, r:r + 1, :])).reshape(c, LANES)
            part = _dot_nt((ch["q"] * w).astype(BF16), (ch["key"] * w).astype(BF16)) * ch["masks"][lvl]
            ch["scores"] = part if ch["scores"] is None else ch["scores"] + part
    for ch in chains:
        b3 = ch["b"].reshape(c // BASE, BASE, LANES)
        l3 = ch["logf"].reshape(c // BASE, BASE, LANES)
        r = BASE - 1 if ch["rev"] else 0
        eq = (b3 - (b3[:, r:r + 1, :] - l3[:, r:r + 1, :])).reshape(c, LANES)
        qb = (ch["q"] * jnp.exp(eq)).astype(BF16)
        kb = (ch["key"] * jnp.exp(jnp.minimum(-eq, EXP_CLAMP))).astype(BF16)
        ch["scores"] = ch["scores"] + _dot_nt(qb, kb) * ch["masks"][len(halves)]
    out = []
    for ch in chains:
        b, st, v = ch["b"], ch["st"], ch["v"]
        o = _dot(ch["scores"].astype(BF16), v.astype(BF16)) + \
            _dot_nt((ch["q"] * jnp.exp(b)).astype(BF16), st.astype(BF16))
        r_last = 0 if ch["rev"] else c - 1
        b_last = b[r_last:r_last + 1, :]
        kt = (ch["key"] * jnp.exp(b_last - b)).astype(BF16)
        out.append((o, st * jnp.exp(b_last) + _dot(v.T.astype(BF16), kt)))
    return out


HG_HEADS_PER_STEP = 2


def _hgrn_kernel(*refs, seq_len, use_s0, hps):
    if use_s0:
        (q_ref, zf_ref, zb_ref, v_ref, g_ref, lb_ref, on_ref, tri_ref, msk_ref, s0f_ref, s0b_ref,
         y_ref, st_ref, of_ref, ob_ref) = refs
    else:
        (q_ref, zf_ref, zb_ref, v_ref, g_ref, lb_ref, on_ref, tri_ref, msk_ref,
         y_ref, st_ref, of_ref, ob_ref) = refs
    nc = seq_len // CHUNK
    lanes = [slice(hh * LANES, (hh + 1) * LANES) for hh in range(hps)]

    def rows_of(ci):
        return pl.ds(pl.multiple_of(ci * CHUNK, CHUNK), CHUNK)

    def body(i, sts):
        rf = rows_of(i)
        rb = rows_of(nc - 1 - i)
        chains = []
        for hh, ln in enumerate(lanes):
            chains.append(dict(q=q_ref[rf, ln], z=zf_ref[rf, ln], v=v_ref[rf, ln], lb=lb_ref[:, ln],
                               tri=tri_ref[0], masks=msk_ref.at[0], st=sts[hh], rev=False))
        for hh, ln in enumerate(lanes):
            chains.append(dict(q=q_ref[rb, ln], z=zb_ref[rb, ln], v=v_ref[rb, ln], lb=lb_ref[:, ln],
                               tri=tri_ref[1], masks=msk_ref.at[1], st=sts[hps + hh], rev=True))
        res = _hgrn_chunks(chains)
        for hh, ln in enumerate(lanes):
            of_ref[rf, ln] = res[hh][0]
            ob_ref[rb, ln] = res[hps + hh][0]
        return tuple(st for _, st in res)

    zero = jnp.zeros((LANES, LANES), F32)
    init = tuple((s0f_ref[hh].T if use_s0 else zero) for hh in range(hps)) + \
        tuple((s0b_ref[hh].T if use_s0 else zero) for hh in range(hps))
    sts = lax.fori_loop(0, nc, body, init)
    for hh in range(hps):
        st_ref[0, hh] = sts[hh].T
        st_ref[1, hh] = sts[hps + hh].T

    def finish(ci, carry):
        rows = rows_of(ci)
        for ln in lanes:
            o = of_ref[rows, ln] + ob_ref[rows, ln]
            ms = jnp.mean(o * o, axis=-1, keepdims=True)
            y = o * lax.rsqrt(ms + NORM_EPS) * on_ref[...]
            y_ref[rows, ln] = (y * _silu(g_ref[rows, ln])).astype(y_ref.dtype)
        return carry

    lax.fori_loop(0, nc, finish, 0)


def _hgrn_core(proj, lb_row, onorm, s0, n_seq, seq_len, row_block0, n_heads):
    use_s0 = s0 is not None
    hd = LANES
    hps = HG_HEADS_PER_STEP * (2 if seq_len * 4 <= 1024 else 1)
    hps = math.gcd(hps, n_heads)
    nhb = n_heads // hps
    wd = hps * hd

    def col(off):
        return pl.BlockSpec((seq_len, wd), lambda s, h, off=off: (row_block0 + s, off * nhb + h))

    in_specs = [col(0), col(1), col(2), col(3), col(4),
                pl.BlockSpec((1, wd), lambda s, h: (0, h)),
                pl.BlockSpec((1, hd), lambda s, h: (0, 0)),
                pl.BlockSpec((2, CHUNK, CHUNK), lambda s, h: (0, 0, 0)),
                pl.BlockSpec((2, len(_level_halves()) + 1, CHUNK, CHUNK), lambda s, h: (0, 0, 0, 0))]
    args = [proj] * 5 + [lb_row, onorm.reshape(1, hd), _tri_pair(CHUNK), _gla_masks()]
    if use_s0:
        in_specs += [pl.BlockSpec((None, None, hps, hd, hd), lambda s, h: (s, 0, h, 0, 0)),
                     pl.BlockSpec((None, None, hps, hd, hd), lambda s, h: (s, 1, h, 0, 0))]
        args += [s0, s0]
    y, st = pl.pallas_call(
        functools.partial(_hgrn_kernel, seq_len=seq_len, use_s0=use_s0, hps=hps),
        out_shape=[jax.ShapeDtypeStruct((n_seq * seq_len, n_heads * hd), BF16),
                   jax.ShapeDtypeStruct((n_seq, 2, n_heads, hd, hd), F32)],
        grid=(n_seq, nhb),
        in_specs=in_specs,
        out_specs=[pl.BlockSpec((seq_len, wd), lambda s, h: (s, h)),
                   pl.BlockSpec((None, 2, hps, hd, hd), lambda s, h: (s, 0, h, 0, 0))],
        scratch_shapes=[pltpu.VMEM((seq_len, wd), F32), pltpu.VMEM((seq_len, wd), F32)],
        compiler_params=_cparams("parallel", "parallel"),
    )(*args)
    return y, st


def _ssd_conv_kernel(x_ref, w_ref, b_ref, o_ref, *, seq_len, n_taps):
    x = x_ref[...]
    row = lax.broadcasted_iota(jnp.int32, x.shape, 0)
    half = n_taps // 2
    acc = x * w_ref[half:half + 1, :] + b_ref[...]
    for j in range(n_taps):
        d = j - half
        if d == 0:
            continue
        shifted = pltpu.roll(x, (-d) % seq_len, 0)
        valid = (row + d >= 0) & (row + d < seq_len)
        acc = acc + jnp.where(valid, shifted, 0.0) * w_ref[j:j + 1, :]
    o_ref[...] = _silu(acc)


def _ssd_conv(proj, conv_w, conv_b, col_block0, n_seq, seq_len, row_block0):
    n_taps, c = conv_w.shape
    tc = _pick_tile(math.gcd(c, col_block0 * LANES), max(256, (512 * 1024) // seq_len))
    assert (col_block0 * LANES) % tc == 0
    cb0 = col_block0 * LANES // tc
    return pl.pallas_call(
        functools.partial(_ssd_conv_kernel, seq_len=seq_len, n_taps=n_taps),
        out_shape=jax.ShapeDtypeStruct((n_seq * seq_len, c), F32),
        grid=(n_seq, c // tc),
        in_specs=[pl.BlockSpec((seq_len, tc), lambda s, j: (row_block0 + s, cb0 + j)),
                  pl.BlockSpec((n_taps, tc), lambda s, j: (0, j)),
                  pl.BlockSpec((1, tc), lambda s, j: (0, j))],
        out_specs=pl.BlockSpec((seq_len, tc), lambda s, j: (s, j)),
        compiler_params=_cparams("parallel", "parallel"),
    )(proj, conv_w, conv_b.reshape(1, c))


def _ssd_dt_kernel(d_ref, bias_ref, a_ref, o_ref):
    x = d_ref[...] + bias_ref[...]
    dt = jnp.maximum(x, 0.0) + jnp.log1p(jnp.exp(-jnp.abs(x)))
    o_ref[0] = dt.T
    o_ref[1] = (dt * a_ref[...]).T


def _ssd_dt(proj, dt_bias, a_log, col_block, n_seq, seq_len, row_block0):
    nh2 = dt_bias.size
    assert nh2 == LANES
    a_row = (-jnp.exp(a_log.astype(F32))).reshape(1, nh2)
    return pl.pallas_call(
        _ssd_dt_kernel,
        out_shape=jax.ShapeDtypeStruct((n_seq, 2, nh2, seq_len), F32),
        grid=(n_seq,),
        in_specs=[pl.BlockSpec((seq_len, nh2), lambda s: (row_block0 + s, col_block)),
                  pl.BlockSpec((1, nh2), lambda s: (0, 0)),
                  pl.BlockSpec((1, nh2), lambda s: (0, 0))],
        out_specs=pl.BlockSpec((None, 2, nh2, seq_len), lambda s: (s, 0, 0, 0)),
        compiler_params=_cparams("parallel"),
    )(proj, dt_bias.reshape(1, nh2).astype(F32), a_row)


def _ssd_expanders(hpg, hdim):
    r = np.arange(2 * LANES)[:, None] % LANES
    head = np.arange(hpg * hdim)[None, :] // hdim
    return jnp.asarray(np.stack([r == hpg + head, r == 2 * hpg + head]).astype(np.float32), BF16)


def _ssd_decays(dta, tri, rev):
    dt_t, a_t = dta[0], dta[1]
    hpg, c = dt_t.shape
    acs_t = _chunk_cumsum_rows(a_t, tri)
    r_last = 0 if rev else c - 1
    a_last = acs_t[:, r_last:r_last + 1]
    pad = jnp.zeros((LANES - 3 * hpg, c), F32)
    cols = jnp.concatenate([acs_t, dt_t * jnp.exp(a_last - acs_t), jnp.exp(acs_t), pad], axis=0).T
    hi = cols.astype(BF16)
    lo = (cols - hi.astype(F32)).astype(BF16)
    pieces = jnp.concatenate([hi, lo], axis=1)
    return dict(acs_t=acs_t, dt_t=dt_t, cdec=jnp.exp(a_last), cols=cols, pieces=pieces)


def _ssd_chunks(chains, hpg, hdim):
    c = chains[0]["xs"].shape[0]
    pair = LANES // hdim
    ti = lax.broadcasted_iota(jnp.int32, (c, c), 0)
    si = lax.broadcasted_iota(jnp.int32, (c, c), 1)
    lane = lax.broadcasted_iota(jnp.int32, (c, LANES), 1)
    sels = [(lane >= u * hdim) & (lane < (u + 1) * hdim) for u in range(pair)]
    for ch in chains:
        ch.update(ch["dec"])
        ch["causal"] = (si >= ti) if ch["rev"] else (si <= ti)
    for ch in chains:
        cmb = ch["cm"].astype(BF16)
        ch["cb"] = _dot_nt(cmb, ch["bm"].astype(BF16))
        ch["y_inter"] = _dot(cmb, ch["h"].astype(BF16))
        ch["bt"] = ch["bm"].T.astype(BF16)
        ch["dtd"] = _dot(ch["pieces"], ch["sel"][0])
        ch["dfs"] = _dot(ch["pieces"], ch["sel"][1])
        ch["ys"] = []
        ch["hs"] = []
    for p in range(hpg // pair):
        for ch in chains:
            cols, acs_t, dt_t = ch["cols"], ch["acs_t"], ch["dt_t"]
            lanes_p = slice(p * LANES, (p + 1) * LANES)
            ms = []
            cdec_row = None
            for u in range(pair):
                hh = p * pair + u
                seg = jnp.where(ch["causal"], cols[:, hh:hh + 1] - acs_t[hh:hh + 1, :], NEG_BIG)
                ms.append((ch["cb"] * jnp.exp(seg) * dt_t[hh:hh + 1, :]).astype(BF16))
                cd = ch["cdec"][hh:hh + 1, :]
                cdec_row = cd if cdec_row is None else jnp.where(sels[u][:1, :], cd, cdec_row)
            x2 = ch["xs"][:, lanes_p]
            rhs = [jnp.where(sels[u], x2, 0.0).astype(BF16) for u in range(pair)]
            y_intra = _dot(jnp.concatenate(ms, axis=1), jnp.concatenate(rhs, axis=0))
            ch["ys"].append(y_intra + ch["y_inter"][:, lanes_p] * ch["dfs"][:, lanes_p])
            upd = _dot(ch["bt"], (x2 * ch["dtd"][:, lanes_p]).astype(BF16))
            ch["hs"].append(ch["h"][:, lanes_p] * cdec_row + upd)
    return [(jnp.concatenate(ch["ys"], axis=1), jnp.concatenate(ch["hs"], axis=1)) for ch in chains]


def _chunk_cumsum_rows(a_t, tri):
    h = a_t.shape[0]
    parts = jnp.concatenate(_split3(a_t), axis=0).astype(BF16)
    p = _dot(parts, tri)
    return (p[2 * h:] + p[h:2 * h]) + p[:h]


def _ssd_kernel(*refs, seq_len, use_h0, hpg, hdim):
    if use_h0:
        (xs_ref, bm_ref, cm_ref, z_ref, dta_ref, dsk_ref, nw_ref, tri_ref, sel_ref, h0f_ref, h0b_ref,
         y_ref, st_ref, yf_ref, yb_ref) = refs
    else:
        (xs_ref, bm_ref, cm_ref, z_ref, dta_ref, dsk_ref, nw_ref, tri_ref, sel_ref,
         y_ref, st_ref, yf_ref, yb_ref) = refs
    nc = seq_len // CHUNK
    width = hpg * hdim
    n_state = bm_ref.shape[1]

    def rows_of(ci):
        return pl.ds(pl.multiple_of(ci * CHUNK, CHUNK), CHUNK)

    def decays(ci, rev):
        return _ssd_decays(dta_ref[:, 1 if rev else 0, ci], tri_ref[0 if rev else 1], rev)

    def body(i, carry):
        h_f, h_b, dec_f, dec_b = carry
        rf = rows_of(i)
        cb = nc - 1 - i
        rb = rows_of(cb)
        nxt_f = decays(jnp.minimum(i + 1, nc - 1), False)
        nxt_b = decays(jnp.maximum(cb - 1, 0), True)
        (y_f, h_f), (y_b, h_b) = _ssd_chunks(
            [dict(xs=xs_ref[rf, :], bm=bm_ref[rf, :], cm=cm_ref[rf, :], dec=dec_f, h=h_f, sel=sel_ref,
                  rev=False),
             dict(xs=xs_ref[rb, :], bm=bm_ref[rb, :], cm=cm_ref[rb, :], dec=dec_b, h=h_b, sel=sel_ref,
                  rev=True)],
            hpg, hdim)
        yf_ref[rf, :] = y_f
        yb_ref[rb, :] = y_b
        return h_f, h_b, nxt_f, nxt_b

    def load_state(ref):
        return ref[...].reshape(width, n_state).T

    zero = jnp.zeros((n_state, width), F32)
    init = (load_state(h0f_ref), load_state(h0b_ref)) if use_h0 else (zero, zero)
    h_f, h_b, _, _ = lax.fori_loop(0, nc, body, init + (decays(0, False), decays(nc - 1, True)))
    st_ref[0] = h_f.T.reshape(hpg, hdim, n_state)
    st_ref[1] = h_b.T.reshape(hpg, hdim, n_state)

    def finish(ci, carry):
        rows = rows_of(ci)
        y = yf_ref[rows, :] + yb_ref[rows, :] + dsk_ref[...] * xs_ref[rows, :]
        y = y * _silu(z_ref[rows, :])
        ms = jnp.mean(y * y, axis=-1, keepdims=True)
        y_ref[rows, :] = (y * lax.rsqrt(ms + NORM_EPS) * nw_ref[...]).astype(y_ref.dtype)
        return carry

    lax.fori_loop(0, nc, finish, 0)


def _ssd_core(xbc, proj, dta, d_cols, norm_w, h0, n_seq, seq_len, row_block0, n_groups, n_heads,
              hdim, n_state):
    use_h0 = h0 is not None
    hpg = n_heads // n_groups
    width = hpg * hdim
    inner = n_heads * hdim
    assert n_state == LANES and width % LANES == 0 and inner % width == 0
    nc = seq_len // CHUNK
    b_blk0 = inner // n_state
    c_blk0 = b_blk0 + n_groups
    in_specs = [pl.BlockSpec((seq_len, width), lambda s, g: (s, g)),
                pl.BlockSpec((seq_len, n_state), lambda s, g: (s, b_blk0 + g)),
                pl.BlockSpec((seq_len, n_state), lambda s, g: (s, c_blk0 + g)),
                pl.BlockSpec((seq_len, width), lambda s, g: (row_block0 + s, g)),
                pl.BlockSpec((None, 2, 2, None, nc, hpg, CHUNK), lambda s, g: (s, 0, 0, g, 0, 0, 0)),
                pl.BlockSpec((1, width), lambda s, g: (0, g)),
                pl.BlockSpec((1, width), lambda s, g: (0, g)),
                pl.BlockSpec((2, CHUNK, CHUNK), lambda s, g: (0, 0, 0)),
                pl.BlockSpec((2, 2 * LANES, width), lambda s, g: (0, 0, 0))]
    args = [xbc, xbc, xbc, proj, dta, d_cols, norm_w.reshape(1, inner), _tri_pair(CHUNK),
            _ssd_expanders(hpg, hdim)]
    if use_h0:
        st_in = (None, None, hpg, hdim, n_state)
        in_specs += [pl.BlockSpec(st_in, lambda s, g: (s, 0, g, 0, 0)),
                     pl.BlockSpec(st_in, lambda s, g: (s, 1, g, 0, 0))]
        args += [h0, h0]
    y, st = pl.pallas_call(
        functools.partial(_ssd_kernel, seq_len=seq_len, use_h0=use_h0, hpg=hpg, hdim=hdim),
        out_shape=[jax.ShapeDtypeStruct((n_seq * seq_len, inner), BF16),
                   jax.ShapeDtypeStruct((n_seq, 2, n_heads, hdim, n_state), F32)],
        grid=(n_seq, n_groups),
        in_specs=in_specs,
        out_specs=[pl.BlockSpec((seq_len, width), lambda s, g: (s, g)),
                   pl.BlockSpec((None, 2, hpg, hdim, n_state), lambda s, g: (s, 0, g, 0, 0))],
        scratch_shapes=[pltpu.VMEM((seq_len, width), F32), pltpu.VMEM((seq_len, width), F32)],
        compiler_params=_cparams("parallel", "parallel"),
    )(*args)
    return y, st


def _rope_tables(rows, head_dim):
    quarter = head_dim // 4
    inv = ROPE_THETA ** (-np.arange(quarter, dtype=np.float64) / quarter)
    t = np.arange(rows.dec_seq)
    ang_r = (t // GRID_W)[:, None] * inv[None, :]
    ang_c = (t % GRID_W)[:, None] * inv[None, :]
    cos = np.concatenate([np.cos(ang_r)] * 2 + [np.cos(ang_c)] * 2, axis=1)
    sin = np.concatenate([-np.sin(ang_r), np.sin(ang_r), -np.sin(ang_c), np.sin(ang_c)], axis=1)
    n_dec = (rows.total - rows.ctx_rows) // rows.dec_seq
    cos = np.concatenate([np.ones((rows.ctx_rows, head_dim))] + [cos] * n_dec, axis=0)
    sin = np.concatenate([np.zeros((rows.ctx_rows, head_dim))] + [sin] * n_dec, axis=0)
    return jnp.asarray(cos, F32), jnp.asarray(sin, F32)


def _qk_prep_kernel(p_ref, cos_ref, sin_ref, qn_ref, kn_ref, q_ref, k_ref, *, n_q, n_kv, scale):
    cos = cos_ref[...]
    sin = sin_ref[...]
    lane = lax.broadcasted_iota(jnp.int32, cos.shape, 1)
    first = (lane % (LANES // 2)) < (LANES // 4)

    def norm_rope(x, w):
        ms = jnp.mean(x * x, axis=-1, keepdims=True)
        y = x * lax.rsqrt(ms + NORM_EPS) * w
        partner = jnp.where(first, pltpu.roll(y, LANES - LANES // 4, 1), pltpu.roll(y, LANES // 4, 1))
        return y * cos + partner * sin

    for h in range(n_q):
        x = p_ref[:, h * LANES:(h + 1) * LANES]
        q_ref[:, h * LANES:(h + 1) * LANES] = (norm_rope(x, qn_ref[...]) * scale).astype(q_ref.dtype)
    for h in range(n_kv):
        x = p_ref[:, (n_q + h) * LANES:(n_q + h + 1) * LANES]
        k_ref[:, h * LANES:(h + 1) * LANES] = norm_rope(x, kn_ref[...])


def _qk_prep(proj, cos, sin, qn, kn, n_q, n_kv, rows):
    nt = proj.shape[0]
    hd = LANES
    tm = rows.row_tile(256)
    scale = float(hd) ** -0.5
    return pl.pallas_call(
        functools.partial(_qk_prep_kernel, n_q=n_q, n_kv=n_kv, scale=scale),
        out_shape=[jax.ShapeDtypeStruct((nt, n_q * hd), BF16),
                   jax.ShapeDtypeStruct((nt, n_kv * hd), F32)],
        grid=(nt // tm,),
        in_specs=[pl.BlockSpec((tm, proj.shape[1]), lambda i: (i, 0)),
                  pl.BlockSpec((tm, hd), lambda i: (i, 0)),
                  pl.BlockSpec((tm, hd), lambda i: (i, 0)),
                  pl.BlockSpec((1, hd), lambda i: (0, 0)),
                  pl.BlockSpec((1, hd), lambda i: (0, 0))],
        out_specs=[pl.BlockSpec((tm, n_q * hd), lambda i: (i, 0)),
                   pl.BlockSpec((tm, n_kv * hd), lambda i: (i, 0))],
        compiler_params=_cparams("parallel"),
    )(proj, cos, sin, qn.reshape(1, hd), kn.reshape(1, hd))


ATTN_KEY_CHUNK = 512


def _attn_kernel(*refs, grp, use_ctx):
    if use_ctx:
        q_ref, k_ref, v_ref, kc_ref, vc_ref, o_ref = refs
    else:
        q_ref, k_ref, v_ref, o_ref = refs
    tq = q_ref.shape[0]
    q = jnp.concatenate([q_ref[:, g * LANES:(g + 1) * LANES] for g in range(grp)], axis=0)
    chunks = [(kc_ref, vc_ref, 0, kc_ref.shape[0])] if use_ctx else []
    n_keys = k_ref.shape[0]
    kc = min(ATTN_KEY_CHUNK, n_keys)
    chunks += [(k_ref, v_ref, c0, kc) for c0 in range(0, n_keys, kc)]

    def scores(chunk):
        kr, _, c0, n = chunk
        return _dot_nt(q, kr[c0:c0 + n, :].astype(BF16))

    s_next = scores(chunks[0])
    m = l = acc = None
    for idx, (_, vr, c0, n) in enumerate(chunks):
        s = s_next
        if idx + 1 < len(chunks):
            s_next = scores(chunks[idx + 1])
        mj = jnp.max(s, axis=-1, keepdims=True)
        m_new = mj if m is None else jnp.maximum(m, mj)
        p = jnp.exp(s - m_new)
        pv = _dot(p.astype(BF16), vr[c0:c0 + n, :].astype(BF16))
        pl_sum = jnp.sum(p, axis=-1, keepdims=True)
        if m is None:
            l, acc = pl_sum, pv
        else:
            alpha = jnp.exp(m - m_new)
            l = alpha * l + pl_sum
            acc = alpha * acc + pv
        m = m_new
    o = acc / l
    for g in range(grp):
        o_ref[:, g * LANES:(g + 1) * LANES] = o[g * tq:(g + 1) * tq].astype(o_ref.dtype)


def _attention(qn, kn, proj, ctx_k, ctx_v, n_seq, seq_len, row_block0, n_q, n_kv):
    use_ctx = ctx_k is not None
    grp = n_q // n_kv
    hd = LANES
    tq = min(seq_len, 256)
    nqb = seq_len // tq
    v_blk0 = n_q + n_kv
    in_specs = [pl.BlockSpec((tq, grp * hd), lambda s, kv, i: ((row_block0 + s) * nqb + i, kv)),
                pl.BlockSpec((seq_len, hd), lambda s, kv, i: (row_block0 + s, kv)),
                pl.BlockSpec((seq_len, hd), lambda s, kv, i: (row_block0 + s, v_blk0 + kv))]
    args = [qn, kn, proj]
    if use_ctx:
        past = ctx_k.shape[1]
        in_specs += [pl.BlockSpec((None, past, hd), lambda s, kv, i: (s, 0, kv)),
                     pl.BlockSpec((None, past, hd), lambda s, kv, i: (s, 0, kv))]
        args += [ctx_k, ctx_v]
    return pl.pallas_call(
        functools.partial(_attn_kernel, grp=grp, use_ctx=use_ctx),
        out_shape=jax.ShapeDtypeStruct((n_seq * seq_len, n_q * hd), BF16),
        grid=(n_seq, n_kv, nqb),
        in_specs=in_specs,
        out_specs=pl.BlockSpec((tq, grp * hd), lambda s, kv, i: (s * nqb + i, kv)),
        compiler_params=_cparams("parallel", "parallel", "arbitrary"),
    )(*args)


def kernel(x_prompt, x_sample, state_hgrn, state_ssd, cache_k, cache_v, c, c_ctx, ada_w, ada_b, norm_w,
           hg_w_in, hg_lb_logits, hg_onorm, hg_w_o, ssd_w_in, ssd_conv_w, ssd_conv_b, ssd_a_log,
           ssd_dt_bias, ssd_d, ssd_norm, ssd_w_o, at_w_qkv, at_qn, at_kn, at_w_o, ff_w13, ff_w2,
           moe_router, moe_w13, moe_w2):
    batch, seq, d = x_prompt.shape
    n_dec, dec_seq, _ = x_sample.shape
    depth = ada_w.shape[0]
    rows = _Rows(batch * seq, dec_seq, n_dec)
    ctx_rows = rows.ctx_rows
    assert seq % CHUNK == 0 and dec_seq % CHUNK == 0 and dec_seq % seq == 0 and ctx_rows % dec_seq == 0

    hg_heads = d // LANES
    ssd_heads = ssd_a_log.shape[2]
    ssd_inner = ssd_w_o.shape[1]
    ssd_hdim = ssd_inner // ssd_heads
    ssd_nstate = state_ssd.shape[-1]
    ssd_groups = (ssd_conv_w.shape[2] - ssd_inner) // (2 * ssd_nstate)
    at_kv = cache_k.shape[3]
    at_heads = at_w_o.shape[1] // LANES
    past = cache_k.shape[2]

    x = jnp.concatenate([x_prompt.reshape(ctx_rows, d), x_sample.reshape(n_dec * dec_seq, d)], axis=0)

    n_sets = 1 + n_dec
    cond = jnp.concatenate([c_ctx[None, :], c], axis=0)
    cond = jnp.pad(cond, ((0, (-n_sets) % SUBLANES), (0, 0)))
    mod = _ada_mod(cond, ada_w, ada_b)[:, :n_sets].reshape(depth, n_sets, 6, d)
    mod = jnp.pad(mod, ((0, 0), (0, 0), (0, SUBLANES - 6), (0, 0)))

    lb_all = None
    cos_t = sin_t = None
    hg_states, ssd_states, k_list, v_list = [], [], [], []
    lat_blk_seq = ctx_rows // dec_seq

    for layer in range(depth):
        j = layer // 3
        kind = layer % 3
        mod_l = mod[layer]
        nw1 = norm_w[layer, 0]
        if kind == 0:
            if lb_all is None:
                pr = jax.nn.softmax(hg_lb_logits.astype(F32), axis=0)
                lb_all = jnp.cumsum(pr, axis=0) - pr[0]
            proj = _matmul(x, nw1, mod_l, 0, hg_w_in, j, F32, rows)
            lb_row = lb_all[j].reshape(1, -1)
            y_c, st_c = _hgrn_core(proj, lb_row, hg_onorm[j], None, batch, seq, 0, hg_heads)
            y_l, _ = _hgrn_core(proj, lb_row, hg_onorm[j], state_hgrn[:, j], n_dec, dec_seq,
                                lat_blk_seq, hg_heads)
            hg_states.append(st_c)
            x = _matmul_residual(y_c, y_l, hg_w_o, j, x, mod_l, 2, rows)
        elif kind == 1:
            proj = _matmul(x, nw1, mod_l, 0, ssd_w_in, j, F32, rows)
            gn2 = 2 * ssd_groups * ssd_nstate
            xbc_blk0 = ssd_inner // LANES
            dt_blk = (2 * ssd_inner + gn2) // LANES
            d_cols = jnp.repeat((ssd_d[j, 0] + ssd_d[j, 1]).astype(F32), ssd_hdim).reshape(1, ssd_inner)
            hpg = ssd_heads // ssd_groups
            ys = []
            for (n_s, s_len, rb0, h0) in ((batch, seq, 0, None),
                                          (n_dec, dec_seq, lat_blk_seq, state_ssd[:, j])):
                xbc = _ssd_conv(proj, ssd_conv_w[j], ssd_conv_b[j], xbc_blk0, n_s, s_len, rb0)
                dta = _ssd_dt(proj, ssd_dt_bias[j], ssd_a_log[j], dt_blk, n_s, s_len, rb0)
                nc = s_len // CHUNK
                dta = dta.reshape(n_s, 2, 2, ssd_groups, hpg, nc, CHUNK).transpose(0, 1, 2, 3, 5, 4, 6)
                y_p, st_p = _ssd_core(xbc, proj, dta, d_cols, ssd_norm[j], h0, n_s, s_len, rb0,
                                      ssd_groups, ssd_heads, ssd_hdim, ssd_nstate)
                ys.append(y_p)
                if h0 is None:
                    ssd_states.append(st_p)
            x = _matmul_residual(ys[0], ys[1], ssd_w_o, j, x, mod_l, 2, rows)
        else:
            proj = _matmul(x, nw1, mod_l, 0, at_w_qkv, j, F32, rows)
            if cos_t is None:
                cos_t, sin_t = _rope_tables(rows, LANES)
            qn, kn = _qk_prep(proj, cos_t, sin_t, at_qn[j], at_kn[j], at_heads, at_kv, rows)
            o_c = _attention(qn, kn, proj, None, None, batch, seq, 0, at_heads, at_kv)
            ck = cache_k[:, j].reshape(n_dec, past, at_kv * LANES)
            cv = cache_v[:, j].reshape(n_dec, past, at_kv * LANES)
            o_l = _attention(qn, kn, proj, ck, cv, n_dec, dec_seq, lat_blk_seq, at_heads, at_kv)
            k_list.append(kn[:ctx_rows].reshape(batch, seq, at_kv, LANES))
            v_list.append(proj[:ctx_rows, (at_heads + at_kv) * LANES:].reshape(batch, seq, at_kv, LANES))
            x = _matmul_residual(o_c, o_l, at_w_o, j, x, mod_l, 2, rows)

        if layer % 2 == 0:
            h2 = _norm_mod(x, norm_w[layer, 1], mod_l, 1, rows)
            act = _matmul_swiglu(h2, ff_w13, layer // 2)
            x = _matmul_residual(act, None, ff_w2, layer // 2, x, mod_l, 5, rows)
        else:
            e = layer // 2
            n_e = moe_router.shape[2]
            tm_e = min(MOE_TILE, rows.total)
            hp, rinfo = _router(x, norm_w[layer, 1], mod_l, moe_router[e], rows)
            src, dest, tile_e, n_used = _route_tables(rinfo, n_e, tm_e)
            ys = _moe_experts(hp, moe_w13[e].astype(BF16), moe_w2[e].astype(BF16), src, tile_e,
                              n_used, tm_e)
            x = _moe_combine(ys, dest, x, rinfo, mod_l, rows)

    y_prompt = x[:ctx_rows].reshape(batch, seq, d)
    y_sample = x[ctx_rows:].reshape(n_dec, dec_seq, d)
    return (y_prompt, y_sample, jnp.stack(hg_states, axis=1), jnp.stack(ssd_states, axis=1),
            jnp.stack(k_list, axis=1), jnp.stack(v_list, axis=1))
```
